```python
import math
import jax, jax.numpy as jnp
from jax import lax
import numpy as np

D_MODEL = 1024
BATCH = 32
SEQ = 2048
DEPTH = 1
DEC_BATCH = 32
DEC_SEQ = 64
PAST_LEN = 1024

CHUNK = 64
N_MEM = 256
EPS = 1e-6
HG_HEADS = 8
HG_KD = 128
HG_VD = 128
HG_WIDTH = HG_HEADS * HG_VD
GLA_BLOCK = 16
DF_HEADS = 8
DF_HD = 64
DF_VD = 2 * DF_HD
DF_WIDTH = DF_HEADS * DF_VD
Q_BLOCK = 128
MEM_HEADS = 4
MEM_HD = 256
MEM_WIDTH = MEM_HEADS * MEM_HD
N_BRANCH = 3
BR_WIDTH = HG_WIDTH
COL_SIZES = (HG_HEADS * HG_KD, HG_HEADS * HG_KD, HG_WIDTH, HG_WIDTH,
             DF_HEADS * 2 * DF_HD, DF_HEADS * 2 * DF_HD, DF_WIDTH, DF_WIDTH,
             MEM_WIDTH, MEM_WIDTH, N_BRANCH * D_MODEL)
IN_COLS = sum(COL_SIZES)

kernel_name = 'hgrn2_diffattn_gated_streaming_encoder_step'


def rms_norm(x, g):
    xf = x.astype(jnp.float32)
    y = xf * lax.rsqrt(jnp.mean(xf * xf, axis=-1, keepdims=True) + EPS)
    return (y * g.astype(jnp.float32)).astype(x.dtype)


def split_columns(a):
    bounds = tuple(int(b) for b in np.cumsum(COL_SIZES)[:-1])
    return jnp.split(a, bounds, axis=-1)


def hgrn2_recurrence(q, k, v, logf, s0):
    B, H, T, _ = q.shape
    pad = (-T) % GLA_BLOCK
    if pad:
        pw = ((0, 0), (0, 0), (0, pad), (0, 0))
        q, k, v, logf = (jnp.pad(a, pw) for a in (q, k, v, logf))
    nb = (T + pad) // GLA_BLOCK

    def to_blocks(a):
        return jnp.moveaxis(a.reshape(B, H, nb, GLA_BLOCK, a.shape[-1]), 2, 0)

    causal = jnp.tril(jnp.ones((GLA_BLOCK, GLA_BLOCK), dtype=bool))

    def step(s, blk):
        qb, kb, vb, gb = blk
        b = jnp.cumsum(gb, axis=2)
        b_end = b[:, :, -1:, :]
        q_dec = qb * jnp.exp(b)
        k_inv = kb * jnp.exp(-b)
        att = jnp.where(causal, jnp.einsum('bhtk,bhsk->bhts', q_dec, k_inv), 0.0)
        o = jnp.einsum('bhts,bhsv->bhtv', att, vb) + jnp.einsum('bhtk,bhkv->bhtv', q_dec, s)
        s = (jnp.exp(b_end[:, :, 0, :])[..., None] * s
             + jnp.einsum('bhsk,bhsv->bhkv', kb * jnp.exp(b_end - b), vb))
        return s, o

    s_T, o = lax.scan(step, s0, tuple(to_blocks(a) for a in (q, k, v, logf)))
    o = jnp.moveaxis(o, 0, 2).reshape(B, H, nb * GLA_BLOCK, -1)[:, :, :T]
    return o, s_T


def diff_softmax_pair(q, k, v, lam, mask):
    s = jnp.einsum('bqhcd,bshcd->bhcqs', q, k).astype(jnp.float32) * (DF_HD ** -0.5)
    if mask is not None:
        s = jnp.where(mask, s, -jnp.inf)
    p = jax.nn.softmax(s, axis=-1)
    a = p[:, :, 0] - lam * p[:, :, 1]
    return jnp.einsum('bhqs,bshv->bqhv', a.astype(v.dtype), v)


def diff_attention_prompt(q, k, v, lam):
    B, T = q.shape[:2]
    nqb = T // Q_BLOCK
    q_blocks = jnp.moveaxis(q.reshape(B, nqb, Q_BLOCK, DF_HEADS, 2, DF_HD), 1, 0)
    k_chunk = jnp.arange(T) // CHUNK

    def one_block(args):
        qi, i = args
        q_chunk = (i * Q_BLOCK + jnp.arange(Q_BLOCK)) // CHUNK
        mask = k_chunk[None, :] <= q_chunk[:, None]
        return diff_softmax_pair(qi, k, v, lam, mask)

    o = lax.map(one_block, (q_blocks, jnp.arange(nqb)))
    return jnp.moveaxis(o, 0, 1).reshape(B, T, DF_HEADS, DF_VD)


def memory_kv(mem, g_mem, w_mkv, g_mk):
    B, N, _ = mem.shape
    k, v = jnp.split(rms_norm(mem, g_mem) @ w_mkv, 2, axis=-1)
    k = rms_norm(k.reshape(B, N, MEM_HEADS, MEM_HD), g_mk)
    return k, v.reshape(B, N, MEM_HEADS, MEM_HD)


def memory_attention(q, k, v):
    s = jnp.einsum('bthd,bnhd->bhtn', q, k).astype(jnp.float32) * (MEM_HD ** -0.5)
    p = jax.nn.softmax(s, axis=-1)
    return jnp.einsum('bhtn,bnhd->bthd', p.astype(v.dtype), v)


def encoder_layer(x, s0, past_k, past_v, mem_k, mem_v, lb, lam_init,
                  g_norm, w_in, g_hg_out, g_dq, g_dk, lam_q1, lam_k1, lam_q2, lam_k2,
                  g_dsub, g_mq, w_branch, w_out):
    B, T, _ = x.shape
    f32 = jnp.float32
    xn = rms_norm(x, g_norm)
    hq, hf, hi, za, dq, dk, dv, zb, mq, zm, gr = split_columns(xn @ w_in)

    hf = hf.astype(f32)
    logf = jnp.log(lb + (1.0 - lb) * jax.nn.sigmoid(hf))
    hk = (1.0 - lb) * jax.nn.sigmoid(-hf)

    def heads(a):
        return a.astype(f32).reshape(B, T, HG_HEADS, -1).transpose(0, 2, 1, 3)

    o_hg, s_new = hgrn2_recurrence(heads(hq), heads(hk), heads(hi), heads(logf), s0.astype(f32))
    y_a = rms_norm(o_hg.transpose(0, 2, 1, 3), g_hg_out).reshape(B, T, HG_WIDTH).astype(x.dtype)

    q = rms_norm(dq.reshape(B, T, DF_HEADS, 2, DF_HD), g_dq)
    k = rms_norm(dk.reshape(B, T, DF_HEADS, 2, DF_HD), g_dk)
    v = dv.reshape(B, T, DF_HEADS, DF_VD)
    lam = (jnp.exp(jnp.sum(lam_q1.astype(f32) * lam_k1.astype(f32)))
           - jnp.exp(jnp.sum(lam_q2.astype(f32) * lam_k2.astype(f32))) + lam_init)
    if past_k is None:
        o_df = diff_attention_prompt(q, k, v, lam)
    else:
        k_all = jnp.concatenate([past_k.astype(k.dtype), k], axis=1)
        v_all = jnp.concatenate([past_v.astype(v.dtype), v], axis=1)
        o_df = diff_softmax_pair(q, k_all, v_all, lam, None)
    y_b = (rms_norm(o_df, g_dsub) * (1.0 - lam_init)).reshape(B, T, DF_WIDTH)

    q_m = rms_norm(mq.reshape(B, T, MEM_HEADS, MEM_HD), g_mq)
    y_m = memory_attention(q_m, mem_k.astype(x.dtype), mem_v.astype(x.dtype)).reshape(B, T, MEM_WIDTH)

    gates = jax.nn.sigmoid(gr.astype(f32)).astype(x.dtype).reshape(B, T, N_BRANCH, D_MODEL)
    h = 0
    for n, (y_n, z_n) in enumerate(((y_a, za), (y_b, zb), (y_m, zm))):
        h = h + gates[:, :, n] * ((y_n * jax.nn.silu(z_n)) @ w_branch[n])
    y = x + h @ w_out
    return y, s_new, k, v


def setup_inputs(seed: int = 0) -> dict:
    key = jax.random.key(seed)
    ks = jax.random.split(key, 26)

    def nrm(k, shape, scale=1.0):
        return scale * jax.random.normal(k, shape, jnp.float32)

    def gain(k, shape):
        return 1.0 + 0.02 * jax.random.normal(k, shape, jnp.float32)

    return {
        'x_prompt': nrm(ks[0], (BATCH, SEQ, D_MODEL)),
        'x_sample': nrm(ks[1], (DEC_BATCH, DEC_SEQ, D_MODEL)),
        'mem_prompt': nrm(ks[2], (BATCH, N_MEM, D_MODEL)),
        'cache_diff_k': nrm(ks[3], (DEPTH, DEC_BATCH, PAST_LEN, DF_HEADS, 2, DF_HD)),
        'cache_diff_v': nrm(ks[4], (DEPTH, DEC_BATCH, PAST_LEN, DF_HEADS, DF_VD)),
        'cache_mem_k': nrm(ks[5], (DEPTH, DEC_BATCH, N_MEM, MEM_HEADS, MEM_HD)),
        'cache_mem_v': nrm(ks[6], (DEPTH, DEC_BATCH, N_MEM, MEM_HEADS, MEM_HD)),
        'state_hgrn': nrm(ks[7], (DEPTH, DEC_BATCH, HG_HEADS, HG_KD, HG_VD), 0.3),
        'g_norm': gain(ks[8], (DEPTH, D_MODEL)),
        'w_in': nrm(ks[9], (DEPTH, D_MODEL, IN_COLS), D_MODEL ** -0.5),
        'hg_lb_logits': nrm(ks[10], (DEPTH + 1, HG_HEADS * HG_KD), 0.1),
        'g_hg_out': gain(ks[11], (DEPTH, HG_VD)),
        'g_dq': gain(ks[12], (DEPTH, DF_HD)),
        'g_dk': gain(ks[13], (DEPTH, DF_HD)),
        'lam_q1': nrm(ks[14], (DEPTH, DF_HD), 0.1),
        'lam_k1': nrm(ks[15], (DEPTH, DF_HD), 0.1),
        'lam_q2': nrm(ks[16], (DEPTH, DF_HD), 0.1),
        'lam_k2': nrm(ks[17], (DEPTH, DF_HD), 0.1),
        'g_dsub': gain(ks[18], (DEPTH, DF_VD)),
        'g_mem': gain(ks[19], (DEPTH, D_MODEL)),
        'w_mkv': nrm(ks[20], (DEPTH, D_MODEL, 2 * MEM_WIDTH), D_MODEL ** -0.5),
        'g_mq': gain(ks[21], (DEPTH, MEM_HD)),
        'g_mk': gain(ks[22], (DEPTH, MEM_HD)),
        'w_branch': nrm(ks[23], (DEPTH, N_BRANCH, BR_WIDTH, D_MODEL), BR_WIDTH ** -0.5),
        'w_out': nrm(ks[24], (DEPTH, D_MODEL, D_MODEL), D_MODEL ** -0.5),
    }


def reference(x_prompt, x_sample, mem_prompt, cache_diff_k, cache_diff_v, cache_mem_k, cache_mem_v,
              state_hgrn, g_norm, w_in, hg_lb_logits, g_hg_out, g_dq, g_dk, lam_q1, lam_k1,
              lam_q2, lam_k2, g_dsub, g_mem, w_mkv, g_mq, g_mk, w_branch, w_out):
    lower_bounds = jnp.cumsum(jax.nn.softmax(hg_lb_logits.astype(jnp.float32), axis=0), axis=0)
    s0_prompt = jnp.zeros((x_prompt.shape[0], HG_HEADS, HG_KD, HG_VD), jnp.float32)
    h_p, h_s = x_prompt, x_sample
    st_p, st_s, dk_p, dv_p, dk_s, dv_s, mk_p, mv_p = ([] for _ in range(8))
    for l in range(DEPTH):
        lam_init = 0.8 - 0.6 * math.exp(-0.3 * l)
        shared = (lower_bounds[l], lam_init, g_norm[l], w_in[l], g_hg_out[l], g_dq[l], g_dk[l],
                  lam_q1[l], lam_k1[l], lam_q2[l], lam_k2[l], g_dsub[l], g_mq[l], w_branch[l], w_out[l])
        mem_k, mem_v = memory_kv(mem_prompt, g_mem[l], w_mkv[l], g_mk[l])
        h_p, s_p, k_p, v_p = encoder_layer(h_p, s0_prompt, None, None, mem_k, mem_v, *shared)
        h_s, s_s, k_s, v_s = encoder_layer(h_s, state_hgrn[l], cache_diff_k[l], cache_diff_v[l],
                                           cache_mem_k[l], cache_mem_v[l], *shared)
        st_p.append(s_p)
        st_s.append(s_s)
        dk_p.append(k_p)
        dv_p.append(v_p)
        dk_s.append(k_s)
        dv_s.append(v_s)
        mk_p.append(mem_k)
        mv_p.append(mem_v)
    return (h_p, h_s, jnp.stack(st_p), jnp.stack(st_s), jnp.stack(dk_p), jnp.stack(dv_p),
            jnp.stack(dk_s), jnp.stack(dv_s), jnp.stack(mk_p), jnp.stack(mv_p))
```

```python
import functools

import jax
import jax.numpy as jnp
from jax import lax
from jax.experimental import pallas as pl
from jax.experimental.pallas import tpu as pltpu

F32 = jnp.float32
BF16 = jnp.bfloat16

EPS = 1e-6
D_MODEL = 1024
CHUNK = 64
GLA_BLOCK = 16
HG_HEADS, HG_D = 8, 128
DF_HEADS, DF_HD, DF_VD = 8, 64, 128
MEM_HEADS, MEM_HD = 4, 256
N_BRANCH = 3
N_COLGROUPS = 13
(CG_HQ, CG_HF, CG_HI, CG_ZA, CG_DQ, CG_DK, CG_DV, CG_ZB, CG_MQ, CG_ZM, CG_G0) = range(11)

LANE = 128
VMEM_LIMIT = 52 * 1024 * 1024

NT = (((1,), (1,)), ((), ()))
TN = (((0,), (0,)), ((), ()))


def _dot(a, b):
    return jnp.dot(a, b, preferred_element_type=F32)


def _dot_nt(a, b):
    return lax.dot_general(a, b, NT, preferred_element_type=F32)


def _dot_tn(a, b):
    return lax.dot_general(a, b, TN, preferred_element_type=F32)


def _rms(x, g):
    return x * lax.rsqrt(jnp.mean(x * x, axis=-1, keepdims=True) + EPS) * g


def _params(*sem):
    return pltpu.CompilerParams(dimension_semantics=sem, vmem_limit_bytes=VMEM_LIMIT)


def _memkv_kernel(mem_ref, g_ref, w_ref, gk_ref, k_ref, v_ref):
    xn = _rms(mem_ref[...], g_ref[...]).astype(BF16)
    kv = _dot(xn, w_ref[...])
    for h in range(MEM_HEADS):
        sl = slice(h * MEM_HD, (h + 1) * MEM_HD)
        k_ref[:, sl] = _rms(kv[:, sl], gk_ref[...])
    v_ref[...] = kv[:, D_MODEL:]


def _memory_kv(mem, g_mem, w_mkv, g_mk):
    n = mem.shape[0]
    tm = 512
    return pl.pallas_call(
        _memkv_kernel,
        grid=(n // tm,),
        in_specs=[
            pl.BlockSpec((tm, D_MODEL), lambda i: (i, 0)),
            pl.BlockSpec((1, D_MODEL), lambda i: (0, 0)),
            pl.BlockSpec((D_MODEL, 2 * D_MODEL), lambda i: (0, 0)),
            pl.BlockSpec((1, MEM_HD), lambda i: (0, 0)),
        ],
        out_specs=[pl.BlockSpec((tm, D_MODEL), lambda i: (i, 0))] * 2,
        out_shape=[jax.ShapeDtypeStruct((n, D_MODEL), F32)] * 2,
        compiler_params=_params("arbitrary"),
        name="memory_kv",
    )(mem, g_mem.reshape(1, -1), w_mkv.astype(BF16), g_mk.reshape(1, -1))


def _half_head_rms(acc, g_ref, scale):
    lane = lax.broadcasted_iota(jnp.int32, (1, LANE), 1)
    lo = lane < DF_HD
    outs = []
    for h in range(DF_HEADS):
        sl = slice(h * LANE, (h + 1) * LANE)
        xh = acc[:, sl]
        sq = xh * xh
        s_lo = jnp.sum(jnp.where(lo, sq, 0.0), axis=-1, keepdims=True)
        s_hi = jnp.sum(jnp.where(lo, 0.0, sq), axis=-1, keepdims=True)
        r = jnp.where(lo, lax.rsqrt(s_lo * (1.0 / DF_HD) + EPS), lax.rsqrt(s_hi * (1.0 / DF_HD) + EPS))
        outs.append(xh * r * (g_ref[:, sl] * scale))
    return outs


def _inproj_kernel(x_ref, gn_ref, w_ref, lbl_ref, gq_ref, gk_ref, gmq_ref,
                   p_ref, logf_ref, k_ref, v_ref, xn_scr):
    j = pl.program_id(1)

    @pl.when(j == 0)
    def _():
        xn_scr[...] = _rms(x_ref[...], gn_ref[...]).astype(BF16)

    acc = _dot(xn_scr[...], w_ref[...])

    @pl.when((j == CG_HQ) | (j == CG_HI))
    def _():
        p_ref[...] = acc.astype(BF16)

    @pl.when(j == CG_DV)
    def _():
        p_ref[...] = acc.astype(BF16)
        v_ref[...] = acc

    @pl.when(j == CG_HF)
    def _():
        l = lbl_ref[...]
        e = jnp.exp(l - jnp.max(l, axis=0, keepdims=True))
        lb = e[0:1] / jnp.sum(e, axis=0, keepdims=True)
        sp = jax.nn.sigmoid(acc)
        logf_ref[...] = jnp.log(lb + (1.0 - lb) * sp)
        p_ref[...] = ((1.0 - lb) * (1.0 - sp)).astype(BF16)

    @pl.when((j == CG_ZA) | (j == CG_ZB) | (j == CG_ZM))
    def _():
        p_ref[...] = (acc * jax.nn.sigmoid(acc)).astype(BF16)

    @pl.when(j >= CG_G0)
    def _():
        p_ref[...] = jax.nn.sigmoid(acc).astype(BF16)

    @pl.when(j == CG_DQ)
    def _():
        outs = _half_head_rms(acc, gq_ref, DF_HD ** -0.5)
        for h in range(DF_HEADS):
            p_ref[:, h * LANE:(h + 1) * LANE] = outs[h].astype(BF16)

    @pl.when(j == CG_DK)
    def _():
        outs = _half_head_rms(acc, gk_ref, 1.0)
        for h in range(DF_HEADS):
            sl = slice(h * LANE, (h + 1) * LANE)
            k_ref[:, sl] = outs[h]
            p_ref[:, sl] = outs[h].astype(BF16)

    @pl.when(j == CG_MQ)
    def _():
        for h in range(MEM_HEADS):
            sl = slice(h * MEM_HD, (h + 1) * MEM_HD)
            p_ref[:, sl] = (_rms(acc[:, sl], gmq_ref[...]) * (MEM_HD ** -0.5)).astype(BF16)


def _in_proj(x, g_norm, w_in, lb_logits, g_dq, g_dk, g_mq):
    n = x.shape[0]
    tm = min(512, n)
    row = lambda i, j: (i, 0)
    const = lambda i, j: (0, 0)
    return pl.pallas_call(
        _inproj_kernel,
        grid=(n // tm, N_COLGROUPS),
        in_specs=[
            pl.BlockSpec((tm, D_MODEL), row),
            pl.BlockSpec((1, D_MODEL), const),
            pl.BlockSpec((D_MODEL, D_MODEL), lambda i, j: (0, j)),
            pl.BlockSpec((2, D_MODEL), const),
            pl.BlockSpec((1, D_MODEL), const),
            pl.BlockSpec((1, D_MODEL), const),
            pl.BlockSpec((1, MEM_HD), const),
        ],
        out_specs=[
            pl.BlockSpec((tm, D_MODEL), lambda i, j: (i, j)),
            pl.BlockSpec((tm, D_MODEL), row),
            pl.BlockSpec((tm, D_MODEL), row),
            pl.BlockSpec((tm, D_MODEL), row),
        ],
        out_shape=[
            jax.ShapeDtypeStruct((n, N_COLGROUPS * D_MODEL), BF16),
            jax.ShapeDtypeStruct((n, D_MODEL), F32),
            jax.ShapeDtypeStruct((n, D_MODEL), F32),
            jax.ShapeDtypeStruct((n, D_MODEL), F32),
        ],
        scratch_shapes=[pltpu.VMEM((tm, D_MODEL), BF16)],
        compiler_params=_params("arbitrary", "arbitrary"),
        name="in_proj",
    )(x, g_norm.reshape(1, -1), w_in, lb_logits,
      jnp.tile(g_dq, 2 * DF_HEADS).reshape(1, -1), jnp.tile(g_dk, 2 * DF_HEADS).reshape(1, -1), g_mq.reshape(1, -1))


def _split_bf16(x):
    hi = x.astype(BF16)
    return hi, (x - hi.astype(F32)).astype(BF16)


def _hgrn_kernel(q_ref, k_ref, v_ref, z_ref, lf_ref, s0_ref, g_ref, y_ref, s_ref, *, seq, tc, hp):
    nb = tc // GLA_BLOCK
    row = lax.broadcasted_iota(jnp.int32, (tc, tc), 0)
    col = lax.broadcasted_iota(jnp.int32, (tc, tc), 1)
    causal = col <= row
    tril = jnp.where(causal, 1.0, 0.0).astype(BF16)
    ones = jnp.ones((tc, LANE), BF16)
    rowk = lax.broadcasted_iota(jnp.int32, (tc, LANE), 0)

    for h in range(hp):
        sl = slice(h * LANE, (h + 1) * LANE)

        def chunk(c, s, sl=sl):
            r0 = pl.multiple_of(c * tc, tc)
            rows = pl.ds(r0, tc)
            g_hi, g_lo = _split_bf16(lf_ref[rows, sl])
            b = _dot(tril, g_hi) + _dot(tril, g_lo)
            b_end_col = _dot_tn(g_hi, ones) + _dot_tn(g_lo, ones)
            b_end = b[tc - 1:tc, :]
            q = q_ref[rows, sl].astype(F32)
            k = k_ref[rows, sl].astype(F32)
            v = v_ref[rows, sl]
            o = _dot((q * jnp.exp(b)).astype(BF16), s.astype(BF16))
            k_dec = (k * jnp.exp(b_end - b)).astype(BF16)
            s_new = jnp.exp(b_end_col) * s + _dot_tn(k_dec, v)
            att = []
            for n in range(nb):
                lo, hi = n * GLA_BLOCK, (n + 1) * GLA_BLOCK
                g_n = b[lo - 1:lo, :] if n else jnp.zeros((1, LANE), F32)
                q_dec = (q[lo:hi] * jnp.exp(b[lo:hi] - g_n)).astype(BF16)
                k_inv = (k * jnp.exp(jnp.where(rowk < hi, g_n - b, 0.0))).astype(BF16)
                att.append(_dot_nt(q_dec, k_inv))
            a = jnp.where(causal, jnp.concatenate(att, axis=0), 0.0).astype(BF16)
            o = o + _dot(a, v)
            y = _rms(o, g_ref[...]) * z_ref[rows, sl].astype(F32)
            y_ref[rows, sl] = y.astype(BF16)
            return s_new

        s_ref[0, h] = lax.fori_loop(0, seq // tc, chunk, s0_ref[0, h])


def _hgrn(p, logf, s0, g_hg_out, batch, seq):
    hp = 2
    tc = min(128, seq)
    nhp = HG_HEADS // hp
    w = hp * LANE

    def col(cg):
        return pl.BlockSpec((seq, w), lambda b, h: (b, cg * nhp + h))

    st = pl.BlockSpec((1, hp, HG_D, HG_D), lambda b, h: (b, h, 0, 0))
    return pl.pallas_call(
        functools.partial(_hgrn_kernel, seq=seq, tc=tc, hp=hp),
        grid=(batch, nhp),
        in_specs=[col(CG_HQ), col(CG_HF), col(CG_HI), col(CG_ZA),
                  pl.BlockSpec((seq, w), lambda b, h: (b, h)), st,
                  pl.BlockSpec((1, HG_D), lambda b, h: (0, 0))],
        out_specs=[pl.BlockSpec((seq, w), lambda b, h: (b, h)), st],
        out_shape=[jax.ShapeDtypeStruct((batch * seq, D_MODEL), BF16),
                   jax.ShapeDtypeStruct((batch, HG_HEADS, HG_D, HG_D), F32)],
        compiler_params=_params("arbitrary", "arbitrary"),
        name="hgrn",
    )(p, p, p, p, logf, s0, g_hg_out.reshape(1, -1))


def _diffattn_kernel(lam_ref, q_ref, k_ref, v_ref, z_ref, g_ref, y_ref, s_scr,
                     *, n_q, tq, tk, hp, causal, s_len, n_valid, lam_init):
    lv = lam_ref[...]
    lam = (jnp.exp(jnp.sum(lv[0:1] * lv[1:2], axis=-1, keepdims=True))
           - jnp.exp(jnp.sum(lv[2:3] * lv[3:4], axis=-1, keepdims=True)) + lam_init)
    lane = lax.broadcasted_iota(jnp.int32, (1, LANE), 1)
    first = lane < DF_HD
    row = lax.broadcasted_iota(jnp.int32, (tq, tk), 0)
    col = lax.broadcasted_iota(jnp.int32, (tq, tk), 1)
    n_lt = tk // LANE
    zero = jnp.zeros((), BF16)

    def lane_fold(x, op):
        r = x[:, :LANE]
        for t in range(1, n_lt):
            r = op(r, x[:, t * LANE:(t + 1) * LANE])
        return r

    for h in range(hp):
        sl = slice(h * LANE, (h + 1) * LANE)

        def q_tile(i, carry, sl=sl):
            r0 = pl.multiple_of(i * tq, tq)
            rows = pl.ds(r0, tq)
            q = q_ref[rows, sl]
            qs = jnp.concatenate([jnp.where(first, q, zero), jnp.where(first, zero, q)], axis=0)
            if causal:
                n_full = (i * tq) // tk
                visible = ((n_full * tk + col) // CHUNK) <= ((r0 + row) // CHUNK)
            else:
                n_full = (n_valid // tk) if n_valid % tk else (n_valid // tk - 1)
                visible = (n_full * tk + col) < n_valid
            bias = jnp.where(visible, 0.0, -jnp.inf)
            bias2 = jnp.concatenate([bias, bias], axis=0)

            def scores(kt):
                return _dot_nt(qs, k_ref[pl.ds(pl.multiple_of(kt * tk, tk), tk), sl])

            def pass_a(kt, mp):
                s = scores(kt)
                s_scr[kt] = s
                return jnp.maximum(mp, lane_fold(s, jnp.maximum))

            mp = lax.fori_loop(0, n_full, pass_a, jnp.full((2 * tq, LANE), -jnp.inf, F32))
            s = scores(n_full) + bias2
            s_scr[n_full] = s
            m = jnp.max(jnp.maximum(mp, lane_fold(s, jnp.maximum)), axis=-1, keepdims=True)

            def pass_b(kt, lp):
                p = jnp.exp(s_scr[kt] - m)
                s_scr[kt] = p
                return lp + lane_fold(p, jnp.add)

            lp = lax.fori_loop(0, n_full + 1, pass_b, jnp.zeros((2 * tq, LANE), F32))
            rinv = 1.0 / jnp.sum(lp, axis=-1, keepdims=True)
            r1 = rinv[:tq]
            r2 = rinv[tq:] * lam

            def pass_c(kt, acc):
                p = s_scr[kt]
                a = (p[:tq] * r1 - p[tq:] * r2).astype(BF16)
                return acc + _dot(a, v_ref[pl.ds(pl.multiple_of(kt * tk, tk), tk), sl])

            o = lax.fori_loop(0, n_full + 1, pass_c, jnp.zeros((tq, LANE), F32))
            y = _rms(o, g_ref[...]) * (1.0 - lam_init) * z_ref[rows, sl].astype(F32)
            y_ref[rows, sl] = y.astype(BF16)
            return carry

        lax.fori_loop(0, n_q, q_tile, 0)


def _diff_attn(lam_vecs, q_arr, q_cg, k_arr, k_cg, v_arr, v_cg, z_arr, z_cg, g_dsub,
               *, batch, seq, s_len, n_valid, causal, tq, tk, lam_init):
    hp = 2
    nhp = DF_HEADS // hp
    w = hp * LANE

    def col(rows, cg):
        return pl.BlockSpec((rows, w), lambda b, h: (b, cg * nhp + h))

    kern = functools.partial(_diffattn_kernel, n_q=seq // tq, tq=tq, tk=tk, hp=hp, causal=causal,
                             s_len=s_len, n_valid=n_valid, lam_init=lam_init)
    return pl.pallas_call(
        kern,
        grid=(batch, nhp),
        in_specs=[pl.BlockSpec((4, DF_HD), lambda b, h: (0, 0)),
                  col(seq, q_cg), col(s_len, k_cg), col(s_len, v_cg), col(seq, z_cg),
                  pl.BlockSpec((1, DF_VD), lambda b, h: (0, 0))],
        out_specs=pl.BlockSpec((seq, w), lambda b, h: (b, h)),
        out_shape=jax.ShapeDtypeStruct((batch * seq, D_MODEL), BF16),
        scratch_shapes=[pltpu.VMEM((s_len // tk, 2 * tq, tk), F32)],
        compiler_params=_params("arbitrary", "arbitrary"),
        name="diff_attn",
    )(lam_vecs, q_arr, k_arr, v_arr, z_arr, g_dsub.reshape(1, -1))


def _merge_kernel(x_ref, ya_ref, yb_ref, qm_ref, zm_ref, g0_ref, g1_ref, g2_ref, mk_ref, mv_ref,
                  wb_ref, wo_ref, y_ref):
    ym = []
    for h in range(MEM_HEADS):
        sl = slice(h * MEM_HD, (h + 1) * MEM_HD)
        s = _dot_nt(qm_ref[:, sl], mk_ref[:, sl].astype(BF16))
        p = jnp.exp(s - jnp.max(s, axis=-1, keepdims=True))
        rinv = 1.0 / jnp.sum(p, axis=-1, keepdims=True)
        o = _dot(p.astype(BF16), mv_ref[:, sl].astype(BF16)) * rinv
        ym.append((o * zm_ref[:, sl].astype(F32)).astype(BF16))
    ym = jnp.concatenate(ym, axis=-1)
    hsum = (g0_ref[...].astype(F32) * _dot(ya_ref[...], wb_ref[0])
            + g1_ref[...].astype(F32) * _dot(yb_ref[...], wb_ref[1])
            + g2_ref[...].astype(F32) * _dot(ym, wb_ref[2]))
    y_ref[...] = x_ref[...] + _dot(hsum.astype(BF16), wo_ref[...])


def _merge(x, ya, yb, p, mem_k, mem_v, w_branch, w_out, *, seq, n_mem):
    n = x.shape[0]
    tm = min(256, seq)
    per_b = seq // tm
    row = lambda i: (i, 0)

    def col(cg):
        return pl.BlockSpec((tm, D_MODEL), lambda i: (i, cg))

    mem = pl.BlockSpec((n_mem, D_MODEL), lambda i: (i // per_b, 0))
    return pl.pallas_call(
        _merge_kernel,
        grid=(n // tm,),
        in_specs=[pl.BlockSpec((tm, D_MODEL), row), pl.BlockSpec((tm, D_MODEL), row), pl.BlockSpec((tm, D_MODEL), row),
                  col(CG_MQ), col(CG_ZM), col(CG_G0), col(CG_G0 + 1), col(CG_G0 + 2), mem, mem,
                  pl.BlockSpec((N_BRANCH, D_MODEL, D_MODEL), lambda i: (0, 0, 0)),
                  pl.BlockSpec((D_MODEL, D_MODEL), lambda i: (0, 0))],
        out_specs=pl.BlockSpec((tm, D_MODEL), row),
        out_shape=jax.ShapeDtypeStruct((n, D_MODEL), F32),
        compiler_params=_params("arbitrary"),
        name="merge",
    )(x, ya, yb, p, p, p, p, p, mem_k, mem_v, w_branch, w_out)


def _layer(x, s0, past_k, past_v, mem_k, mem_v, lam_init, w, n_mem):
    batch, seq, _ = x.shape
    n = batch * seq
    xf = x.reshape(n, D_MODEL)
    p, logf, k32, v32 = _in_proj(xf, w["g_norm"], w["w_in"], w["lb_logits"], w["g_dq"], w["g_dk"], w["g_mq"])
    ya, s_new = _hgrn(p, logf, s0, w["g_hg_out"], batch, seq)
    if past_k is None:
        yb = _diff_attn(w["lam_vecs"], p, CG_DQ, p, CG_DK, p, CG_DV, p, CG_ZB, w["g_dsub"],
                        batch=batch, seq=seq, s_len=seq, n_valid=seq, causal=True, tq=128, tk=256, lam_init=lam_init)
    else:
        past = past_k.shape[1]
        tk = LANE
        n_valid = past + seq
        s_len = -(-n_valid // tk) * tk
        pad = jnp.zeros((batch, s_len - n_valid, D_MODEL), BF16)
        k_all = jnp.concatenate([past_k.reshape(batch, past, D_MODEL).astype(BF16),
                                 p[:, CG_DK * D_MODEL:(CG_DK + 1) * D_MODEL].reshape(batch, seq, D_MODEL), pad], axis=1)
        v_all = jnp.concatenate([past_v.reshape(batch, past, D_MODEL).astype(BF16),
                                 p[:, CG_DV * D_MODEL:(CG_DV + 1) * D_MODEL].reshape(batch, seq, D_MODEL), pad], axis=1)
        yb = _diff_attn(w["lam_vecs"], p, CG_DQ, k_all.reshape(batch * s_len, D_MODEL), 0,
                        v_all.reshape(batch * s_len, D_MODEL), 0, p, CG_ZB, w["g_dsub"],
                        batch=batch, seq=seq, s_len=s_len, n_valid=n_valid, causal=False, tq=seq, tk=tk,
                        lam_init=lam_init)
    y = _merge(xf, ya, yb, p, mem_k, mem_v, w["w_branch"], w["w_out"], seq=seq, n_mem=n_mem)
    return (y.reshape(batch, seq, D_MODEL), s_new,
            k32.reshape(batch, seq, DF_HEADS, 2, DF_HD), v32.reshape(batch, seq, DF_HEADS, DF_VD))


def kernel(x_prompt, x_sample, mem_prompt, cache_diff_k, cache_diff_v, cache_mem_k, cache_mem_v, state_hgrn,
           g_norm, w_in, hg_lb_logits, g_hg_out, g_dq, g_dk, lam_q1, lam_k1, lam_q2, lam_k2, g_dsub, g_mem,
           w_mkv, g_mq, g_mk, w_branch, w_out):
    depth = g_norm.shape[0]
    assert depth == 1 and hg_lb_logits.shape[0] == 2, "single-layer step only"
    batch, n_mem = mem_prompt.shape[:2]
    dec_batch = x_sample.shape[0]
    lam_init = 0.8 - 0.6
    w = dict(g_norm=g_norm[0], w_in=w_in[0].astype(BF16), lb_logits=hg_lb_logits, g_hg_out=g_hg_out[0],
             g_dq=g_dq[0], g_dk=g_dk[0], g_mq=g_mq[0], g_dsub=g_dsub[0],
             lam_vecs=jnp.stack([lam_q1[0], lam_k1[0], lam_q2[0], lam_k2[0]]),
             w_branch=w_branch[0].astype(BF16), w_out=w_out[0].astype(BF16))

    mem_k, mem_v = _memory_kv(mem_prompt.reshape(batch * n_mem, D_MODEL), g_mem[0], w_mkv[0], g_mk[0])
    s0 = jnp.zeros((batch, HG_HEADS, HG_D, HG_D), F32)
    y_p, s_p, k_p, v_p = _layer(x_prompt, s0, None, None, mem_k, mem_v, lam_init, w, n_mem)
    y_s, s_s, k_s, v_s = _layer(x_sample, state_hgrn[0], cache_diff_k[0], cache_diff_v[0],
                                cache_mem_k[0].reshape(dec_batch * n_mem, D_MODEL),
                                cache_mem_v[0].reshape(dec_batch * n_mem, D_MODEL), lam_init, w, n_mem)
    mem_shape = (1, batch, n_mem, MEM_HEADS, MEM_HD)
    return (y_p, y_s, s_p[None], s_s[None], k_p[None], v_p[None], k_s[None], v_s[None],
            mem_k.reshape(mem_shape), mem_v.reshape(mem_shape))
```

```python
import functools

import jax
import jax.numpy as jnp
from jax import lax
from jax.experimental import pallas as pl
from jax.experimental.pallas import tpu as pltpu

F32 = jnp.float32
BF16 = jnp.bfloat16

EPS = 1e-6
D_MODEL = 1024
CHUNK = 64
GLA_BLOCK = 16
HG_HEADS, HG_D = 8, 128
DF_HEADS, DF_HD, DF_VD = 8, 64, 128
MEM_HEADS, MEM_HD = 4, 256
N_BRANCH = 3
N_COLGROUPS = 13
(CG_HQ, CG_HF, CG_HI, CG_ZA, CG_DQ, CG_DK, CG_DV, CG_ZB, CG_MQ, CG_ZM, CG_G0) = range(11)

LANE = 128
VMEM_LIMIT = 52 * 1024 * 1024

NT = (((1,), (1,)), ((), ()))
TN = (((0,), (0,)), ((), ()))


def _dot(a, b):
    return jnp.dot(a, b, preferred_element_type=F32)


def _dot_nt(a, b):
    return lax.dot_general(a, b, NT, preferred_element_type=F32)


def _dot_tn(a, b):
    return lax.dot_general(a, b, TN, preferred_element_type=F32)


def _sigmoid(x):
    return 0.5 * jnp.tanh(0.5 * x) + 0.5


def _rms(x, g):
    return x * lax.rsqrt(jnp.mean(x * x, axis=-1, keepdims=True) + EPS) * g


def _params(*sem):
    return pltpu.CompilerParams(dimension_semantics=sem, vmem_limit_bytes=VMEM_LIMIT)


def _memkv_kernel(mem_ref, g_ref, w_ref, gk_ref, k_ref, v_ref):
    xn = _rms(mem_ref[...], g_ref[...]).astype(BF16)
    kv = _dot(xn, w_ref[...])
    for h in range(MEM_HEADS):
        sl = slice(h * MEM_HD, (h + 1) * MEM_HD)
        k_ref[:, sl] = _rms(kv[:, sl], gk_ref[...])
    v_ref[...] = kv[:, D_MODEL:]


def _memory_kv(mem, g_mem, w_mkv, g_mk):
    n = mem.shape[0]
    tm = 512
    return pl.pallas_call(
        _memkv_kernel,
        grid=(n // tm,),
        in_specs=[
            pl.BlockSpec((tm, D_MODEL), lambda i: (i, 0)),
            pl.BlockSpec((1, D_MODEL), lambda i: (0, 0)),
            pl.BlockSpec((D_MODEL, 2 * D_MODEL), lambda i: (0, 0)),
            pl.BlockSpec((1, MEM_HD), lambda i: (0, 0)),
        ],
        out_specs=[pl.BlockSpec((tm, D_MODEL), lambda i: (i, 0))] * 2,
        out_shape=[jax.ShapeDtypeStruct((n, D_MODEL), F32)] * 2,
        compiler_params=_params("arbitrary"),
        name="memory_kv",
    )(mem, g_mem.reshape(1, -1), w_mkv.astype(BF16), g_mk.reshape(1, -1))


def _half_head_rms(acc, g_ref, scale):
    lane = lax.broadcasted_iota(jnp.int32, (1, LANE), 1)
    lo = lane < DF_HD
    outs = []
    for h in range(DF_HEADS):
        sl = slice(h * LANE, (h + 1) * LANE)
        xh = acc[:, sl]
        sq = xh * xh
        s_lo = jnp.sum(jnp.where(lo, sq, 0.0), axis=-1, keepdims=True)
        s_hi = jnp.sum(jnp.where(lo, 0.0, sq), axis=-1, keepdims=True)
        r = jnp.where(lo, lax.rsqrt(s_lo * (1.0 / DF_HD) + EPS), lax.rsqrt(s_hi * (1.0 / DF_HD) + EPS))
        outs.append(xh * r * (g_ref[:, sl] * scale))
    return outs


def _inproj_kernel(x_ref, gn_ref, w_ref, lbl_ref, gq_ref, gk_ref, gmq_ref,
                   p_ref, logf_ref, k_ref, v_ref, xn_scr, *, k_transposed):
    j = pl.program_id(1)

    @pl.when(j == 0)
    def _():
        xn_scr[...] = _rms(x_ref[...], gn_ref[...]).astype(BF16)

    def proj():
        return _dot(xn_scr[...], w_ref[...])

    @pl.when((j == CG_HQ) | (j == CG_HI))
    def _():
        p_ref[...] = proj().astype(BF16)

    @pl.when(j == CG_DV)
    def _():
        acc = proj()
        p_ref[...] = acc.astype(BF16)
        v_ref[...] = acc

    @pl.when(j == CG_HF)
    def _():
        l = lbl_ref[...]
        e = jnp.exp(l - jnp.max(l, axis=0, keepdims=True))
        lb = e[0:1] / jnp.sum(e, axis=0, keepdims=True)
        sp = _sigmoid(proj())
        logf_ref[...] = jnp.log(lb + (1.0 - lb) * sp)
        p_ref[...] = ((1.0 - lb) * (1.0 - sp)).astype(BF16)

    @pl.when((j == CG_ZA) | (j == CG_ZB) | (j == CG_ZM))
    def _():
        acc = proj()
        p_ref[...] = (acc * _sigmoid(acc)).astype(BF16)

    @pl.when(j >= CG_G0)
    def _():
        p_ref[...] = _sigmoid(proj()).astype(BF16)

    @pl.when(j == CG_DQ)
    def _():
        outs = _half_head_rms(proj(), gq_ref, DF_HD ** -0.5)
        for h in range(DF_HEADS):
            p_ref[:, h * LANE:(h + 1) * LANE] = outs[h].astype(BF16)

    @pl.when(j == CG_DK)
    def _():
        outs = _half_head_rms(proj(), gk_ref, 1.0)
        for h in range(DF_HEADS):
            sl = slice(h * LANE, (h + 1) * LANE)
            p_ref[:, sl] = outs[h].astype(BF16)
            if k_transposed:
                k_ref[0, sl, :] = outs[h].T
            else:
                k_ref[:, sl] = outs[h]

    @pl.when(j == CG_MQ)
    def _():
        acc = proj()
        for h in range(MEM_HEADS):
            sl = slice(h * MEM_HD, (h + 1) * MEM_HD)
            p_ref[:, sl] = (_rms(acc[:, sl], gmq_ref[...]) * (MEM_HD ** -0.5)).astype(BF16)


def _in_proj(x, g_norm, w_in, lb_logits, g_dq, g_dk, g_mq, *, seq, k_transposed):
    n = x.shape[0]
    tm = min(512, seq) if k_transposed else min(512, n)
    row = lambda i, j: (i, 0)
    const = lambda i, j: (0, 0)
    if k_transposed:
        per_b = seq // tm
        k_spec = pl.BlockSpec((1, D_MODEL, tm), lambda i, j: (i // per_b, 0, i % per_b))
        k_shape = jax.ShapeDtypeStruct((n // seq, D_MODEL, seq), F32)
    else:
        k_spec = pl.BlockSpec((tm, D_MODEL), row)
        k_shape = jax.ShapeDtypeStruct((n, D_MODEL), F32)
    return pl.pallas_call(
        functools.partial(_inproj_kernel, k_transposed=k_transposed),
        grid=(n // tm, N_COLGROUPS),
        in_specs=[
            pl.BlockSpec((tm, D_MODEL), row),
            pl.BlockSpec((1, D_MODEL), const),
            pl.BlockSpec((D_MODEL, D_MODEL), lambda i, j: (0, j)),
            pl.BlockSpec((2, D_MODEL), const),
            pl.BlockSpec((1, D_MODEL), const),
            pl.BlockSpec((1, D_MODEL), const),
            pl.BlockSpec((1, MEM_HD), const),
        ],
        out_specs=[
            pl.BlockSpec((tm, D_MODEL), lambda i, j: (i, j)),
            pl.BlockSpec((tm, D_MODEL), row),
            k_spec,
            pl.BlockSpec((tm, D_MODEL), row),
        ],
        out_shape=[
            jax.ShapeDtypeStruct((n, N_COLGROUPS * D_MODEL), BF16),
            jax.ShapeDtypeStruct((n, D_MODEL), F32),
            k_shape,
            jax.ShapeDtypeStruct((n, D_MODEL), F32),
        ],
        scratch_shapes=[pltpu.VMEM((tm, D_MODEL), BF16)],
        compiler_params=_params("arbitrary", "arbitrary"),
        name="in_proj",
    )(x, g_norm.reshape(1, -1), w_in, lb_logits,
      jnp.tile(g_dq, 2 * DF_HEADS).reshape(1, -1), jnp.tile(g_dk, 2 * DF_HEADS).reshape(1, -1), g_mq.reshape(1, -1))


def _split_bf16(x):
    hi = x.astype(BF16)
    return hi, (x - hi.astype(F32)).astype(BF16)


def _hgrn_kernel(q_ref, k_ref, v_ref, z_ref, lf_ref, s0_ref, g_ref, y_ref, s_ref, *, seq, tc, hp):
    nb = tc // GLA_BLOCK
    row = lax.broadcasted_iota(jnp.int32, (tc, tc), 0)
    col = lax.broadcasted_iota(jnp.int32, (tc, tc), 1)
    causal = col <= row
    tril = jnp.where(causal, 1.0, 0.0).astype(BF16)
    ones = jnp.ones((tc, LANE), BF16)
    rowk = lax.broadcasted_iota(jnp.int32, (tc, LANE), 0)

    s_ref[...] = s0_ref[...]

    def chunk(c, carry):
        rows = pl.ds(pl.multiple_of(c * tc, tc), tc)
        for h in range(hp):
            sl = slice(h * LANE, (h + 1) * LANE)
            s = s_ref[0, h]
            g_hi, g_lo = _split_bf16(lf_ref[rows, sl])
            b = _dot(tril, g_hi) + _dot(tril, g_lo)
            b_end_col = _dot_tn(g_hi, ones) + _dot_tn(g_lo, ones)
            b_end = b[tc - 1:tc, :]
            q = q_ref[rows, sl].astype(F32)
            k = k_ref[rows, sl].astype(F32)
            v = v_ref[rows, sl]
            o = _dot((q * jnp.exp(b)).astype(BF16), s.astype(BF16))
            k_dec = (k * jnp.exp(b_end - b)).astype(BF16)
            s_ref[0, h] = jnp.exp(b_end_col) * s + _dot_tn(k_dec, v)
            att = []
            for n in range(nb):
                lo, hi = n * GLA_BLOCK, (n + 1) * GLA_BLOCK
                g_n = b[lo - 1:lo, :] if n else jnp.zeros((1, LANE), F32)
                q_dec = (q[lo:hi] * jnp.exp(b[lo:hi] - g_n)).astype(BF16)
                k_inv = (k * jnp.exp(jnp.where(rowk < hi, g_n - b, 0.0))).astype(BF16)
                att.append(_dot_nt(q_dec, k_inv))
            a = jnp.where(causal, jnp.concatenate(att, axis=0), 0.0).astype(BF16)
            o = o + _dot(a, v)
            y = _rms(o, g_ref[...]) * z_ref[rows, sl].astype(F32)
            y_ref[rows, sl] = y.astype(BF16)
        return carry

    lax.fori_loop(0, seq // tc, chunk, 0)


def _hgrn(p, logf, s0, g_hg_out, batch, seq):
    hp = 4
    tc = min(128, seq)
    nhp = HG_HEADS // hp
    w = hp * LANE

    def col(cg):
        return pl.BlockSpec((seq, w), lambda b, h: (b, cg * nhp + h))

    st = pl.BlockSpec((1, hp, HG_D, HG_D), lambda b, h: (b, h, 0, 0))
    return pl.pallas_call(
        functools.partial(_hgrn_kernel, seq=seq, tc=tc, hp=hp),
        grid=(batch, nhp),
        in_specs=[col(CG_HQ), col(CG_HF), col(CG_HI), col(CG_ZA),
                  pl.BlockSpec((seq, w), lambda b, h: (b, h)), st,
                  pl.BlockSpec((1, HG_D), lambda b, h: (0, 0))],
        out_specs=[pl.BlockSpec((seq, w), lambda b, h: (b, h)), st],
        out_shape=[jax.ShapeDtypeStruct((batch * seq, D_MODEL), BF16),
                   jax.ShapeDtypeStruct((batch, HG_HEADS, HG_D, HG_D), F32)],
        compiler_params=_params("arbitrary", "arbitrary"),
        name="hgrn",
    )(p, p, p, p, logf, s0, g_hg_out.reshape(1, -1))


def _diffattn_kernel(lam_ref, q_ref, k_ref, v_ref, z_ref, g_ref, y_ref, s_scr, a_scr, vt_scr,
                     *, n_q, tq, tqp, rc, bt, s_len, n_valid, causal, lam_init):
    w = 2 * tqp
    lv = lam_ref[...]
    lam = (jnp.exp(jnp.sum(lv[0:1] * lv[1:2], axis=-1, keepdims=True))
           - jnp.exp(jnp.sum(lv[2:3] * lv[3:4], axis=-1, keepdims=True)) + lam_init)
    first = lax.broadcasted_iota(jnp.int32, (1, LANE), 1) < DF_HD
    zero = jnp.zeros((), BF16)

    vt_scr[...] = v_ref[...].astype(F32).T.astype(BF16)

    key = lax.broadcasted_iota(jnp.int32, (bt, w), 0)
    qry = lax.broadcasted_iota(jnp.int32, (bt, w), 1) & (tqp - 1)
    if causal:
        visible = (key // CHUNK) <= (qry // CHUNK)
    else:
        visible = key < (n_valid - (s_len - bt))
    bias = jnp.where(visible, 0.0, -jnp.inf)

    def fold8(x, op):
        return op(x.reshape(rc // 8, 8, w), axis=0)

    for i in range(n_q):
        n_vis = (i + 1) * tq if causal else s_len
        rows_q = slice(i * tq, (i + 1) * tq)
        q = q_ref[rows_q, :]
        halves = [jnp.where(first, q, zero), jnp.where(first, zero, q)]
        if tqp > tq:
            pad = jnp.zeros((tqp - tq, LANE), BF16)
            halves = [halves[0], pad, halves[1], pad]
        qs = jnp.concatenate(halves, axis=0)
        s_scr[0:n_vis, :] = _dot_nt(k_ref[0:n_vis, :], qs)
        s_scr[n_vis - bt:n_vis, :] = s_scr[n_vis - bt:n_vis, :] + bias
        n_c = n_vis // rc

        def chunk_rows(c):
            return pl.ds(pl.multiple_of(c * rc, rc), rc)

        def pass_max(c, m8):
            return jnp.maximum(m8, fold8(s_scr[chunk_rows(c), :], jnp.max))

        m8 = lax.fori_loop(0, n_c, pass_max, jnp.full((8, w), -jnp.inf, F32))
        m = jnp.max(m8, axis=0, keepdims=True)

        def pass_exp(c, l8):
            p = jnp.exp(s_scr[chunk_rows(c), :] - m)
            s_scr[chunk_rows(c), :] = p
            return l8 + fold8(p, jnp.sum)

        l8 = lax.fori_loop(0, n_c, pass_exp, jnp.zeros((8, w), F32))
        rinv = 1.0 / jnp.sum(l8, axis=0, keepdims=True)
        r1 = rinv[:, :tqp]
        r2 = rinv[:, tqp:] * lam

        def pass_mix(c, carry):
            p = s_scr[chunk_rows(c), :]
            a_scr[chunk_rows(c), :] = (p[:, :tqp] * r1 - p[:, tqp:] * r2).astype(BF16)
            return carry

        lax.fori_loop(0, n_c, pass_mix, 0)
        o = _dot(vt_scr[:, 0:n_vis], a_scr[0:n_vis, :]).T[:tq]
        y = _rms(o, g_ref[...]) * (1.0 - lam_init) * z_ref[rows_q, :].astype(F32)
        y_ref[rows_q, :] = y.astype(BF16)


def _diff_attn(lam_vecs, q_arr, q_cg, k_arr, k_cg, v_arr, v_cg, z_arr, z_cg, g_dsub,
               *, batch, seq, s_len, n_valid, causal, tq, rc, lam_init):
    tqp = max(tq, LANE)
    bt = tq if causal else rc

    def col(rows, cg):
        return pl.BlockSpec((rows, LANE), lambda b, h: (b, cg * DF_HEADS + h))

    kern = functools.partial(_diffattn_kernel, n_q=seq // tq, tq=tq, tqp=tqp, rc=rc, bt=bt, s_len=s_len,
                             n_valid=n_valid, causal=causal, lam_init=lam_init)
    return pl.pallas_call(
        kern,
        grid=(batch, DF_HEADS),
        in_specs=[pl.BlockSpec((4, DF_HD), lambda b, h: (0, 0)),
                  col(seq, q_cg), col(s_len, k_cg), col(s_len, v_cg), col(seq, z_cg),
                  pl.BlockSpec((1, DF_VD), lambda b, h: (0, 0))],
        out_specs=pl.BlockSpec((seq, LANE), lambda b, h: (b, h)),
        out_shape=jax.ShapeDtypeStruct((batch * seq, D_MODEL), BF16),
        scratch_shapes=[pltpu.VMEM((s_len, 2 * tqp), F32), pltpu.VMEM((s_len, tqp), BF16),
                        pltpu.VMEM((LANE, s_len), BF16)],
        compiler_params=_params("arbitrary", "arbitrary"),
        name="diff_attn",
    )(lam_vecs, q_arr, k_arr, v_arr, z_arr, g_dsub.reshape(1, -1))


def _merge_kernel(x_ref, ya_ref, yb_ref, qm_ref, zm_ref, g0_ref, g1_ref, g2_ref, mk_ref, mv_ref,
                  wb_ref, wo_ref, y_ref):
    ym = []
    for h in range(MEM_HEADS):
        sl = slice(h * MEM_HD, (h + 1) * MEM_HD)
        s = _dot_nt(qm_ref[:, sl], mk_ref[:, sl].astype(BF16))
        p = jnp.exp(s - jnp.max(s, axis=-1, keepdims=True))
        rinv = 1.0 / jnp.sum(p, axis=-1, keepdims=True)
        o = _dot(p.astype(BF16), mv_ref[:, sl].astype(BF16)) * rinv
        ym.append((o * zm_ref[:, sl].astype(F32)).astype(BF16))
    ym = jnp.concatenate(ym, axis=-1)
    hsum = (g0_ref[...].astype(F32) * _dot(ya_ref[...], wb_ref[0])
            + g1_ref[...].astype(F32) * _dot(yb_ref[...], wb_ref[1])
            + g2_ref[...].astype(F32) * _dot(ym, wb_ref[2]))
    y_ref[...] = x_ref[...] + _dot(hsum.astype(BF16), wo_ref[...])


def _merge(x, ya, yb, p, mem_k, mem_v, w_branch, w_out, *, seq, n_mem):
    n = x.shape[0]
    tm = min(256, seq)
    per_b = seq // tm
    row = lambda i: (i, 0)

    def col(cg):
        return pl.BlockSpec((tm, D_MODEL), lambda i: (i, cg))

    mem = pl.BlockSpec((n_mem, D_MODEL), lambda i: (i // per_b, 0))
    return pl.pallas_call(
        _merge_kernel,
        grid=(n // tm,),
        in_specs=[pl.BlockSpec((tm, D_MODEL), row), pl.BlockSpec((tm, D_MODEL), row), pl.BlockSpec((tm, D_MODEL), row),
                  col(CG_MQ), col(CG_ZM), col(CG_G0), col(CG_G0 + 1), col(CG_G0 + 2), mem, mem,
                  pl.BlockSpec((N_BRANCH, D_MODEL, D_MODEL), lambda i: (0, 0, 0)),
                  pl.BlockSpec((D_MODEL, D_MODEL), lambda i: (0, 0))],
        out_specs=pl.BlockSpec((tm, D_MODEL), row),
        out_shape=jax.ShapeDtypeStruct((n, D_MODEL), F32),
        compiler_params=_params("arbitrary"),
        name="merge",
    )(x, ya, yb, p, p, p, p, p, mem_k, mem_v, w_branch, w_out)


def _layer(x, s0, past_k, past_v, mem_k, mem_v, lam_init, w, n_mem):
    batch, seq, _ = x.shape
    n = batch * seq
    xf = x.reshape(n, D_MODEL)
    prompt = past_k is None
    p, logf, k32, v32 = _in_proj(xf, w["g_norm"], w["w_in"], w["lb_logits"], w["g_dq"], w["g_dk"], w["g_mq"],
                                 seq=seq, k_transposed=prompt)
    ya, s_new = _hgrn(p, logf, s0, w["g_hg_out"], batch, seq)
    if prompt:
        yb = _diff_attn(w["lam_vecs"], p, CG_DQ, p, CG_DK, p, CG_DV, p, CG_ZB, w["g_dsub"],
                        batch=batch, seq=seq, s_len=seq, n_valid=seq, causal=True, tq=256, rc=256, lam_init=lam_init)
    else:
        past = past_k.shape[1]
        tk = LANE
        n_valid = past + seq
        s_len = -(-n_valid // tk) * tk
        pad = jnp.zeros((batch, s_len - n_valid, D_MODEL), BF16)
        k_all = jnp.concatenate([past_k.reshape(batch, past, D_MODEL).astype(BF16),
                                 p[:, CG_DK * D_MODEL:(CG_DK + 1) * D_MODEL].reshape(batch, seq, D_MODEL), pad], axis=1)
        v_all = jnp.concatenate([past_v.reshape(batch, past, D_MODEL).astype(BF16),
                                 p[:, CG_DV * D_MODEL:(CG_DV + 1) * D_MODEL].reshape(batch, seq, D_MODEL), pad], axis=1)
        yb = _diff_attn(w["lam_vecs"], p, CG_DQ, k_all.reshape(batch * s_len, D_MODEL), 0,
                        v_all.reshape(batch * s_len, D_MODEL), 0, p, CG_ZB, w["g_dsub"],
                        batch=batch, seq=seq, s_len=s_len, n_valid=n_valid, causal=False, tq=seq, rc=tk,
                        lam_init=lam_init)
    y = _merge(xf, ya, yb, p, mem_k, mem_v, w["w_branch"], w["w_out"], seq=seq, n_mem=n_mem)
    if prompt:
        k_out = jnp.transpose(k32.reshape(batch, DF_HEADS, 2, DF_HD, seq), (0, 4, 1, 2, 3))
    else:
        k_out = k32.reshape(batch, seq, DF_HEADS, 2, DF_HD)
    return y.reshape(batch, seq, D_MODEL), s_new, k_out, v32.reshape(batch, seq, DF_HEADS, DF_VD)


def kernel(x_prompt, x_sample, mem_prompt, cache_diff_k, cache_diff_v, cache_mem_k, cache_mem_v, state_hgrn,
           g_norm, w_in, hg_lb_logits, g_hg_out, g_dq, g_dk, lam_q1, lam_k1, lam_q2, lam_k2, g_dsub, g_mem,
           w_mkv, g_mq, g_mk, w_branch, w_out):
    depth = g_norm.shape[0]
    assert depth == 1 and hg_lb_logits.shape[0] == 2, "single-layer step only"
    batch, n_mem = mem_prompt.shape[:2]
    dec_batch = x_sample.shape[0]
    lam_init = 0.8 - 0.6
    w = dict(g_norm=g_norm[0], w_in=w_in[0].astype(BF16), lb_logits=hg_lb_logits, g_hg_out=g_hg_out[0],
             g_dq=g_dq[0], g_dk=g_dk[0], g_mq=g_mq[0], g_dsub=g_dsub[0],
             lam_vecs=jnp.stack([lam_q1[0], lam_k1[0], lam_q2[0], lam_k2[0]]),
             w_branch=w_branch[0].astype(BF16), w_out=w_out[0].astype(BF16))

    mem_k, mem_v = _memory_kv(mem_prompt.reshape(batch * n_mem, D_MODEL), g_mem[0], w_mkv[0], g_mk[0])
    s0 = jnp.zeros((batch, HG_HEADS, HG_D, HG_D), F32)
    y_p, s_p, k_p, v_p = _layer(x_prompt, s0, None, None, mem_k, mem_v, lam_init, w, n_mem)
    y_s, s_s, k_s, v_s = _layer(x_sample, state_hgrn[0], cache_diff_k[0], cache_diff_v[0],
                                cache_mem_k[0].reshape(dec_batch * n_mem, D_MODEL),
                                cache_mem_v[0].reshape(dec_batch * n_mem, D_MODEL), lam_init, w, n_mem)
    mem_shape = (1, batch, n_mem, MEM_HEADS, MEM_HD)
    return (y_p, y_s, s_p[None], s_s[None], k_p[None], v_p[None], k_s[None], v_s[None],
            mem_k.reshape(mem_shape), mem_v.reshape(mem_shape))
```

```python
import functools

import jax
import jax.numpy as jnp
from jax import lax
from jax.experimental import pallas as pl
from jax.experimental.pallas import tpu as pltpu

F32 = jnp.float32
BF16 = jnp.bfloat16

EPS = 1e-6
LOG2E = 1.4426950408889634
D_MODEL = 1024
CHUNK = 64
GLA_BLOCK = 16
HG_HEADS, HG_D = 8, 128
DF_HEADS, DF_HD, DF_VD = 8, 64, 128
MEM_HEADS, MEM_HD = 4, 256
N_BRANCH = 3
N_COLGROUPS = 13
(CG_HQ, CG_HF, CG_HI, CG_ZA, CG_DQ, CG_DK, CG_DV, CG_ZB, CG_MQ, CG_ZM, CG_G0) = range(11)

LANE = 128
VMEM_LIMIT = 52 * 1024 * 1024
VMEM_LIMIT_INPROJ = 60 * 1024 * 1024

NT = (((1,), (1,)), ((), ()))
TN = (((0,), (0,)), ((), ()))


def _dot(a, b):
    return jnp.dot(a, b, preferred_element_type=F32)


def _dot_nt(a, b):
    return lax.dot_general(a, b, NT, preferred_element_type=F32)


def _dot_tn(a, b):
    return lax.dot_general(a, b, TN, preferred_element_type=F32)


def _sigmoid(x):
    return 0.5 * jnp.tanh(0.5 * x) + 0.5


def _rms(x, g):
    return x * lax.rsqrt(jnp.mean(x * x, axis=-1, keepdims=True) + EPS) * g


def _params(*sem, vmem=VMEM_LIMIT):
    return pltpu.CompilerParams(dimension_semantics=sem, vmem_limit_bytes=vmem)


def _memkv_kernel(mem_ref, g_ref, w_ref, gk_ref, k_ref, v_ref):
    xn = _rms(mem_ref[...], g_ref[...]).astype(BF16)
    kv = _dot(xn, w_ref[...])
    for h in range(MEM_HEADS):
        sl = slice(h * MEM_HD, (h + 1) * MEM_HD)
        k_ref[:, sl] = _rms(kv[:, sl], gk_ref[...])
    v_ref[...] = kv[:, D_MODEL:]


def _memory_kv(mem, g_mem, w_mkv, g_mk):
    n = mem.shape[0]
    tm = 512
    return pl.pallas_call(
        _memkv_kernel,
        grid=(n // tm,),
        in_specs=[
            pl.BlockSpec((tm, D_MODEL), lambda i: (i, 0)),
            pl.BlockSpec((1, D_MODEL), lambda i: (0, 0)),
            pl.BlockSpec((D_MODEL, 2 * D_MODEL), lambda i: (0, 0)),
            pl.BlockSpec((1, MEM_HD), lambda i: (0, 0)),
        ],
        out_specs=[pl.BlockSpec((tm, D_MODEL), lambda i: (i, 0))] * 2,
        out_shape=[jax.ShapeDtypeStruct((n, D_MODEL), F32)] * 2,
        compiler_params=_params("arbitrary"),
        name="memory_kv",
    )(mem, g_mem.reshape(1, -1), w_mkv.astype(BF16), g_mk.reshape(1, -1))


def _half_head_rms(acc, g_ref, scale):
    lane = lax.broadcasted_iota(jnp.int32, (1, LANE), 1)
    lo = lane < DF_HD
    outs = []
    for h in range(DF_HEADS):
        sl = slice(h * LANE, (h + 1) * LANE)
        xh = acc[:, sl]
        sq = xh * xh
        s_lo = jnp.sum(jnp.where(lo, sq, 0.0), axis=-1, keepdims=True)
        s_hi = jnp.sum(jnp.where(lo, 0.0, sq), axis=-1, keepdims=True)
        r = jnp.where(lo, lax.rsqrt(s_lo * (1.0 / DF_HD) + EPS), lax.rsqrt(s_hi * (1.0 / DF_HD) + EPS))
        outs.append(xh * r * (g_ref[:, sl] * scale))
    return outs


def _inproj_kernel(x_ref, gn_ref, w_ref, lbl_ref, gq_ref, gk_ref, gmq_ref,
                   p_ref, logf_ref, k_ref, v_ref, xn_scr, *, k_transposed):
    j = pl.program_id(1)

    @pl.when(j == 0)
    def _():
        xn_scr[...] = _rms(x_ref[...], gn_ref[...]).astype(BF16)

    def proj():
        return _dot(xn_scr[...], w_ref[j])

    @pl.when((j == CG_HQ) | (j == CG_HI))
    def _():
        p_ref[...] = proj().astype(BF16)

    @pl.when(j == CG_DV)
    def _():
        acc = proj()
        p_ref[...] = acc.astype(BF16)
        v_ref[...] = acc

    @pl.when(j == CG_HF)
    def _():
        l = lbl_ref[...]
        e = jnp.exp(l - jnp.max(l, axis=0, keepdims=True))
        lb = e[0:1] / jnp.sum(e, axis=0, keepdims=True)
        sp = _sigmoid(proj())
        logf_ref[...] = jnp.log(lb + (1.0 - lb) * sp)
        p_ref[...] = ((1.0 - lb) * (1.0 - sp)).astype(BF16)

    @pl.when((j == CG_ZA) | (j == CG_ZB) | (j == CG_ZM))
    def _():
        acc = proj()
        p_ref[...] = (acc * _sigmoid(acc)).astype(BF16)

    @pl.when(j >= CG_G0)
    def _():
        p_ref[...] = _sigmoid(proj()).astype(BF16)

    @pl.when(j == CG_DQ)
    def _():
        outs = _half_head_rms(proj(), gq_ref, DF_HD ** -0.5 * LOG2E)
        for h in range(DF_HEADS):
            p_ref[:, h * LANE:(h + 1) * LANE] = outs[h].astype(BF16)

    @pl.when(j == CG_DK)
    def _():
        outs = _half_head_rms(proj(), gk_ref, 1.0)
        for h in range(DF_HEADS):
            sl = slice(h * LANE, (h + 1) * LANE)
            p_ref[:, sl] = outs[h].astype(BF16)
            if k_transposed:
                k_ref[0, sl, :] = outs[h].T
            else:
                k_ref[:, sl] = outs[h]

    @pl.when(j == CG_MQ)
    def _():
        acc = proj()
        for h in range(MEM_HEADS):
            sl = slice(h * MEM_HD, (h + 1) * MEM_HD)
            p_ref[:, sl] = (_rms(acc[:, sl], gmq_ref[...]) * (MEM_HD ** -0.5)).astype(BF16)


def _in_proj(x, g_norm, w_in, lb_logits, g_dq, g_dk, g_mq, *, seq, k_transposed):
    n = x.shape[0]
    tm = min(512, seq) if k_transposed else min(512, n)
    row = lambda i, j: (i, 0)
    const = lambda i, j: (0, 0)
    if k_transposed:
        per_b = seq // tm
        k_spec = pl.BlockSpec((1, D_MODEL, tm), lambda i, j: (i // per_b, 0, i % per_b))
        k_shape = jax.ShapeDtypeStruct((n // seq, D_MODEL, seq), F32)
    else:
        k_spec = pl.BlockSpec((tm, D_MODEL), row)
        k_shape = jax.ShapeDtypeStruct((n, D_MODEL), F32)
    return pl.pallas_call(
        functools.partial(_inproj_kernel, k_transposed=k_transposed),
        grid=(n // tm, N_COLGROUPS),
        in_specs=[
            pl.BlockSpec((tm, D_MODEL), row),
            pl.BlockSpec((1, D_MODEL), const),
            pl.BlockSpec((N_COLGROUPS, D_MODEL, D_MODEL), lambda i, j: (0, 0, 0), pipeline_mode=pl.Buffered(1)),
            pl.BlockSpec((2, D_MODEL), const),
            pl.BlockSpec((1, D_MODEL), const),
            pl.BlockSpec((1, D_MODEL), const),
            pl.BlockSpec((1, MEM_HD), const),
        ],
        out_specs=[
            pl.BlockSpec((tm, D_MODEL), lambda i, j: (i, j)),
            pl.BlockSpec((tm, D_MODEL), row),
            k_spec,
            pl.BlockSpec((tm, D_MODEL), row),
        ],
        out_shape=[
            jax.ShapeDtypeStruct((n, N_COLGROUPS * D_MODEL), BF16),
            jax.ShapeDtypeStruct((n, D_MODEL), F32),
            k_shape,
            jax.ShapeDtypeStruct((n, D_MODEL), F32),
        ],
        scratch_shapes=[pltpu.VMEM((tm, D_MODEL), BF16)],
        compiler_params=_params("arbitrary", "arbitrary", vmem=VMEM_LIMIT_INPROJ),
        name="in_proj",
    )(x, g_norm.reshape(1, -1), w_in, lb_logits,
      jnp.tile(g_dq, 2 * DF_HEADS).reshape(1, -1), jnp.tile(g_dk, 2 * DF_HEADS).reshape(1, -1), g_mq.reshape(1, -1))


def _split_bf16(x):
    hi = x.astype(BF16)
    return hi, (x - hi.astype(F32)).astype(BF16)


def _hgrn_kernel(q_ref, k_ref, v_ref, z_ref, lf_ref, s0_ref, g_ref, y_ref, s_ref, *, seq, tc, hp):
    nb = tc // GLA_BLOCK
    row = lax.broadcasted_iota(jnp.int32, (tc, tc), 0)
    col = lax.broadcasted_iota(jnp.int32, (tc, tc), 1)
    causal = col <= row
    tril = jnp.where(causal, 1.0, 0.0).astype(BF16)
    ones = jnp.ones((tc, LANE), BF16)
    rowk = lax.broadcasted_iota(jnp.int32, (tc, LANE), 0)

    @pl.when(pl.program_id(1) == 0)
    def _():
        s_ref[...] = s0_ref[...]

    def chunk(c, carry):
        rows = pl.ds(pl.multiple_of(c * tc, tc), tc)
        for h in range(hp):
            sl = slice(h * LANE, (h + 1) * LANE)
            s = s_ref[0, h]
            g_hi, g_lo = _split_bf16(lf_ref[rows, sl])
            b = _dot(tril, g_hi) + _dot(tril, g_lo)
            b_end_col = _dot_tn(g_hi, ones) + _dot_tn(g_lo, ones)
            b_end = b[tc - 1:tc, :]
            q = q_ref[rows, sl].astype(F32)
            k = k_ref[rows, sl].astype(F32)
            v = v_ref[rows, sl]
            o = _dot((q * jnp.exp(b)).astype(BF16), s.astype(BF16))
            k_dec = (k * jnp.exp(b_end - b)).astype(BF16)
            s_ref[0, h] = jnp.exp(b_end_col) * s + _dot_tn(k_dec, v)
            att = []
            for n in range(nb):
                lo, hi = n * GLA_BLOCK, (n + 1) * GLA_BLOCK
                g_n = b[lo - 1:lo, :] if n else jnp.zeros((1, LANE), F32)
                q_dec = (q[lo:hi] * jnp.exp(b[lo:hi] - g_n)).astype(BF16)
                k_inv = (k * jnp.exp(jnp.where(rowk < hi, g_n - b, 0.0))).astype(BF16)
                att.append(_dot_nt(q_dec, k_inv))
            a = jnp.where(causal, jnp.concatenate(att, axis=0), 0.0).astype(BF16)
            o = o + _dot(a, v)
            y = _rms(o, g_ref[...]) * z_ref[rows, sl].astype(F32)
            y_ref[rows, sl] = y.astype(BF16)
        return carry

    lax.fori_loop(0, seq // tc, chunk, 0)


def _hgrn(p, logf, s0, g_hg_out, batch, seq):
    hp = HG_HEADS
    ts = min(512, seq)
    tc = min(128, ts)
    per_b = seq // ts

    def col(cg):
        return pl.BlockSpec((ts, D_MODEL), lambda b, t: (b * per_b + t, cg))

    st = pl.BlockSpec((1, hp, HG_D, HG_D), lambda b, t: (b, 0, 0, 0))
    return pl.pallas_call(
        functools.partial(_hgrn_kernel, seq=ts, tc=tc, hp=hp),
        grid=(batch, per_b),
        in_specs=[col(CG_HQ), col(CG_HF), col(CG_HI), col(CG_ZA), col(0), st,
                  pl.BlockSpec((1, HG_D), lambda b, t: (0, 0))],
        out_specs=[col(0), st],
        out_shape=[jax.ShapeDtypeStruct((batch * seq, D_MODEL), BF16),
                   jax.ShapeDtypeStruct((batch, HG_HEADS, HG_D, HG_D), F32)],
        compiler_params=_params("arbitrary", "arbitrary"),
        name="hgrn",
    )(p, p, p, p, logf, s0, g_hg_out.reshape(1, -1))


def _diffattn_kernel(lam_ref, q_ref, k_ref, v_ref, z_ref, g_ref, y_ref, s_scr, a_scr, vt_scr,
                     *, n_q, tq, tqp, rc, bt, s_len, n_valid, causal, lam_init):
    w = 2 * tqp
    lv = lam_ref[...]
    lam = (jnp.exp(jnp.sum(lv[0:1] * lv[1:2], axis=-1, keepdims=True))
           - jnp.exp(jnp.sum(lv[2:3] * lv[3:4], axis=-1, keepdims=True)) + lam_init)
    first = lax.broadcasted_iota(jnp.int32, (1, LANE), 1) < DF_HD
    zero = jnp.zeros((), BF16)

    vt_scr[...] = v_ref[...].astype(F32).T.astype(BF16)

    key = lax.broadcasted_iota(jnp.int32, (bt, w), 0)
    qry = lax.broadcasted_iota(jnp.int32, (bt, w), 1) & (tqp - 1)
    if causal:
        visible = (key // CHUNK) <= (qry // CHUNK)
    else:
        visible = key < (n_valid - (s_len - bt))
    bias = jnp.where(visible, 0.0, -jnp.inf)

    def fold8(x, op):
        return op(x.reshape(rc // 8, 8, w), axis=0)

    for i in range(n_q):
        n_vis = (i + 1) * tq if causal else s_len
        s_buf, a_buf = s_scr.at[i % 2], a_scr.at[i % 2]
        rows_q = slice(i * tq, (i + 1) * tq)
        q = q_ref[rows_q, :]
        halves = [jnp.where(first, q, zero), jnp.where(first, zero, q)]
        if tqp > tq:
            pad = jnp.zeros((tqp - tq, LANE), BF16)
            halves = [halves[0], pad, halves[1], pad]
        qs = jnp.concatenate(halves, axis=0)
        s_buf[0:n_vis, :] = _dot_nt(k_ref[0:n_vis, :], qs)
        s_buf[n_vis - bt:n_vis, :] = s_buf[n_vis - bt:n_vis, :] + bias
        chunks = [slice(c * rc, (c + 1) * rc) for c in range(n_vis // rc)]

        m8 = fold8(s_buf[chunks[0], :], jnp.max)
        for rows in chunks[1:]:
            m8 = jnp.maximum(m8, fold8(s_buf[rows, :], jnp.max))
        m = jnp.max(m8, axis=0, keepdims=True)

        l8 = jnp.zeros((8, w), F32)
        for rows in chunks:
            p = jnp.exp2(s_buf[rows, :] - m)
            s_buf[rows, :] = p
            l8 = l8 + fold8(p, jnp.sum)
        rinv = 1.0 / jnp.sum(l8, axis=0, keepdims=True)
        r1 = rinv[:, :tqp]
        r2 = rinv[:, tqp:] * lam

        for rows in chunks:
            p = s_buf[rows, :]
            a_buf[rows, :] = (p[:, :tqp] * r1 - p[:, tqp:] * r2).astype(BF16)
        o = _dot(vt_scr[:, 0:n_vis], a_buf[0:n_vis, :]).T[:tq]
        y = _rms(o, g_ref[...]) * (1.0 - lam_init) * z_ref[rows_q, :].astype(F32)
        y_ref[rows_q, :] = y.astype(BF16)


def _diff_attn(lam_vecs, q_arr, q_cg, k_arr, k_cg, v_arr, v_cg, z_arr, z_cg, g_dsub,
               *, batch, seq, s_len, n_valid, causal, tq, rc, lam_init):
    tqp = max(tq, LANE)
    bt = tq if causal else rc

    def col(rows, cg):
        return pl.BlockSpec((rows, LANE), lambda b, h: (b, cg * DF_HEADS + h))

    kern = functools.partial(_diffattn_kernel, n_q=seq // tq, tq=tq, tqp=tqp, rc=rc, bt=bt, s_len=s_len,
                             n_valid=n_valid, causal=causal, lam_init=lam_init)
    return pl.pallas_call(
        kern,
        grid=(batch, DF_HEADS),
        in_specs=[pl.BlockSpec((4, DF_HD), lambda b, h: (0, 0)),
                  col(seq, q_cg), col(s_len, k_cg), col(s_len, v_cg), col(seq, z_cg),
                  pl.BlockSpec((1, DF_VD), lambda b, h: (0, 0))],
        out_specs=pl.BlockSpec((seq, LANE), lambda b, h: (b, h)),
        out_shape=jax.ShapeDtypeStruct((batch * seq, D_MODEL), BF16),
        scratch_shapes=[pltpu.VMEM((2, s_len, 2 * tqp), F32), pltpu.VMEM((2, s_len, tqp), BF16),
                        pltpu.VMEM((LANE, s_len), BF16)],
        compiler_params=_params("arbitrary", "arbitrary"),
        name="diff_attn",
    )(lam_vecs, q_arr, k_arr, v_arr, z_arr, g_dsub.reshape(1, -1))


def _merge_kernel(x_ref, ya_ref, yb_ref, qm_ref, zm_ref, g0_ref, g1_ref, g2_ref, mk_ref, mv_ref,
                  wb_ref, wo_ref, y_ref):
    ym = []
    for h in range(MEM_HEADS):
        sl = slice(h * MEM_HD, (h + 1) * MEM_HD)
        s = _dot_nt(qm_ref[:, sl], mk_ref[:, sl].astype(BF16))
        p = jnp.exp(s - jnp.max(s, axis=-1, keepdims=True))
        rinv = 1.0 / jnp.sum(p, axis=-1, keepdims=True)
        o = _dot(p.astype(BF16), mv_ref[:, sl].astype(BF16)) * rinv
        ym.append((o * zm_ref[:, sl].astype(F32)).astype(BF16))
    ym = jnp.concatenate(ym, axis=-1)
    hsum = (g0_ref[...].astype(F32) * _dot(ya_ref[...], wb_ref[0])
            + g1_ref[...].astype(F32) * _dot(yb_ref[...], wb_ref[1])
            + g2_ref[...].astype(F32) * _dot(ym, wb_ref[2]))
    y_ref[...] = x_ref[...] + _dot(hsum.astype(BF16), wo_ref[...])


def _merge(x, ya, yb, p, mem_k, mem_v, w_branch, w_out, *, seq, n_mem):
    n = x.shape[0]
    tm = min(256, seq)
    per_b = seq // tm
    row = lambda i: (i, 0)

    def col(cg):
        return pl.BlockSpec((tm, D_MODEL), lambda i: (i, cg))

    mem = pl.BlockSpec((n_mem, D_MODEL), lambda i: (i // per_b, 0))
    return pl.pallas_call(
        _merge_kernel,
        grid=(n // tm,),
        in_specs=[pl.BlockSpec((tm, D_MODEL), row), pl.BlockSpec((tm, D_MODEL), row), pl.BlockSpec((tm, D_MODEL), row),
                  col(CG_MQ), col(CG_ZM), col(CG_G0), col(CG_G0 + 1), col(CG_G0 + 2), mem, mem,
                  pl.BlockSpec((N_BRANCH, D_MODEL, D_MODEL), lambda i: (0, 0, 0)),
                  pl.BlockSpec((D_MODEL, D_MODEL), lambda i: (0, 0))],
        out_specs=pl.BlockSpec((tm, D_MODEL), row),
        out_shape=jax.ShapeDtypeStruct((n, D_MODEL), F32),
        compiler_params=_params("arbitrary"),
        name="merge",
    )(x, ya, yb, p, p, p, p, p, mem_k, mem_v, w_branch, w_out)


def _layer(x, s0, past_k, past_v, mem_k, mem_v, lam_init, w, n_mem):
    batch, seq, _ = x.shape
    n = batch * seq
    xf = x.reshape(n, D_MODEL)
    prompt = past_k is None
    p, logf, k32, v32 = _in_proj(xf, w["g_norm"], w["w_in"], w["lb_logits"], w["g_dq"], w["g_dk"], w["g_mq"],
                                 seq=seq, k_transposed=prompt)
    ya, s_new = _hgrn(p, logf, s0, w["g_hg_out"], batch, seq)
    if prompt:
        yb = _diff_attn(w["lam_vecs"], p, CG_DQ, p, CG_DK, p, CG_DV, p, CG_ZB, w["g_dsub"],
                        batch=batch, seq=seq, s_len=seq, n_valid=seq, causal=True, tq=256, rc=256, lam_init=lam_init)
    else:
        past = past_k.shape[1]
        tk = LANE
        n_valid = past + seq
        s_len = -(-n_valid // tk) * tk
        pad = jnp.zeros((batch, s_len - n_valid, D_MODEL), BF16)
        k_all = jnp.concatenate([past_k.reshape(batch, past, D_MODEL).astype(BF16),
                                 p[:, CG_DK * D_MODEL:(CG_DK + 1) * D_MODEL].reshape(batch, seq, D_MODEL), pad], axis=1)
        v_all = jnp.concatenate([past_v.reshape(batch, past, D_MODEL).astype(BF16),
                                 p[:, CG_DV * D_MODEL:(CG_DV + 1) * D_MODEL].reshape(batch, seq, D_MODEL), pad], axis=1)
        yb = _diff_attn(w["lam_vecs"], p, CG_DQ, k_all.reshape(batch * s_len, D_MODEL), 0,
                        v_all.reshape(batch * s_len, D_MODEL), 0, p, CG_ZB, w["g_dsub"],
                        batch=batch, seq=seq, s_len=s_len, n_valid=n_valid, causal=False, tq=seq, rc=tk,
                        lam_init=lam_init)
    y = _merge(xf, ya, yb, p, mem_k, mem_v, w["w_branch"], w["w_out"], seq=seq, n_mem=n_mem)
    if prompt:
        k_out = jnp.transpose(k32.reshape(batch, DF_HEADS, 2, DF_HD, seq), (0, 4, 1, 2, 3))
    else:
        k_out = k32.reshape(batch, seq, DF_HEADS, 2, DF_HD)
    return y.reshape(batch, seq, D_MODEL), s_new, k_out, v32.reshape(batch, seq, DF_HEADS, DF_VD)


def kernel(x_prompt, x_sample, mem_prompt, cache_diff_k, cache_diff_v, cache_mem_k, cache_mem_v, state_hgrn,
           g_norm, w_in, hg_lb_logits, g_hg_out, g_dq, g_dk, lam_q1, lam_k1, lam_q2, lam_k2, g_dsub, g_mem,
           w_mkv, g_mq, g_mk, w_branch, w_out):
    depth = g_norm.shape[0]
    assert depth == 1 and hg_lb_logits.shape[0] == 2, "single-layer step only"
    batch, n_mem = mem_prompt.shape[:2]
    dec_batch = x_sample.shape[0]
    lam_init = 0.8 - 0.6
    w_in_groups = w_in[0].astype(BF16).reshape(D_MODEL, N_COLGROUPS, D_MODEL).transpose(1, 0, 2)
    w = dict(g_norm=g_norm[0], w_in=w_in_groups, lb_logits=hg_lb_logits, g_hg_out=g_hg_out[0],
             g_dq=g_dq[0], g_dk=g_dk[0], g_mq=g_mq[0], g_dsub=g_dsub[0],
             lam_vecs=jnp.stack([lam_q1[0], lam_k1[0], lam_q2[0], lam_k2[0]]),
             w_branch=w_branch[0].astype(BF16), w_out=w_out[0].astype(BF16))

    mem_k, mem_v = _memory_kv(mem_prompt.reshape(batch * n_mem, D_MODEL), g_mem[0], w_mkv[0], g_mk[0])
    s0 = jnp.zeros((batch, HG_HEADS, HG_D, HG_D), F32)
    y_p, s_p, k_p, v_p = _layer(x_prompt, s0, None, None, mem_k, mem_v, lam_init, w, n_mem)
    y_s, s_s, k_s, v_s = _layer(x_sample, state_hgrn[0], cache_diff_k[0], cache_diff_v[0],
                                cache_mem_k[0].reshape(dec_batch * n_mem, D_MODEL),
                                cache_mem_v[0].reshape(dec_batch * n_mem, D_MODEL), lam_init, w, n_mem)
    mem_shape = (1, batch, n_mem, MEM_HEADS, MEM_HD)
    return (y_p, y_s, s_p[None], s_s[None], k_p[None], v_p[None], k_s[None], v_s[None],
            mem_k.reshape(mem_shape), mem_v.reshape(mem_shape))
```

```python
import functools

import jax
import jax.numpy as jnp
from jax import lax
from jax.experimental import pallas as pl
from jax.experimental.pallas import tpu as pltpu

F32 = jnp.float32
BF16 = jnp.bfloat16

EPS = 1e-6
LOG2E = 1.4426950408889634
D_MODEL = 1024
CHUNK = 64
GLA_BLOCK = 16
HG_HEADS, HG_D = 8, 128
DF_HEADS, DF_HD, DF_VD = 8, 64, 128
MEM_HEADS, MEM_HD = 4, 256
N_BRANCH = 3
N_COLGROUPS = 13
(CG_HQ, CG_HF, CG_HI, CG_ZA, CG_DQ, CG_DK, CG_DV, CG_ZB, CG_MQ, CG_ZM, CG_G0) = range(11)

LANE = 128
VMEM_LIMIT = 52 * 1024 * 1024
VMEM_LIMIT_INPROJ = 60 * 1024 * 1024

NT = (((1,), (1,)), ((), ()))
TN = (((0,), (0,)), ((), ()))


def _dot(a, b):
    return jnp.dot(a, b, preferred_element_type=F32)


def _dot_nt(a, b):
    return lax.dot_general(a, b, NT, preferred_element_type=F32)


def _dot_tn(a, b):
    return lax.dot_general(a, b, TN, preferred_element_type=F32)


def _sigmoid(x):
    return 0.5 * jnp.tanh(0.5 * x) + 0.5


def _rms(x, g):
    return x * lax.rsqrt(jnp.mean(x * x, axis=-1, keepdims=True) + EPS) * g


def _params(*sem, vmem=VMEM_LIMIT):
    return pltpu.CompilerParams(dimension_semantics=sem, vmem_limit_bytes=vmem)


def _memkv_kernel(mem_ref, g_ref, w_ref, gk_ref, k_ref, v_ref):
    xn = _rms(mem_ref[...], g_ref[...]).astype(BF16)
    kv = _dot(xn, w_ref[...])
    for h in range(MEM_HEADS):
        sl = slice(h * MEM_HD, (h + 1) * MEM_HD)
        k_ref[:, sl] = _rms(kv[:, sl], gk_ref[...])
    v_ref[...] = kv[:, D_MODEL:]


def _memory_kv(mem, g_mem, w_mkv, g_mk):
    n = mem.shape[0]
    tm = 512
    return pl.pallas_call(
        _memkv_kernel,
        grid=(n // tm,),
        in_specs=[
            pl.BlockSpec((tm, D_MODEL), lambda i: (i, 0)),
            pl.BlockSpec((1, D_MODEL), lambda i: (0, 0)),
            pl.BlockSpec((D_MODEL, 2 * D_MODEL), lambda i: (0, 0)),
            pl.BlockSpec((1, MEM_HD), lambda i: (0, 0)),
        ],
        out_specs=[pl.BlockSpec((tm, D_MODEL), lambda i: (i, 0))] * 2,
        out_shape=[jax.ShapeDtypeStruct((n, D_MODEL), F32)] * 2,
        compiler_params=_params("arbitrary"),
        name="memory_kv",
    )(mem, g_mem.reshape(1, -1), w_mkv.astype(BF16), g_mk.reshape(1, -1))


def _half_head_rms(acc, g_ref, scale):
    lane = lax.broadcasted_iota(jnp.int32, (1, LANE), 1)
    lo = lane < DF_HD
    outs = []
    for h in range(DF_HEADS):
        sl = slice(h * LANE, (h + 1) * LANE)
        xh = acc[:, sl]
        sq = xh * xh
        s_lo = jnp.sum(jnp.where(lo, sq, 0.0), axis=-1, keepdims=True)
        s_hi = jnp.sum(jnp.where(lo, 0.0, sq), axis=-1, keepdims=True)
        r = jnp.where(lo, lax.rsqrt(s_lo * (1.0 / DF_HD) + EPS), lax.rsqrt(s_hi * (1.0 / DF_HD) + EPS))
        outs.append(xh * r * (g_ref[:, sl] * scale))
    return outs


def _inproj_kernel(x_ref, gn_ref, w_ref, lbl_ref, gq_ref, gk_ref, gmq_ref,
                   p_ref, logf_ref, k_ref, v_ref, xn_scr, *, k_transposed):
    xn_scr[...] = _rms(x_ref[...], gn_ref[...]).astype(BF16)

    def proj(j):
        return _dot(xn_scr[...], w_ref[j])

    def put(j, val):
        p_ref[:, j * D_MODEL:(j + 1) * D_MODEL] = val.astype(BF16)

    put(CG_HQ, proj(CG_HQ))

    l = lbl_ref[...]
    e = jnp.exp(l - jnp.max(l, axis=0, keepdims=True))
    lb = e[0:1] / jnp.sum(e, axis=0, keepdims=True)
    sp = _sigmoid(proj(CG_HF))
    logf_ref[...] = jnp.log(lb + (1.0 - lb) * sp)
    put(CG_HF, (1.0 - lb) * (1.0 - sp))

    put(CG_HI, proj(CG_HI))

    for j in (CG_ZA, CG_ZB, CG_ZM):
        acc = proj(j)
        put(j, acc * _sigmoid(acc))

    outs = _half_head_rms(proj(CG_DQ), gq_ref, DF_HD ** -0.5 * LOG2E)
    for h in range(DF_HEADS):
        p_ref[:, CG_DQ * D_MODEL + h * LANE:CG_DQ * D_MODEL + (h + 1) * LANE] = outs[h].astype(BF16)

    outs = _half_head_rms(proj(CG_DK), gk_ref, 1.0)
    for h in range(DF_HEADS):
        sl = slice(h * LANE, (h + 1) * LANE)
        p_ref[:, CG_DK * D_MODEL + h * LANE:CG_DK * D_MODEL + (h + 1) * LANE] = outs[h].astype(BF16)
        if k_transposed:
            k_ref[0, sl, :] = outs[h].T
        else:
            k_ref[:, sl] = outs[h]

    acc = proj(CG_DV)
    put(CG_DV, acc)
    v_ref[...] = acc

    acc = proj(CG_MQ)
    for h in range(MEM_HEADS):
        sl = slice(h * MEM_HD, (h + 1) * MEM_HD)
        p_ref[:, CG_MQ * D_MODEL + h * MEM_HD:CG_MQ * D_MODEL + (h + 1) * MEM_HD] = (
            _rms(acc[:, sl], gmq_ref[...]) * (MEM_HD ** -0.5)).astype(BF16)

    for j in range(CG_G0, N_COLGROUPS):
        put(j, _sigmoid(proj(j)))


def _in_proj(x, g_norm, w_in, lb_logits, g_dq, g_dk, g_mq, *, seq, k_transposed):
    n = x.shape[0]
    tm = min(256, seq) if k_transposed else min(256, n)
    row = lambda i: (i, 0)
    const = lambda i: (0, 0)
    if k_transposed:
        per_b = seq // tm
        k_spec = pl.BlockSpec((1, D_MODEL, tm), lambda i: (i // per_b, 0, i % per_b))
        k_shape = jax.ShapeDtypeStruct((n // seq, D_MODEL, seq), F32)
    else:
        k_spec = pl.BlockSpec((tm, D_MODEL), row)
        k_shape = jax.ShapeDtypeStruct((n, D_MODEL), F32)
    return pl.pallas_call(
        functools.partial(_inproj_kernel, k_transposed=k_transposed),
        grid=(n // tm,),
        in_specs=[
            pl.BlockSpec((tm, D_MODEL), row),
            pl.BlockSpec((1, D_MODEL), const),
            pl.BlockSpec((N_COLGROUPS, D_MODEL, D_MODEL), lambda i: (0, 0, 0), pipeline_mode=pl.Buffered(1)),
            pl.BlockSpec((2, D_MODEL), const),
            pl.BlockSpec((1, D_MODEL), const),
            pl.BlockSpec((1, D_MODEL), const),
            pl.BlockSpec((1, MEM_HD), const),
        ],
        out_specs=[
            pl.BlockSpec((tm, N_COLGROUPS * D_MODEL), row),
            pl.BlockSpec((tm, D_MODEL), row),
            k_spec,
            pl.BlockSpec((tm, D_MODEL), row),
        ],
        out_shape=[
            jax.ShapeDtypeStruct((n, N_COLGROUPS * D_MODEL), BF16),
            jax.ShapeDtypeStruct((n, D_MODEL), F32),
            k_shape,
            jax.ShapeDtypeStruct((n, D_MODEL), F32),
        ],
        scratch_shapes=[pltpu.VMEM((tm, D_MODEL), BF16)],
        compiler_params=_params("arbitrary", vmem=VMEM_LIMIT_INPROJ),
        name="in_proj",
    )(x, g_norm.reshape(1, -1), w_in, lb_logits,
      jnp.tile(g_dq, 2 * DF_HEADS).reshape(1, -1), jnp.tile(g_dk, 2 * DF_HEADS).reshape(1, -1), g_mq.reshape(1, -1))


def _split_bf16(x):
    hi = x.astype(BF16)
    return hi, (x - hi.astype(F32)).astype(BF16)


def _hgrn_kernel(q_ref, k_ref, v_ref, z_ref, lf_ref, s0_ref, g_ref, y_ref, s_ref, *, seq, tc, hp):
    nb = tc // GLA_BLOCK
    row = lax.broadcasted_iota(jnp.int32, (tc, tc), 0)
    col = lax.broadcasted_iota(jnp.int32, (tc, tc), 1)
    causal = col <= row
    tril = jnp.where(causal, 1.0, 0.0).astype(BF16)
    ones = jnp.ones((tc, LANE), BF16)
    rowk = lax.broadcasted_iota(jnp.int32, (tc, LANE), 0)

    @pl.when(pl.program_id(1) == 0)
    def _():
        s_ref[...] = s0_ref[...]

    def chunk(c, carry):
        rows = pl.ds(pl.multiple_of(c * tc, tc), tc)
        for h in range(hp):
            sl = slice(h * LANE, (h + 1) * LANE)
            s = s_ref[0, h]
            g_hi, g_lo = _split_bf16(lf_ref[rows, sl])
            b = _dot(tril, g_hi) + _dot(tril, g_lo)
            b_end_col = _dot_tn(g_hi, ones) + _dot_tn(g_lo, ones)
            b_end = b[tc - 1:tc, :]
            q = q_ref[rows, sl].astype(F32)
            k = k_ref[rows, sl].astype(F32)
            v = v_ref[rows, sl]
            o = _dot((q * jnp.exp(b)).astype(BF16), s.astype(BF16))
            k_dec = (k * jnp.exp(b_end - b)).astype(BF16)
            s_ref[0, h] = jnp.exp(b_end_col) * s + _dot_tn(k_dec, v)
            att = []
            for n in range(nb):
                lo, hi = n * GLA_BLOCK, (n + 1) * GLA_BLOCK
                g_n = b[lo - 1:lo, :] if n else jnp.zeros((1, LANE), F32)
                q_dec = (q[lo:hi] * jnp.exp(b[lo:hi] - g_n)).astype(BF16)
                k_inv = (k * jnp.exp(jnp.where(rowk < hi, g_n - b, 0.0))).astype(BF16)
                att.append(_dot_nt(q_dec, k_inv))
            a = jnp.where(causal, jnp.concatenate(att, axis=0), 0.0).astype(BF16)
            o = o + _dot(a, v)
            y = _rms(o, g_ref[...]) * z_ref[rows, sl].astype(F32)
            y_ref[rows, sl] = y.astype(BF16)
        return carry

    lax.fori_loop(0, seq // tc, chunk, 0)


def _hgrn(p, logf, s0, g_hg_out, batch, seq):
    hp = HG_HEADS
    ts = min(512, seq)
    tc = min(128, ts)
    per_b = seq // ts

    def col(cg):
        return pl.BlockSpec((ts, D_MODEL), lambda b, t: (b * per_b + t, cg))

    st = pl.BlockSpec((1, hp, HG_D, HG_D), lambda b, t: (b, 0, 0, 0))
    return pl.pallas_call(
        functools.partial(_hgrn_kernel, seq=ts, tc=tc, hp=hp),
        grid=(batch, per_b),
        in_specs=[col(CG_HQ), col(CG_HF), col(CG_HI), col(CG_ZA), col(0), st,
                  pl.BlockSpec((1, HG_D), lambda b, t: (0, 0))],
        out_specs=[col(0), st],
        out_shape=[jax.ShapeDtypeStruct((batch * seq, D_MODEL), BF16),
                   jax.ShapeDtypeStruct((batch, HG_HEADS, HG_D, HG_D), F32)],
        compiler_params=_params("arbitrary", "arbitrary"),
        name="hgrn",
    )(p, p, p, p, logf, s0, g_hg_out.reshape(1, -1))


def _diffattn_kernel(lam_ref, q_ref, k_ref, v_ref, z_ref, g_ref, y_ref, s_scr, a_scr, vt_scr,
                     *, n_q, tq, tqp, rc, bt, s_len, n_valid, causal, lam_init):
    w = 2 * tqp
    lv = lam_ref[...]
    lam = (jnp.exp(jnp.sum(lv[0:1] * lv[1:2], axis=-1, keepdims=True))
           - jnp.exp(jnp.sum(lv[2:3] * lv[3:4], axis=-1, keepdims=True)) + lam_init)
    first = lax.broadcasted_iota(jnp.int32, (1, LANE), 1) < DF_HD
    zero = jnp.zeros((), BF16)

    vt_scr[...] = v_ref[...].astype(F32).T.astype(BF16)

    key = lax.broadcasted_iota(jnp.int32, (bt, w), 0)
    qry = lax.broadcasted_iota(jnp.int32, (bt, w), 1) & (tqp - 1)
    if causal:
        visible = (key // CHUNK) <= (qry // CHUNK)
    else:
        visible = key < (n_valid - (s_len - bt))
    bias = jnp.where(visible, 0.0, -jnp.inf)

    def fold8(x, op):
        return op(x.reshape(rc // 8, 8, w), axis=0)

    for i in range(n_q):
        n_vis = (i + 1) * tq if causal else s_len
        s_buf, a_buf = s_scr.at[i % 2], a_scr.at[i % 2]
        rows_q = slice(i * tq, (i + 1) * tq)
        q = q_ref[rows_q, :]
        halves = [jnp.where(first, q, zero), jnp.where(first, zero, q)]
        if tqp > tq:
            pad = jnp.zeros((tqp - tq, LANE), BF16)
            halves = [halves[0], pad, halves[1], pad]
        qs = jnp.concatenate(halves, axis=0)
        s_buf[0:n_vis, :] = _dot_nt(k_ref[0:n_vis, :], qs)
        s_buf[n_vis - bt:n_vis, :] = s_buf[n_vis - bt:n_vis, :] + bias
        chunks = [slice(c * rc, (c + 1) * rc) for c in range(n_vis // rc)]

        m8 = fold8(s_buf[chunks[0], :], jnp.max)
        for rows in chunks[1:]:
            m8 = jnp.maximum(m8, fold8(s_buf[rows, :], jnp.max))
        m = jnp.max(m8, axis=0, keepdims=True)

        l8 = jnp.zeros((8, w), F32)
        for rows in chunks:
            p = jnp.exp2(s_buf[rows, :] - m)
            s_buf[rows, :] = p
            l8 = l8 + fold8(p, jnp.sum)
        rinv = 1.0 / jnp.sum(l8, axis=0, keepdims=True)
        r1 = rinv[:, :tqp]
        r2 = rinv[:, tqp:] * lam

        for rows in chunks:
            p = s_buf[rows, :]
            a_buf[rows, :] = (p[:, :tqp] * r1 - p[:, tqp:] * r2).astype(BF16)
        o = _dot(vt_scr[:, 0:n_vis], a_buf[0:n_vis, :]).T[:tq]
        y = _rms(o, g_ref[...]) * (1.0 - lam_init) * z_ref[rows_q, :].astype(F32)
        y_ref[rows_q, :] = y.astype(BF16)


def _diff_attn(lam_vecs, q_arr, q_cg, k_arr, k_cg, v_arr, v_cg, z_arr, z_cg, g_dsub,
               *, batch, seq, s_len, n_valid, causal, tq, rc, lam_init):
    tqp = max(tq, LANE)
    bt = tq if causal else rc

    def col(rows, cg):
        return pl.BlockSpec((rows, LANE), lambda b, h: (b, cg * DF_HEADS + h))

    kern = functools.partial(_diffattn_kernel, n_q=seq // tq, tq=tq, tqp=tqp, rc=rc, bt=bt, s_len=s_len,
                             n_valid=n_valid, causal=causal, lam_init=lam_init)
    return pl.pallas_call(
        kern,
        grid=(batch, DF_HEADS),
        in_specs=[pl.BlockSpec((4, DF_HD), lambda b, h: (0, 0)),
                  col(seq, q_cg), col(s_len, k_cg), col(s_len, v_cg), col(seq, z_cg),
                  pl.BlockSpec((1, DF_VD), lambda b, h: (0, 0))],
        out_specs=pl.BlockSpec((seq, LANE), lambda b, h: (b, h)),
        out_shape=jax.ShapeDtypeStruct((batch * seq, D_MODEL), BF16),
        scratch_shapes=[pltpu.VMEM((2, s_len, 2 * tqp), F32), pltpu.VMEM((2, s_len, tqp), BF16),
                        pltpu.VMEM((LANE, s_len), BF16)],
        compiler_params=_params("arbitrary", "arbitrary"),
        name="diff_attn",
    )(lam_vecs, q_arr, k_arr, v_arr, z_arr, g_dsub.reshape(1, -1))


def _merge_kernel(x_ref, ya_ref, yb_ref, qm_ref, zm_ref, g0_ref, g1_ref, g2_ref, mk_ref, mv_ref,
                  wb_ref, wo_ref, y_ref):
    ym = []
    for h in range(MEM_HEADS):
        sl = slice(h * MEM_HD, (h + 1) * MEM_HD)
        s = _dot_nt(qm_ref[:, sl], mk_ref[:, sl].astype(BF16))
        p = jnp.exp(s - jnp.max(s, axis=-1, keepdims=True))
        rinv = 1.0 / jnp.sum(p, axis=-1, keepdims=True)
        o = _dot(p.astype(BF16), mv_ref[:, sl].astype(BF16)) * rinv
        ym.append((o * zm_ref[:, sl].astype(F32)).astype(BF16))
    ym = jnp.concatenate(ym, axis=-1)
    hsum = (g0_ref[...].astype(F32) * _dot(ya_ref[...], wb_ref[0])
            + g1_ref[...].astype(F32) * _dot(yb_ref[...], wb_ref[1])
            + g2_ref[...].astype(F32) * _dot(ym, wb_ref[2]))
    y_ref[...] = x_ref[...] + _dot(hsum.astype(BF16), wo_ref[...])


def _merge(x, ya, yb, p, mem_k, mem_v, w_branch, w_out, *, seq, n_mem):
    n = x.shape[0]
    tm = min(256, seq)
    per_b = seq // tm
    row = lambda i: (i, 0)

    def col(cg):
        return pl.BlockSpec((tm, D_MODEL), lambda i: (i, cg))

    mem = pl.BlockSpec((n_mem, D_MODEL), lambda i: (i // per_b, 0))
    return pl.pallas_call(
        _merge_kernel,
        grid=(n // tm,),
        in_specs=[pl.BlockSpec((tm, D_MODEL), row), pl.BlockSpec((tm, D_MODEL), row), pl.BlockSpec((tm, D_MODEL), row),
                  col(CG_MQ), col(CG_ZM), col(CG_G0), col(CG_G0 + 1), col(CG_G0 + 2), mem, mem,
                  pl.BlockSpec((N_BRANCH, D_MODEL, D_MODEL), lambda i: (0, 0, 0)),
                  pl.BlockSpec((D_MODEL, D_MODEL), lambda i: (0, 0))],
        out_specs=pl.BlockSpec((tm, D_MODEL), row),
        out_shape=jax.ShapeDtypeStruct((n, D_MODEL), F32),
        compiler_params=_params("arbitrary"),
        name="merge",
    )(x, ya, yb, p, p, p, p, p, mem_k, mem_v, w_branch, w_out)


def _layer(x, s0, past_k, past_v, mem_k, mem_v, lam_init, w, n_mem):
    batch, seq, _ = x.shape
    n = batch * seq
    xf = x.reshape(n, D_MODEL)
    prompt = past_k is None
    p, logf, k32, v32 = _in_proj(xf, w["g_norm"], w["w_in"], w["lb_logits"], w["g_dq"], w["g_dk"], w["g_mq"],
                                 seq=seq, k_transposed=prompt)
    ya, s_new = _hgrn(p, logf, s0, w["g_hg_out"], batch, seq)
    if prompt:
        yb = _diff_attn(w["lam_vecs"], p, CG_DQ, p, CG_DK, p, CG_DV, p, CG_ZB, w["g_dsub"],
                        batch=batch, seq=seq, s_len=seq, n_valid=seq, causal=True, tq=256, rc=256, lam_init=lam_init)
    else:
        past = past_k.shape[1]
        tk = LANE
        n_valid = past + seq
        s_len = -(-n_valid // tk) * tk
        pad = jnp.zeros((batch, s_len - n_valid, D_MODEL), BF16)
        k_all = jnp.concatenate([past_k.reshape(batch, past, D_MODEL).astype(BF16),
                                 p[:, CG_DK * D_MODEL:(CG_DK + 1) * D_MODEL].reshape(batch, seq, D_MODEL), pad], axis=1)
        v_all = jnp.concatenate([past_v.reshape(batch, past, D_MODEL).astype(BF16),
                                 p[:, CG_DV * D_MODEL:(CG_DV + 1) * D_MODEL].reshape(batch, seq, D_MODEL), pad], axis=1)
        yb = _diff_attn(w["lam_vecs"], p, CG_DQ, k_all.reshape(batch * s_len, D_MODEL), 0,
                        v_all.reshape(batch * s_len, D_MODEL), 0, p, CG_ZB, w["g_dsub"],
                        batch=batch, seq=seq, s_len=s_len, n_valid=n_valid, causal=False, tq=seq, rc=tk,
                        lam_init=lam_init)
    y = _merge(xf, ya, yb, p, mem_k, mem_v, w["w_branch"], w["w_out"], seq=seq, n_mem=n_mem)
    if prompt:
        k_out = jnp.transpose(k32.reshape(batch, DF_HEADS, 2, DF_HD, seq), (0, 4, 1, 2, 3))
    else:
        k_out = k32.reshape(batch, seq, DF_HEADS, 2, DF_HD)
    return y.reshape(batch, seq, D_MODEL), s_new, k_out, v32.reshape(batch, seq, DF_HEADS, DF_VD)


def kernel(x_prompt, x_sample, mem_prompt, cache_diff_k, cache_diff_v, cache_mem_k, cache_mem_v, state_hgrn,
           g_norm, w_in, hg_lb_logits, g_hg_out, g_dq, g_dk, lam_q1, lam_k1, lam_q2, lam_k2, g_dsub, g_mem,
           w_mkv, g_mq, g_mk, w_branch, w_out):
    depth = g_norm.shape[0]
    assert depth == 1 and hg_lb_logits.shape[0] == 2, "single-layer step only"
    batch, n_mem = mem_prompt.shape[:2]
    dec_batch = x_sample.shape[0]
    lam_init = 0.8 - 0.6
    w_in_groups = w_in[0].astype(BF16).reshape(D_MODEL, N_COLGROUPS, D_MODEL).transpose(1, 0, 2)
    w = dict(g_norm=g_norm[0], w_in=w_in_groups, lb_logits=hg_lb_logits, g_hg_out=g_hg_out[0],
             g_dq=g_dq[0], g_dk=g_dk[0], g_mq=g_mq[0], g_dsub=g_dsub[0],
             lam_vecs=jnp.stack([lam_q1[0], lam_k1[0], lam_q2[0], lam_k2[0]]),
             w_branch=w_branch[0].astype(BF16), w_out=w_out[0].astype(BF16))

    mem_k, mem_v = _memory_kv(mem_prompt.reshape(batch * n_mem, D_MODEL), g_mem[0], w_mkv[0], g_mk[0])
    s0 = jnp.zeros((batch, HG_HEADS, HG_D, HG_D), F32)
    y_p, s_p, k_p, v_p = _layer(x_prompt, s0, None, None, mem_k, mem_v, lam_init, w, n_mem)
    y_s, s_s, k_s, v_s = _layer(x_sample, state_hgrn[0], cache_diff_k[0], cache_diff_v[0],
                                cache_mem_k[0].reshape(dec_batch * n_mem, D_MODEL),
                                cache_mem_v[0].reshape(dec_batch * n_mem, D_MODEL), lam_init, w, n_mem)
    mem_shape = (1, batch, n_mem, MEM_HEADS, MEM_HD)
    return (y_p, y_s, s_p[None], s_s[None], k_p[None], v_p[None], k_s[None], v_s[None],
            mem_k.reshape(mem_shape), mem_v.reshape(mem_shape))
```

```python
import functools

import jax
import jax.numpy as jnp
from jax import lax
from jax.experimental import pallas as pl
from jax.experimental.pallas import tpu as pltpu

F32 = jnp.float32
BF16 = jnp.bfloat16

EPS = 1e-6
LOG2E = 1.4426950408889634
D_MODEL = 1024
CHUNK = 64
GLA_BLOCK = 16
HG_HEADS, HG_D = 8, 128
DF_HEADS, DF_HD, DF_VD = 8, 64, 128
MEM_HEADS, MEM_HD = 4, 256
N_BRANCH = 3
N_COLGROUPS = 13
(CG_HQ, CG_HF, CG_HI, CG_ZA, CG_DQ, CG_DK, CG_DV, CG_ZB, CG_MQ, CG_ZM, CG_G0) = range(11)

LANE = 128
VMEM_LIMIT = 52 * 1024 * 1024
VMEM_LIMIT_INPROJ = 60 * 1024 * 1024

NT = (((1,), (1,)), ((), ()))
TN = (((0,), (0,)), ((), ()))


def _dot(a, b):
    return jnp.dot(a, b, preferred_element_type=F32)


def _dot_nt(a, b):
    return lax.dot_general(a, b, NT, preferred_element_type=F32)


def _dot_tn(a, b):
    return lax.dot_general(a, b, TN, preferred_element_type=F32)


def _sigmoid(x):
    return 0.5 * jnp.tanh(0.5 * x) + 0.5


def _rms(x, g):
    return x * lax.rsqrt(jnp.mean(x * x, axis=-1, keepdims=True) + EPS) * g


def _params(*sem, vmem=VMEM_LIMIT):
    return pltpu.CompilerParams(dimension_semantics=sem, vmem_limit_bytes=vmem)


def _memkv_kernel(mem_ref, g_ref, w_ref, gk_ref, k_ref, v_ref):
    xn = _rms(mem_ref[...], g_ref[...]).astype(BF16)
    kv = _dot(xn, w_ref[...])
    for h in range(MEM_HEADS):
        sl = slice(h * MEM_HD, (h + 1) * MEM_HD)
        k_ref[0, :, h, :] = _rms(kv[:, sl], gk_ref[...])
        v_ref[0, :, h, :] = kv[:, D_MODEL + h * MEM_HD:D_MODEL + (h + 1) * MEM_HD]


def _memory_kv(mem, g_mem, w_mkv, g_mk, *, batch, n_mem):
    out = pl.BlockSpec((1, n_mem, MEM_HEADS, MEM_HD), lambda i: (i, 0, 0, 0))
    return pl.pallas_call(
        _memkv_kernel,
        grid=(batch,),
        in_specs=[
            pl.BlockSpec((n_mem, D_MODEL), lambda i: (i, 0)),
            pl.BlockSpec((1, D_MODEL), lambda i: (0, 0)),
            pl.BlockSpec((D_MODEL, 2 * D_MODEL), lambda i: (0, 0)),
            pl.BlockSpec((1, MEM_HD), lambda i: (0, 0)),
        ],
        out_specs=[out, out],
        out_shape=[jax.ShapeDtypeStruct((batch, n_mem, MEM_HEADS, MEM_HD), F32)] * 2,
        compiler_params=_params("arbitrary"),
        name="memory_kv",
    )(mem, g_mem.reshape(1, -1), w_mkv.astype(BF16), g_mk.reshape(1, -1))


def _half_head_rms(acc, g_ref, scale):
    lane = lax.broadcasted_iota(jnp.int32, (1, LANE), 1)
    lo = lane < DF_HD
    outs = []
    for h in range(DF_HEADS):
        sl = slice(h * LANE, (h + 1) * LANE)
        xh = acc[:, sl]
        sq = xh * xh
        s_lo = jnp.sum(jnp.where(lo, sq, 0.0), axis=-1, keepdims=True)
        s_hi = jnp.sum(jnp.where(lo, 0.0, sq), axis=-1, keepdims=True)
        r = jnp.where(lo, lax.rsqrt(s_lo * (1.0 / DF_HD) + EPS), lax.rsqrt(s_hi * (1.0 / DF_HD) + EPS))
        outs.append(xh * r * (g_ref[:, sl] * scale))
    return outs


def _inproj_kernel(x_ref, gn_ref, w_ref, lbl_ref, gq_ref, gk_ref, gmq_ref,
                   p_ref, logf_ref, k_ref, v_ref, xn_scr, *, k_transposed):
    xn_scr[...] = _rms(x_ref[...], gn_ref[...]).astype(BF16)

    def proj(j):
        return _dot(xn_scr[...], w_ref[j])

    def put(j, val):
        p_ref[:, j * D_MODEL:(j + 1) * D_MODEL] = val.astype(BF16)

    put(CG_HQ, proj(CG_HQ))

    l = lbl_ref[...]
    e = jnp.exp(l - jnp.max(l, axis=0, keepdims=True))
    lb = e[0:1] / jnp.sum(e, axis=0, keepdims=True)
    sp = _sigmoid(proj(CG_HF))
    logf_ref[...] = jnp.log(lb + (1.0 - lb) * sp)
    put(CG_HF, (1.0 - lb) * (1.0 - sp))

    put(CG_HI, proj(CG_HI))

    for j in (CG_ZA, CG_ZB, CG_ZM):
        acc = proj(j)
        put(j, acc * _sigmoid(acc))

    outs = _half_head_rms(proj(CG_DQ), gq_ref, DF_HD ** -0.5 * LOG2E)
    for h in range(DF_HEADS):
        p_ref[:, CG_DQ * D_MODEL + h * LANE:CG_DQ * D_MODEL + (h + 1) * LANE] = outs[h].astype(BF16)

    outs = _half_head_rms(proj(CG_DK), gk_ref, 1.0)
    for h in range(DF_HEADS):
        sl = slice(h * LANE, (h + 1) * LANE)
        p_ref[:, CG_DK * D_MODEL + h * LANE:CG_DK * D_MODEL + (h + 1) * LANE] = outs[h].astype(BF16)
        if k_transposed:
            k_ref[0, sl, :] = outs[h].T
        else:
            k_ref[:, sl] = outs[h]

    acc = proj(CG_DV)
    put(CG_DV, acc)
    v_ref[...] = acc

    acc = proj(CG_MQ)
    for h in range(MEM_HEADS):
        sl = slice(h * MEM_HD, (h + 1) * MEM_HD)
        p_ref[:, CG_MQ * D_MODEL + h * MEM_HD:CG_MQ * D_MODEL + (h + 1) * MEM_HD] = (
            _rms(acc[:, sl], gmq_ref[...]) * (MEM_HD ** -0.5)).astype(BF16)

    for j in range(CG_G0, N_COLGROUPS):
        put(j, _sigmoid(proj(j)))


def _in_proj(x, g_norm, w_in, lb_logits, g_dq, g_dk, g_mq, *, seq, k_transposed):
    n = x.shape[0]
    tm = min(256, seq) if k_transposed else min(256, n)
    row = lambda i: (i, 0)
    const = lambda i: (0, 0)
    if k_transposed:
        per_b = seq // tm
        k_spec = pl.BlockSpec((1, D_MODEL, tm), lambda i: (i // per_b, 0, i % per_b))
        k_shape = jax.ShapeDtypeStruct((n // seq, D_MODEL, seq), F32)
    else:
        k_spec = pl.BlockSpec((tm, D_MODEL), row)
        k_shape = jax.ShapeDtypeStruct((n, D_MODEL), F32)
    return pl.pallas_call(
        functools.partial(_inproj_kernel, k_transposed=k_transposed),
        grid=(n // tm,),
        in_specs=[
            pl.BlockSpec((tm, D_MODEL), row),
            pl.BlockSpec((1, D_MODEL), const),
            pl.BlockSpec((N_COLGROUPS, D_MODEL, D_MODEL), lambda i: (0, 0, 0), pipeline_mode=pl.Buffered(1)),
            pl.BlockSpec((2, D_MODEL), const),
            pl.BlockSpec((1, D_MODEL), const),
            pl.BlockSpec((1, D_MODEL), const),
            pl.BlockSpec((1, MEM_HD), const),
        ],
        out_specs=[
            pl.BlockSpec((tm, N_COLGROUPS * D_MODEL), row),
            pl.BlockSpec((tm, D_MODEL), row),
            k_spec,
            pl.BlockSpec((tm, D_MODEL), row),
        ],
        out_shape=[
            jax.ShapeDtypeStruct((n, N_COLGROUPS * D_MODEL), BF16),
            jax.ShapeDtypeStruct((n, D_MODEL), F32),
            k_shape,
            jax.ShapeDtypeStruct((n, D_MODEL), F32),
        ],
        scratch_shapes=[pltpu.VMEM((tm, D_MODEL), BF16)],
        compiler_params=_params("arbitrary", vmem=VMEM_LIMIT_INPROJ),
        name="in_proj",
    )(x, g_norm.reshape(1, -1), w_in, lb_logits,
      jnp.tile(g_dq, 2 * DF_HEADS).reshape(1, -1), jnp.tile(g_dk, 2 * DF_HEADS).reshape(1, -1), g_mq.reshape(1, -1))


def _split_bf16(x):
    hi = x.astype(BF16)
    return hi, (x - hi.astype(F32)).astype(BF16)


def _hgrn_kernel(q_ref, k_ref, v_ref, z_ref, lf_ref, s0_ref, g_ref, y_ref, s_ref, st_scr, *, seq, tc, hp):
    nb = tc // GLA_BLOCK
    row = lax.broadcasted_iota(jnp.int32, (tc, tc), 0)
    col = lax.broadcasted_iota(jnp.int32, (tc, tc), 1)
    causal = col <= row
    tril = jnp.where(causal, 1.0, 0.0).astype(BF16)
    rowk = lax.broadcasted_iota(jnp.int32, (tc, LANE), 0)

    @pl.when(pl.program_id(1) == 0)
    def _():
        for h in range(hp):
            st_scr[h] = s0_ref[0, h].T

    n_chunks = seq // tc
    unroll = 2 if n_chunks % 2 == 0 else 1

    stacked_att = tc % LANE == 0
    tril2 = jnp.concatenate([tril, tril], axis=1)

    def chunks(c, carry):
        for u in range(unroll):
            rows = pl.ds(pl.multiple_of((c * unroll + u) * tc, tc), tc)
            g_hi, g_lo = _split_bf16(lf_ref[rows, :])
            b_all = _dot(tril2, jnp.concatenate([g_hi, g_lo], axis=0))
            for h in range(hp):
                sl = slice(h * LANE, (h + 1) * LANE)
                st = st_scr[h]
                b = b_all[:, sl]
                b_end = b[tc - 1:tc, :]
                q = q_ref[rows, sl].astype(F32)
                k = k_ref[rows, sl].astype(F32)
                v = v_ref[rows, sl]
                o = _dot_nt((q * jnp.exp(b)).astype(BF16), st.astype(BF16))
                k_dec = (k * jnp.exp(b_end - b)).astype(BF16)
                st_scr[h] = jnp.exp(b_end) * st + _dot_tn(v, k_dec)
                starts = [b[n * GLA_BLOCK - 1:n * GLA_BLOCK, :] if n else jnp.zeros((1, LANE), F32) for n in range(nb)]
                k_inv = [(k * jnp.exp(jnp.where(rowk < (n + 1) * GLA_BLOCK, starts[n] - b, 0.0))).astype(BF16)
                         for n in range(nb)]
                if stacked_att:
                    g_rows = jnp.concatenate([jnp.broadcast_to(g, (GLA_BLOCK, LANE)) for g in starts], axis=0)
                    q_dec = (q * jnp.exp(b - g_rows)).astype(BF16)
                    wide = _dot_nt(q_dec, jnp.concatenate(k_inv, axis=0))
                    att = [wide[n * GLA_BLOCK:(n + 1) * GLA_BLOCK, n * tc:(n + 1) * tc] for n in range(nb)]
                else:
                    att = [_dot_nt((q[n * GLA_BLOCK:(n + 1) * GLA_BLOCK]
                                    * jnp.exp(b[n * GLA_BLOCK:(n + 1) * GLA_BLOCK] - starts[n])).astype(BF16), k_inv[n])
                           for n in range(nb)]
                a = jnp.where(causal, jnp.concatenate(att, axis=0), 0.0).astype(BF16)
                o = o + _dot(a, v)
                y = _rms(o, g_ref[...]) * z_ref[rows, sl].astype(F32)
                y_ref[rows, sl] = y.astype(BF16)
        return carry

    lax.fori_loop(0, n_chunks // unroll, chunks, 0)

    @pl.when(pl.program_id(1) == pl.num_programs(1) - 1)
    def _():
        for h in range(hp):
            s_ref[0, h] = st_scr[h].T


def _hgrn(p, logf, s0, g_hg_out, batch, seq):
    hp = HG_HEADS
    ts = min(512, seq)
    tc = min(128, ts)
    per_b = seq // ts

    def col(cg):
        return pl.BlockSpec((ts, D_MODEL), lambda b, t: (b * per_b + t, cg))

    st = pl.BlockSpec((1, hp, HG_D, HG_D), lambda b, t: (b, 0, 0, 0))
    return pl.pallas_call(
        functools.partial(_hgrn_kernel, seq=ts, tc=tc, hp=hp),
        grid=(batch, per_b),
        in_specs=[col(CG_HQ), col(CG_HF), col(CG_HI), col(CG_ZA), col(0), st,
                  pl.BlockSpec((1, HG_D), lambda b, t: (0, 0))],
        out_specs=[col(0), st],
        out_shape=[jax.ShapeDtypeStruct((batch * seq, D_MODEL), BF16),
                   jax.ShapeDtypeStruct((batch, HG_HEADS, HG_D, HG_D), F32)],
        scratch_shapes=[pltpu.VMEM((hp, HG_D, HG_D), F32)],
        compiler_params=_params("arbitrary", "arbitrary"),
        name="hgrn",
    )(p, p, p, p, logf, s0, g_hg_out.reshape(1, -1))


def _lambda(lam_ref, lam_init):
    lv = lam_ref[...]
    return (jnp.exp(jnp.sum(lv[0:1] * lv[1:2], axis=-1, keepdims=True))
            - jnp.exp(jnp.sum(lv[2:3] * lv[3:4], axis=-1, keepdims=True)) + lam_init)


def _split_maps(q):
    first = lax.broadcasted_iota(jnp.int32, (1, LANE), 1) < DF_HD
    zero = jnp.zeros((), q.dtype)
    return jnp.concatenate([jnp.where(first, q, zero), jnp.where(first, zero, q)], axis=0)


def _diffattn_kernel(lam_ref, q_ref, k_ref, v_ref, z_ref, g_ref, y_ref, s_scr, a_scr, vt_scr,
                     *, n_q, tq, rc, lam_init):
    w = 2 * tq
    lam = _lambda(lam_ref, lam_init)
    vt_scr[...] = v_ref[...].astype(F32).T.astype(BF16)

    key = lax.broadcasted_iota(jnp.int32, (tq, w), 0)
    qry = lax.broadcasted_iota(jnp.int32, (tq, w), 1) & (tq - 1)
    bias = jnp.where((key // CHUNK) <= (qry // CHUNK), 0.0, -jnp.inf)

    def fold8(x, op):
        return op(x.reshape(rc // 8, 8, w), axis=0)

    for i in range(n_q):
        n_vis = (i + 1) * tq
        s_buf, a_buf = s_scr.at[i % 2], a_scr.at[i % 2]
        rows_q = slice(i * tq, (i + 1) * tq)
        s_buf[0:n_vis, :] = _dot_nt(k_ref[0:n_vis, :], _split_maps(q_ref[rows_q, :]))
        s_buf[n_vis - tq:n_vis, :] = s_buf[n_vis - tq:n_vis, :] + bias
        chunks = [slice(c * rc, (c + 1) * rc) for c in range(n_vis // rc)]

        m8 = fold8(s_buf[chunks[0], :], jnp.max)
        for rows in chunks[1:]:
            m8 = jnp.maximum(m8, fold8(s_buf[rows, :], jnp.max))
        m = jnp.max(m8, axis=0, keepdims=True)

        l8 = jnp.zeros((8, w), F32)
        for rows in chunks:
            p = jnp.exp2(s_buf[rows, :] - m)
            s_buf[rows, :] = p
            l8 = l8 + fold8(p, jnp.sum)
        rinv = 1.0 / jnp.sum(l8, axis=0, keepdims=True)
        r1 = rinv[:, :tq]
        r2 = rinv[:, tq:] * lam

        for rows in chunks:
            p = s_buf[rows, :]
            a_buf[rows, :] = (p[:, :tq] * r1 - p[:, tq:] * r2).astype(BF16)
        o = _dot(vt_scr[:, 0:n_vis], a_buf[0:n_vis, :]).T
        y = _rms(o, g_ref[...]) * (1.0 - lam_init) * z_ref[rows_q, :].astype(F32)
        y_ref[rows_q, :] = y.astype(BF16)


def _diff_attn(lam_vecs, p, g_dsub, *, batch, seq, lam_init):
    tq = rc = 256

    def col(cg):
        return pl.BlockSpec((seq, LANE), lambda b, h: (b, cg * DF_HEADS + h))

    return pl.pallas_call(
        functools.partial(_diffattn_kernel, n_q=seq // tq, tq=tq, rc=rc, lam_init=lam_init),
        grid=(batch, DF_HEADS),
        in_specs=[pl.BlockSpec((4, DF_HD), lambda b, h: (0, 0)),
                  col(CG_DQ), col(CG_DK), col(CG_DV), col(CG_ZB),
                  pl.BlockSpec((1, DF_VD), lambda b, h: (0, 0))],
        out_specs=pl.BlockSpec((seq, LANE), lambda b, h: (b, h)),
        out_shape=jax.ShapeDtypeStruct((batch * seq, D_MODEL), BF16),
        scratch_shapes=[pltpu.VMEM((2, seq, 2 * tq), F32), pltpu.VMEM((2, seq, tq), BF16),
                        pltpu.VMEM((LANE, seq), BF16)],
        compiler_params=_params("arbitrary", "arbitrary"),
        name="diff_attn",
    )(lam_vecs, p, p, p, p, g_dsub.reshape(1, -1))


def _cached_diffattn_kernel(lam_ref, q_ref, kn_ref, vn_ref, z_ref, g_ref, kt_ref, vc_ref, y_ref, *, tq, lam_init):
    lam = _lambda(lam_ref, lam_init)
    for h in range(DF_HEADS):
        sl = slice(h * LANE, (h + 1) * LANE)
        qs = _split_maps(q_ref[:, sl])
        s_c = _dot(qs, kt_ref[0, sl, :].astype(BF16))
        s_n = _dot_nt(qs, kn_ref[:, sl])
        m = jnp.maximum(jnp.max(s_c, axis=-1, keepdims=True), jnp.max(s_n, axis=-1, keepdims=True))
        p_c = jnp.exp2(s_c - m)
        p_n = jnp.exp2(s_n - m)
        rinv = 1.0 / (jnp.sum(p_c, axis=-1, keepdims=True) + jnp.sum(p_n, axis=-1, keepdims=True))
        r1 = rinv[:tq]
        r2 = rinv[tq:] * lam
        a_c = (p_c[:tq] * r1 - p_c[tq:] * r2).astype(BF16)
        a_n = (p_n[:tq] * r1 - p_n[tq:] * r2).astype(BF16)
        o = _dot(a_c, vc_ref[0, :, h, :].astype(BF16)) + _dot(a_n, vn_ref[:, sl])
        y = _rms(o, g_ref[...]) * (1.0 - lam_init) * z_ref[:, sl].astype(F32)
        y_ref[:, sl] = y.astype(BF16)


def _cached_diff_attn(lam_vecs, p, g_dsub, kt_cache, v_cache, *, batch, seq, lam_init):
    past = v_cache.shape[1]

    def col(cg):
        return pl.BlockSpec((seq, D_MODEL), lambda b: (b, cg))

    return pl.pallas_call(
        functools.partial(_cached_diffattn_kernel, tq=seq, lam_init=lam_init),
        grid=(batch,),
        in_specs=[pl.BlockSpec((4, DF_HD), lambda b: (0, 0)),
                  col(CG_DQ), col(CG_DK), col(CG_DV), col(CG_ZB),
                  pl.BlockSpec((1, DF_VD), lambda b: (0, 0)),
                  pl.BlockSpec((1, D_MODEL, past), lambda b: (b, 0, 0)),
                  pl.BlockSpec((1, past, DF_HEADS, DF_VD), lambda b: (b, 0, 0, 0))],
        out_specs=pl.BlockSpec((seq, D_MODEL), lambda b: (b, 0)),
        out_shape=jax.ShapeDtypeStruct((batch * seq, D_MODEL), BF16),
        compiler_params=_params("arbitrary"),
        name="cached_diff_attn",
    )(lam_vecs, p, p, p, p, g_dsub.reshape(1, -1), kt_cache, v_cache)


def _merge_kernel(x_ref, ya_ref, yb_ref, qm_ref, zm_ref, g0_ref, g1_ref, g2_ref, mk_ref, mv_ref,
                  wb_ref, wo_ref, y_ref):
    ym = []
    for h in range(MEM_HEADS):
        sl = slice(h * MEM_HD, (h + 1) * MEM_HD)
        s = _dot_nt(qm_ref[:, sl], mk_ref[0, :, h, :].astype(BF16))
        p = jnp.exp(s - jnp.max(s, axis=-1, keepdims=True))
        rinv = 1.0 / jnp.sum(p, axis=-1, keepdims=True)
        o = _dot(p.astype(BF16), mv_ref[0, :, h, :].astype(BF16)) * rinv
        ym.append((o * zm_ref[:, sl].astype(F32)).astype(BF16))
    ym = jnp.concatenate(ym, axis=-1)
    hsum = (g0_ref[...].astype(F32) * _dot(ya_ref[...], wb_ref[0])
            + g1_ref[...].astype(F32) * _dot(yb_ref[...], wb_ref[1])
            + g2_ref[...].astype(F32) * _dot(ym, wb_ref[2]))
    y_ref[...] = x_ref[...] + _dot(hsum.astype(BF16), wo_ref[...])


def _merge(x, ya, yb, p, mem_k, mem_v, w_branch, w_out, *, seq, n_mem):
    n = x.shape[0]
    tm = min(512, seq)
    per_b = seq // tm
    row = lambda i: (i, 0)

    def col(cg):
        return pl.BlockSpec((tm, D_MODEL), lambda i: (i, cg))

    mem = pl.BlockSpec((1, n_mem, MEM_HEADS, MEM_HD), lambda i: (i // per_b, 0, 0, 0))
    return pl.pallas_call(
        _merge_kernel,
        grid=(n // tm,),
        in_specs=[pl.BlockSpec((tm, D_MODEL), row), pl.BlockSpec((tm, D_MODEL), row), pl.BlockSpec((tm, D_MODEL), row),
                  col(CG_MQ), col(CG_ZM), col(CG_G0), col(CG_G0 + 1), col(CG_G0 + 2), mem, mem,
                  pl.BlockSpec((N_BRANCH, D_MODEL, D_MODEL), lambda i: (0, 0, 0), pipeline_mode=pl.Buffered(1)),
                  pl.BlockSpec((D_MODEL, D_MODEL), lambda i: (0, 0), pipeline_mode=pl.Buffered(1))],
        out_specs=pl.BlockSpec((tm, D_MODEL), row),
        out_shape=jax.ShapeDtypeStruct((n, D_MODEL), F32),
        compiler_params=_params("arbitrary"),
        name="merge",
    )(x, ya, yb, p, p, p, p, p, mem_k, mem_v, w_branch, w_out)


def _layer(x, s0, past_k, past_v, mem_k, mem_v, lam_init, w, n_mem):
    batch, seq, _ = x.shape
    n = batch * seq
    xf = x.reshape(n, D_MODEL)
    prompt = past_k is None
    p, logf, k32, v32 = _in_proj(xf, w["g_norm"], w["w_in"], w["lb_logits"], w["g_dq"], w["g_dk"], w["g_mq"],
                                 seq=seq, k_transposed=prompt)
    ya, s_new = _hgrn(p, logf, s0, w["g_hg_out"], batch, seq)
    if prompt:
        yb = _diff_attn(w["lam_vecs"], p, w["g_dsub"], batch=batch, seq=seq, lam_init=lam_init)
    else:
        past = past_k.shape[1]
        kt_cache = jnp.transpose(past_k, (0, 2, 3, 4, 1)).reshape(batch, D_MODEL, past)
        yb = _cached_diff_attn(w["lam_vecs"], p, w["g_dsub"], kt_cache, past_v, batch=batch, seq=seq,
                               lam_init=lam_init)
    y = _merge(xf, ya, yb, p, mem_k, mem_v, w["w_branch"], w["w_out"], seq=seq, n_mem=n_mem)
    if prompt:
        k_out = jnp.transpose(k32.reshape(batch, DF_HEADS, 2, DF_HD, seq), (0, 4, 1, 2, 3))
    else:
        k_out = k32.reshape(batch, seq, DF_HEADS, 2, DF_HD)
    return y.reshape(batch, seq, D_MODEL), s_new, k_out, v32.reshape(batch, seq, DF_HEADS, DF_VD)


def kernel(x_prompt, x_sample, mem_prompt, cache_diff_k, cache_diff_v, cache_mem_k, cache_mem_v, state_hgrn,
           g_norm, w_in, hg_lb_logits, g_hg_out, g_dq, g_dk, lam_q1, lam_k1, lam_q2, lam_k2, g_dsub, g_mem,
           w_mkv, g_mq, g_mk, w_branch, w_out):
    depth = g_norm.shape[0]
    assert depth == 1 and hg_lb_logits.shape[0] == 2, "single-layer step only"
    batch, n_mem = mem_prompt.shape[:2]
    lam_init = 0.8 - 0.6
    w_in_groups = w_in[0].astype(BF16).reshape(D_MODEL, N_COLGROUPS, D_MODEL).transpose(1, 0, 2)
    w = dict(g_norm=g_norm[0], w_in=w_in_groups, lb_logits=hg_lb_logits, g_hg_out=g_hg_out[0],
             g_dq=g_dq[0], g_dk=g_dk[0], g_mq=g_mq[0], g_dsub=g_dsub[0],
             lam_vecs=jnp.stack([lam_q1[0], lam_k1[0], lam_q2[0], lam_k2[0]]),
             w_branch=w_branch[0].astype(BF16), w_out=w_out[0].astype(BF16))

    mem_k, mem_v = _memory_kv(mem_prompt.reshape(batch * n_mem, D_MODEL), g_mem[0], w_mkv[0], g_mk[0],
                              batch=batch, n_mem=n_mem)
    s0 = jnp.zeros((batch, HG_HEADS, HG_D, HG_D), F32)
    y_p, s_p, k_p, v_p = _layer(x_prompt, s0, None, None, mem_k, mem_v, lam_init, w, n_mem)
    y_s, s_s, k_s, v_s = _layer(x_sample, state_hgrn[0], cache_diff_k[0], cache_diff_v[0],
                                cache_mem_k[0], cache_mem_v[0], lam_init, w, n_mem)
    return (y_p, y_s, s_p[None], s_s[None], k_p[None], v_p[None], k_s[None], v_s[None], mem_k[None], mem_v[None])
```

```python
import functools

import jax
import jax.numpy as jnp
from jax import lax
from jax.experimental import pallas as pl
from jax.experimental.pallas import tpu as pltpu

F32 = jnp.float32
BF16 = jnp.bfloat16

EPS = 1e-6
LOG2E = 1.4426950408889634
D_MODEL = 1024
CHUNK = 64
GLA_BLOCK = 16
HG_HEADS, HG_D = 8, 128
DF_HEADS, DF_HD, DF_VD = 8, 64, 128
MEM_HEADS, MEM_HD = 4, 256
N_BRANCH = 3
N_COLGROUPS = 13
(CG_HQ, CG_HF, CG_HI, CG_ZA, CG_DQ, CG_DK, CG_DV, CG_ZB, CG_MQ, CG_ZM, CG_G0) = range(11)

LANE = 128
ONES_ROWS = 16
VMEM_LIMIT = 52 * 1024 * 1024
VMEM_LIMIT_INPROJ = 60 * 1024 * 1024

NT = (((1,), (1,)), ((), ()))
TN = (((0,), (0,)), ((), ()))


def _dot(a, b):
    return jnp.dot(a, b, preferred_element_type=F32)


def _dot_nt(a, b):
    return lax.dot_general(a, b, NT, preferred_element_type=F32)


def _dot_tn(a, b):
    return lax.dot_general(a, b, TN, preferred_element_type=F32)


def _sigmoid(x):
    return 0.5 * jnp.tanh(0.5 * x) + 0.5


def _rms(x, g):
    return x * lax.rsqrt(jnp.mean(x * x, axis=-1, keepdims=True) + EPS) * g


def _params(*sem, vmem=VMEM_LIMIT):
    return pltpu.CompilerParams(dimension_semantics=sem, vmem_limit_bytes=vmem)


def _memkv_kernel(mem_ref, g_ref, w_ref, gk_ref, k_ref, v_ref, kb_ref, vb_ref):
    xn = _rms(mem_ref[...], g_ref[...]).astype(BF16)
    kv = _dot(xn, w_ref[...])
    for h in range(MEM_HEADS):
        sl = slice(h * MEM_HD, (h + 1) * MEM_HD)
        k = _rms(kv[:, sl], gk_ref[...])
        k_ref[0, :, h, :] = k
        kb_ref[:, sl] = k.astype(BF16)
        v_ref[0, :, h, :] = kv[:, D_MODEL + h * MEM_HD:D_MODEL + (h + 1) * MEM_HD]
    vb_ref[...] = kv[:, D_MODEL:].astype(BF16)


def _memory_kv(mem, g_mem, w_mkv, g_mk, *, batch, n_mem):
    out = pl.BlockSpec((1, n_mem, MEM_HEADS, MEM_HD), lambda i: (i, 0, 0, 0))
    flat = pl.BlockSpec((n_mem, D_MODEL), lambda i: (i, 0))
    return pl.pallas_call(
        _memkv_kernel,
        grid=(batch,),
        in_specs=[
            pl.BlockSpec((n_mem, D_MODEL), lambda i: (i, 0)),
            pl.BlockSpec((1, D_MODEL), lambda i: (0, 0)),
            pl.BlockSpec((D_MODEL, 2 * D_MODEL), lambda i: (0, 0)),
            pl.BlockSpec((1, MEM_HD), lambda i: (0, 0)),
        ],
        out_specs=[out, out, flat, flat],
        out_shape=[jax.ShapeDtypeStruct((batch, n_mem, MEM_HEADS, MEM_HD), F32)] * 2
        + [jax.ShapeDtypeStruct((batch * n_mem, D_MODEL), BF16)] * 2,
        compiler_params=_params("arbitrary"),
        name="memory_kv",
    )(mem, g_mem.reshape(1, -1), w_mkv.astype(BF16), g_mk.reshape(1, -1))


def _half_head_rms(acc, g_ref, scale):
    lane = lax.broadcasted_iota(jnp.int32, (1, LANE), 1)
    lo = lane < DF_HD
    outs = []
    for h in range(DF_HEADS):
        sl = slice(h * LANE, (h + 1) * LANE)
        xh = acc[:, sl]
        sq = xh * xh
        s_lo = jnp.sum(jnp.where(lo, sq, 0.0), axis=-1, keepdims=True)
        s_hi = jnp.sum(jnp.where(lo, 0.0, sq), axis=-1, keepdims=True)
        r = jnp.where(lo, lax.rsqrt(s_lo * (1.0 / DF_HD) + EPS), lax.rsqrt(s_hi * (1.0 / DF_HD) + EPS))
        outs.append(xh * r * (g_ref[:, sl] * scale))
    return outs


def _inproj_kernel(x_ref, gn_ref, w_ref, lbl_ref, gq_ref, gk_ref, gmq_ref,
                   p_ref, logf_ref, k_ref, v_ref, xn_scr, *, k_transposed):
    xn_scr[...] = _rms(x_ref[...], gn_ref[...]).astype(BF16)

    def proj(j):
        return _dot(xn_scr[...], w_ref[j])

    def put(j, val):
        p_ref[:, j * D_MODEL:(j + 1) * D_MODEL] = val.astype(BF16)

    put(CG_HQ, proj(CG_HQ))

    l = lbl_ref[...]
    e = jnp.exp(l - jnp.max(l, axis=0, keepdims=True))
    lb = e[0:1] / jnp.sum(e, axis=0, keepdims=True)
    sp = _sigmoid(proj(CG_HF))
    logf_ref[...] = jnp.log(lb + (1.0 - lb) * sp)
    put(CG_HF, (1.0 - lb) * (1.0 - sp))

    put(CG_HI, proj(CG_HI))

    for j in (CG_ZA, CG_ZB, CG_ZM):
        acc = proj(j)
        put(j, acc * _sigmoid(acc))

    outs = _half_head_rms(proj(CG_DQ), gq_ref, DF_HD ** -0.5 * LOG2E)
    for h in range(DF_HEADS):
        p_ref[:, CG_DQ * D_MODEL + h * LANE:CG_DQ * D_MODEL + (h + 1) * LANE] = outs[h].astype(BF16)

    outs = _half_head_rms(proj(CG_DK), gk_ref, 1.0)
    for h in range(DF_HEADS):
        sl = slice(h * LANE, (h + 1) * LANE)
        p_ref[:, CG_DK * D_MODEL + h * LANE:CG_DK * D_MODEL + (h + 1) * LANE] = outs[h].astype(BF16)
        if k_transposed:
            k_ref[0, sl, :] = outs[h].T
        else:
            k_ref[:, sl] = outs[h]

    acc = proj(CG_DV)
    put(CG_DV, acc)
    v_ref[...] = acc

    acc = proj(CG_MQ)
    for h in range(MEM_HEADS):
        sl = slice(h * MEM_HD, (h + 1) * MEM_HD)
        p_ref[:, CG_MQ * D_MODEL + h * MEM_HD:CG_MQ * D_MODEL + (h + 1) * MEM_HD] = (
            _rms(acc[:, sl], gmq_ref[...]) * (MEM_HD ** -0.5)).astype(BF16)

    for j in range(CG_G0, N_COLGROUPS):
        put(j, _sigmoid(proj(j)))


def _in_proj(x, g_norm, w_in, lb_logits, g_dq, g_dk, g_mq, *, seq, k_transposed):
    n = x.shape[0]
    tm = min(256, seq) if k_transposed else min(256, n)
    row = lambda i: (i, 0)
    const = lambda i: (0, 0)
    if k_transposed:
        per_b = seq // tm
        k_spec = pl.BlockSpec((1, D_MODEL, tm), lambda i: (i // per_b, 0, i % per_b))
        k_shape = jax.ShapeDtypeStruct((n // seq, D_MODEL, seq), F32)
    else:
        k_spec = pl.BlockSpec((tm, D_MODEL), row)
        k_shape = jax.ShapeDtypeStruct((n, D_MODEL), F32)
    return pl.pallas_call(
        functools.partial(_inproj_kernel, k_transposed=k_transposed),
        grid=(n // tm,),
        in_specs=[
            pl.BlockSpec((tm, D_MODEL), row),
            pl.BlockSpec((1, D_MODEL), const),
            pl.BlockSpec((N_COLGROUPS, D_MODEL, D_MODEL), lambda i: (0, 0, 0), pipeline_mode=pl.Buffered(1)),
            pl.BlockSpec((2, D_MODEL), const),
            pl.BlockSpec((1, D_MODEL), const),
            pl.BlockSpec((1, D_MODEL), const),
            pl.BlockSpec((1, MEM_HD), const),
        ],
        out_specs=[
            pl.BlockSpec((tm, N_COLGROUPS * D_MODEL), row),
            pl.BlockSpec((tm, D_MODEL), row),
            k_spec,
            pl.BlockSpec((tm, D_MODEL), row),
        ],
        out_shape=[
            jax.ShapeDtypeStruct((n, N_COLGROUPS * D_MODEL), BF16),
            jax.ShapeDtypeStruct((n, D_MODEL), F32),
            k_shape,
            jax.ShapeDtypeStruct((n, D_MODEL), F32),
        ],
        scratch_shapes=[pltpu.VMEM((tm, D_MODEL), BF16)],
        compiler_params=_params("arbitrary", vmem=VMEM_LIMIT_INPROJ),
        name="in_proj",
    )(x, g_norm.reshape(1, -1), w_in, lb_logits,
      jnp.tile(g_dq, 2 * DF_HEADS).reshape(1, -1), jnp.tile(g_dk, 2 * DF_HEADS).reshape(1, -1), g_mq.reshape(1, -1))


def _split_bf16(x):
    hi = x.astype(BF16)
    return hi, (x - hi.astype(F32)).astype(BF16)


def _hgrn_kernel(q_ref, k_ref, v_ref, z_ref, lf_ref, s0_ref, g_ref, y_ref, s_ref, st_scr, *, seq, tc, hp):
    nb = tc // GLA_BLOCK
    row = lax.broadcasted_iota(jnp.int32, (tc, tc), 0)
    col = lax.broadcasted_iota(jnp.int32, (tc, tc), 1)
    causal = col <= row
    tril = jnp.where(causal, 1.0, 0.0).astype(BF16)
    rowk = lax.broadcasted_iota(jnp.int32, (tc, LANE), 0)

    @pl.when(pl.program_id(1) == 0)
    def _():
        for h in range(hp):
            st_scr[h] = s0_ref[0, h].T

    n_chunks = seq // tc
    unroll = 2 if n_chunks % 2 == 0 else 1

    stacked_att = tc % LANE == 0
    tril2 = jnp.concatenate([tril, tril], axis=1)

    def chunks(c, carry):
        for u in range(unroll):
            rows = pl.ds(pl.multiple_of((c * unroll + u) * tc, tc), tc)
            g_hi, g_lo = _split_bf16(lf_ref[rows, :])
            b_all = _dot(tril2, jnp.concatenate([g_hi, g_lo], axis=0))
            for h in range(hp):
                sl = slice(h * LANE, (h + 1) * LANE)
                st = st_scr[h]
                b = b_all[:, sl]
                b_end = b[tc - 1:tc, :]
                q = q_ref[rows, sl].astype(F32)
                k = k_ref[rows, sl].astype(F32)
                v = v_ref[rows, sl]
                o = _dot_nt((q * jnp.exp(b)).astype(BF16), st.astype(BF16))
                k_dec = (k * jnp.exp(b_end - b)).astype(BF16)
                st_scr[h] = jnp.exp(b_end) * st + _dot_tn(v, k_dec)
                starts = [b[n * GLA_BLOCK - 1:n * GLA_BLOCK, :] if n else jnp.zeros((1, LANE), F32) for n in range(nb)]
                k_inv = [(k * jnp.exp(jnp.where(rowk < (n + 1) * GLA_BLOCK, starts[n] - b, 0.0))).astype(BF16)
                         for n in range(nb)]
                if stacked_att:
                    g_rows = jnp.concatenate([jnp.broadcast_to(g, (GLA_BLOCK, LANE)) for g in starts], axis=0)
                    q_dec = (q * jnp.exp(b - g_rows)).astype(BF16)
                    wide = _dot_nt(q_dec, jnp.concatenate(k_inv, axis=0))
                    att = [wide[n * GLA_BLOCK:(n + 1) * GLA_BLOCK, n * tc:(n + 1) * tc] for n in range(nb)]
                else:
                    att = [_dot_nt((q[n * GLA_BLOCK:(n + 1) * GLA_BLOCK]
                                    * jnp.exp(b[n * GLA_BLOCK:(n + 1) * GLA_BLOCK] - starts[n])).astype(BF16), k_inv[n])
                           for n in range(nb)]
                a = jnp.where(causal, jnp.concatenate(att, axis=0), 0.0).astype(BF16)
                o = o + _dot(a, v)
                y = _rms(o, g_ref[...]) * z_ref[rows, sl].astype(F32)
                y_ref[rows, sl] = y.astype(BF16)
        return carry

    lax.fori_loop(0, n_chunks // unroll, chunks, 0)

    @pl.when(pl.program_id(1) == pl.num_programs(1) - 1)
    def _():
        for h in range(hp):
            s_ref[0, h] = st_scr[h].T


def _hgrn(p, logf, s0, g_hg_out, batch, seq):
    hp = HG_HEADS
    ts = min(512, seq)
    tc = min(128, ts)
    per_b = seq // ts

    def col(cg):
        return pl.BlockSpec((ts, D_MODEL), lambda b, t: (b * per_b + t, cg))

    st = pl.BlockSpec((1, hp, HG_D, HG_D), lambda b, t: (b, 0, 0, 0))
    return pl.pallas_call(
        functools.partial(_hgrn_kernel, seq=ts, tc=tc, hp=hp),
        grid=(batch, per_b),
        in_specs=[col(CG_HQ), col(CG_HF), col(CG_HI), col(CG_ZA), col(0), st,
                  pl.BlockSpec((1, HG_D), lambda b, t: (0, 0))],
        out_specs=[col(0), st],
        out_shape=[jax.ShapeDtypeStruct((batch * seq, D_MODEL), BF16),
                   jax.ShapeDtypeStruct((batch, HG_HEADS, HG_D, HG_D), F32)],
        scratch_shapes=[pltpu.VMEM((hp, HG_D, HG_D), F32)],
        compiler_params=_params("arbitrary", "arbitrary"),
        name="hgrn",
    )(p, p, p, p, logf, s0, g_hg_out.reshape(1, -1))


def _lambda(lam_ref, lam_init):
    lv = lam_ref[...]
    return (jnp.exp(jnp.sum(lv[0:1] * lv[1:2], axis=-1, keepdims=True))
            - jnp.exp(jnp.sum(lv[2:3] * lv[3:4], axis=-1, keepdims=True)) + lam_init)


def _split_maps(q):
    first = lax.broadcasted_iota(jnp.int32, (1, LANE), 1) < DF_HD
    zero = jnp.zeros((), q.dtype)
    return jnp.concatenate([jnp.where(first, q, zero), jnp.where(first, zero, q)], axis=0)


def _diffattn_kernel(lam_ref, q_ref, k_ref, v_ref, z_ref, g_ref, y_ref, s_scr, p_scr, vt_scr,
                     *, n_q, tq, rc, lam_init):
    w = 2 * tq
    lam = _lambda(lam_ref, lam_init)
    vt_scr[0:DF_VD, :] = v_ref[...].astype(F32).T.astype(BF16)
    vt_scr[DF_VD:, :] = jnp.ones((ONES_ROWS, vt_scr.shape[1]), BF16)

    key = lax.broadcasted_iota(jnp.int32, (tq, w), 0)
    qry = lax.broadcasted_iota(jnp.int32, (tq, w), 1) & (tq - 1)
    bias = jnp.where((key // CHUNK) <= (qry // CHUNK), 0.0, -jnp.inf)

    def fold8(x, op):
        return op(x.reshape(rc // 8, 8, w), axis=0)

    for i in range(n_q):
        n_vis = (i + 1) * tq
        s_buf, p_buf = s_scr.at[i % 2], p_scr.at[i % 2]
        rows_q = slice(i * tq, (i + 1) * tq)
        s_buf[0:n_vis, :] = _dot_nt(k_ref[0:n_vis, :], _split_maps(q_ref[rows_q, :]))
        s_buf[n_vis - tq:n_vis, :] = s_buf[n_vis - tq:n_vis, :] + bias
        chunks = [slice(c * rc, (c + 1) * rc) for c in range(n_vis // rc)]

        m8 = fold8(s_buf[chunks[0], :], jnp.max)
        for rows in chunks[1:]:
            m8 = jnp.maximum(m8, fold8(s_buf[rows, :], jnp.max))
        m = jnp.max(m8, axis=0, keepdims=True)

        for rows in chunks:
            p_buf[rows, :] = jnp.exp2(s_buf[rows, :] - m).astype(BF16)
        pv = _dot(vt_scr[:, 0:n_vis], p_buf[0:n_vis, :])
        rinv = 1.0 / pv[DF_VD:DF_VD + 1, :]
        o = (pv[:DF_VD, :tq] * rinv[:, :tq] - pv[:DF_VD, tq:] * (rinv[:, tq:] * lam)).T
        y = _rms(o, g_ref[...]) * (1.0 - lam_init) * z_ref[rows_q, :].astype(F32)
        y_ref[rows_q, :] = y.astype(BF16)


def _diff_attn(lam_vecs, p, g_dsub, *, batch, seq, lam_init):
    tq = rc = 256

    def col(cg):
        return pl.BlockSpec((seq, LANE), lambda b, h: (b, cg * DF_HEADS + h))

    return pl.pallas_call(
        functools.partial(_diffattn_kernel, n_q=seq // tq, tq=tq, rc=rc, lam_init=lam_init),
        grid=(batch, DF_HEADS),
        in_specs=[pl.BlockSpec((4, DF_HD), lambda b, h: (0, 0)),
                  col(CG_DQ), col(CG_DK), col(CG_DV), col(CG_ZB),
                  pl.BlockSpec((1, DF_VD), lambda b, h: (0, 0))],
        out_specs=pl.BlockSpec((seq, LANE), lambda b, h: (b, h)),
        out_shape=jax.ShapeDtypeStruct((batch * seq, D_MODEL), BF16),
        scratch_shapes=[pltpu.VMEM((2, seq, 2 * tq), F32), pltpu.VMEM((2, seq, 2 * tq), BF16),
                        pltpu.VMEM((DF_VD + ONES_ROWS, seq), BF16)],
        compiler_params=_params("arbitrary", "arbitrary"),
        name="diff_attn",
    )(lam_vecs, p, p, p, p, g_dsub.reshape(1, -1))


def _cached_diffattn_kernel(lam_ref, q_ref, kn_ref, vn_ref, z_ref, g_ref, kt_ref, vc_ref, y_ref, *, tq, lam_init):
    lam = _lambda(lam_ref, lam_init)
    for h in range(DF_HEADS):
        sl = slice(h * LANE, (h + 1) * LANE)
        qs = _split_maps(q_ref[:, sl])
        s_c = _dot(qs, kt_ref[0, sl, :].astype(BF16))
        s_n = _dot_nt(qs, kn_ref[:, sl])
        m = jnp.maximum(jnp.max(s_c, axis=-1, keepdims=True), jnp.max(s_n, axis=-1, keepdims=True))
        p_c = jnp.exp2(s_c - m)
        p_n = jnp.exp2(s_n - m)
        rinv = 1.0 / (jnp.sum(p_c, axis=-1, keepdims=True) + jnp.sum(p_n, axis=-1, keepdims=True))
        r1 = rinv[:tq]
        r2 = rinv[tq:] * lam
        a_c = (p_c[:tq] * r1 - p_c[tq:] * r2).astype(BF16)
        a_n = (p_n[:tq] * r1 - p_n[tq:] * r2).astype(BF16)
        o = _dot(a_c, vc_ref[0, :, h, :].astype(BF16)) + _dot(a_n, vn_ref[:, sl])
        y = _rms(o, g_ref[...]) * (1.0 - lam_init) * z_ref[:, sl].astype(F32)
        y_ref[:, sl] = y.astype(BF16)


def _cached_diff_attn(lam_vecs, p, g_dsub, kt_cache, v_cache, *, batch, seq, lam_init):
    past = v_cache.shape[1]

    def col(cg):
        return pl.BlockSpec((seq, D_MODEL), lambda b: (b, cg))

    return pl.pallas_call(
        functools.partial(_cached_diffattn_kernel, tq=seq, lam_init=lam_init),
        grid=(batch,),
        in_specs=[pl.BlockSpec((4, DF_HD), lambda b: (0, 0)),
                  col(CG_DQ), col(CG_DK), col(CG_DV), col(CG_ZB),
                  pl.BlockSpec((1, DF_VD), lambda b: (0, 0)),
                  pl.BlockSpec((1, D_MODEL, past), lambda b: (b, 0, 0)),
                  pl.BlockSpec((1, past, DF_HEADS, DF_VD), lambda b: (b, 0, 0, 0))],
        out_specs=pl.BlockSpec((seq, D_MODEL), lambda b: (b, 0)),
        out_shape=jax.ShapeDtypeStruct((batch * seq, D_MODEL), BF16),
        compiler_params=_params("arbitrary"),
        name="cached_diff_attn",
    )(lam_vecs, p, p, p, p, g_dsub.reshape(1, -1), kt_cache, v_cache)


def _merge_kernel(x_ref, ya_ref, yb_ref, qm_ref, zm_ref, g0_ref, g1_ref, g2_ref, mk_ref, mv_ref,
                  wb_ref, wo_ref, y_ref):
    ym = []
    for h in range(MEM_HEADS):
        sl = slice(h * MEM_HD, (h + 1) * MEM_HD)
        s = _dot_nt(qm_ref[:, sl], mk_ref[:, sl])
        p = jnp.exp(s - jnp.max(s, axis=-1, keepdims=True))
        rinv = 1.0 / jnp.sum(p, axis=-1, keepdims=True)
        o = _dot(p.astype(BF16), mv_ref[:, sl]) * rinv
        ym.append((o * zm_ref[:, sl].astype(F32)).astype(BF16))
    ym = jnp.concatenate(ym, axis=-1)
    hsum = (g0_ref[...].astype(F32) * _dot(ya_ref[...], wb_ref[0])
            + g1_ref[...].astype(F32) * _dot(yb_ref[...], wb_ref[1])
            + g2_ref[...].astype(F32) * _dot(ym, wb_ref[2]))
    y_ref[...] = x_ref[...] + _dot(hsum.astype(BF16), wo_ref[...])


def _merge(x, ya, yb, p, mem_k, mem_v, w_branch, w_out, *, seq, n_mem):
    n = x.shape[0]
    tm = min(256, seq)
    per_b = seq // tm
    row = lambda i: (i, 0)

    def col(cg):
        return pl.BlockSpec((tm, D_MODEL), lambda i: (i, cg))

    mem = pl.BlockSpec((n_mem, D_MODEL), lambda i: (i // per_b, 0))
    return pl.pallas_call(
        _merge_kernel,
        grid=(n // tm,),
        in_specs=[pl.BlockSpec((tm, D_MODEL), row), pl.BlockSpec((tm, D_MODEL), row), pl.BlockSpec((tm, D_MODEL), row),
                  col(CG_MQ), col(CG_ZM), col(CG_G0), col(CG_G0 + 1), col(CG_G0 + 2), mem, mem,
                  pl.BlockSpec((N_BRANCH, D_MODEL, D_MODEL), lambda i: (0, 0, 0), pipeline_mode=pl.Buffered(1)),
                  pl.BlockSpec((D_MODEL, D_MODEL), lambda i: (0, 0), pipeline_mode=pl.Buffered(1))],
        out_specs=pl.BlockSpec((tm, D_MODEL), row),
        out_shape=jax.ShapeDtypeStruct((n, D_MODEL), F32),
        compiler_params=_params("arbitrary"),
        name="merge",
    )(x, ya, yb, p, p, p, p, p, mem_k, mem_v, w_branch, w_out)


def _layer(x, s0, past_k, past_v, mem_k, mem_v, lam_init, w, n_mem):
    batch, seq, _ = x.shape
    n = batch * seq
    xf = x.reshape(n, D_MODEL)
    prompt = past_k is None
    p, logf, k32, v32 = _in_proj(xf, w["g_norm"], w["w_in"], w["lb_logits"], w["g_dq"], w["g_dk"], w["g_mq"],
                                 seq=seq, k_transposed=prompt)
    ya, s_new = _hgrn(p, logf, s0, w["g_hg_out"], batch, seq)
    if prompt:
        yb = _diff_attn(w["lam_vecs"], p, w["g_dsub"], batch=batch, seq=seq, lam_init=lam_init)
    else:
        past = past_k.shape[1]
        kt_cache = jnp.transpose(past_k, (0, 2, 3, 4, 1)).reshape(batch, D_MODEL, past)
        yb = _cached_diff_attn(w["lam_vecs"], p, w["g_dsub"], kt_cache, past_v, batch=batch, seq=seq,
                               lam_init=lam_init)
    y = _merge(xf, ya, yb, p, mem_k, mem_v, w["w_branch"], w["w_out"], seq=seq, n_mem=n_mem)
    if prompt:
        k_out = jnp.transpose(k32.reshape(batch, DF_HEADS, 2, DF_HD, seq), (0, 4, 1, 2, 3))
    else:
        k_out = k32.reshape(batch, seq, DF_HEADS, 2, DF_HD)
    return y.reshape(batch, seq, D_MODEL), s_new, k_out, v32.reshape(batch, seq, DF_HEADS, DF_VD)


def kernel(x_prompt, x_sample, mem_prompt, cache_diff_k, cache_diff_v, cache_mem_k, cache_mem_v, state_hgrn,
           g_norm, w_in, hg_lb_logits, g_hg_out, g_dq, g_dk, lam_q1, lam_k1, lam_q2, lam_k2, g_dsub, g_mem,
           w_mkv, g_mq, g_mk, w_branch, w_out):
    depth = g_norm.shape[0]
    assert depth == 1 and hg_lb_logits.shape[0] == 2, "single-layer step only"
    batch, n_mem = mem_prompt.shape[:2]
    lam_init = 0.8 - 0.6
    w_in_groups = w_in[0].astype(BF16).reshape(D_MODEL, N_COLGROUPS, D_MODEL).transpose(1, 0, 2)
    w = dict(g_norm=g_norm[0], w_in=w_in_groups, lb_logits=hg_lb_logits, g_hg_out=g_hg_out[0],
             g_dq=g_dq[0], g_dk=g_dk[0], g_mq=g_mq[0], g_dsub=g_dsub[0],
             lam_vecs=jnp.stack([lam_q1[0], lam_k1[0], lam_q2[0], lam_k2[0]]),
             w_branch=w_branch[0].astype(BF16), w_out=w_out[0].astype(BF16))

    mem_k, mem_v, mem_kb, mem_vb = _memory_kv(mem_prompt.reshape(batch * n_mem, D_MODEL), g_mem[0], w_mkv[0], g_mk[0],
                                              batch=batch, n_mem=n_mem)
    s0 = jnp.zeros((batch, HG_HEADS, HG_D, HG_D), F32)
    y_p, s_p, k_p, v_p = _layer(x_prompt, s0, None, None, mem_kb, mem_vb, lam_init, w, n_mem)
    y_s, s_s, k_s, v_s = _layer(x_sample, state_hgrn[0], cache_diff_k[0], cache_diff_v[0],
                                cache_mem_k[0].reshape(-1, D_MODEL).astype(BF16),
                                cache_mem_v[0].reshape(-1, D_MODEL).astype(BF16), lam_init, w, n_mem)
    return (y_p, y_s, s_p[None], s_s[None], k_p[None], v_p[None], k_s[None], v_s[None], mem_k[None], mem_v[None])
```

```python
import functools

import jax
import jax.numpy as jnp
from jax import lax
from jax.experimental import pallas as pl
from jax.experimental.pallas import tpu as pltpu

F32 = jnp.float32
BF16 = jnp.bfloat16

EPS = 1e-6
LOG2E = 1.4426950408889634
D_MODEL = 1024
CHUNK = 64
GLA_BLOCK = 16
HG_HEADS, HG_D = 8, 128
DF_HEADS, DF_HD, DF_VD = 8, 64, 128
MEM_HEADS, MEM_HD = 4, 256
N_BRANCH = 3
N_COLGROUPS = 13
(CG_HQ, CG_HF, CG_HI, CG_ZA, CG_DQ, CG_DK, CG_DV, CG_ZB, CG_MQ, CG_ZM, CG_G0) = range(11)

LANE = 128
ONES_ROWS = 16
VMEM_LIMIT = 52 * 1024 * 1024
VMEM_LIMIT_INPROJ = 60 * 1024 * 1024

NT = (((1,), (1,)), ((), ()))
TN = (((0,), (0,)), ((), ()))


def _dot(a, b):
    return jnp.dot(a, b, preferred_element_type=F32)


def _dot_nt(a, b):
    return lax.dot_general(a, b, NT, preferred_element_type=F32)


def _dot_tn(a, b):
    return lax.dot_general(a, b, TN, preferred_element_type=F32)


def _sigmoid(x):
    return 0.5 * jnp.tanh(0.5 * x) + 0.5


def _rms(x, g):
    return x * lax.rsqrt(jnp.mean(x * x, axis=-1, keepdims=True) + EPS) * g


def _params(*sem, vmem=VMEM_LIMIT):
    return pltpu.CompilerParams(dimension_semantics=sem, vmem_limit_bytes=vmem)


def _memkv_kernel(mem_ref, g_ref, w_ref, gk_ref, k_ref, v_ref, kb_ref, vb_ref):
    xn = _rms(mem_ref[...], g_ref[...]).astype(BF16)
    kv = _dot(xn, w_ref[...])
    for h in range(MEM_HEADS):
        sl = slice(h * MEM_HD, (h + 1) * MEM_HD)
        k = _rms(kv[:, sl], gk_ref[...])
        k_ref[0, :, h, :] = k
        kb_ref[:, sl] = k.astype(BF16)
        v_ref[0, :, h, :] = kv[:, D_MODEL + h * MEM_HD:D_MODEL + (h + 1) * MEM_HD]
    vb_ref[...] = kv[:, D_MODEL:].astype(BF16)


def _memory_kv(mem, g_mem, w_mkv, g_mk, *, batch, n_mem):
    out = pl.BlockSpec((1, n_mem, MEM_HEADS, MEM_HD), lambda i: (i, 0, 0, 0))
    flat = pl.BlockSpec((n_mem, D_MODEL), lambda i: (i, 0))
    return pl.pallas_call(
        _memkv_kernel,
        grid=(batch,),
        in_specs=[
            pl.BlockSpec((n_mem, D_MODEL), lambda i: (i, 0)),
            pl.BlockSpec((1, D_MODEL), lambda i: (0, 0)),
            pl.BlockSpec((D_MODEL, 2 * D_MODEL), lambda i: (0, 0)),
            pl.BlockSpec((1, MEM_HD), lambda i: (0, 0)),
        ],
        out_specs=[out, out, flat, flat],
        out_shape=[jax.ShapeDtypeStruct((batch, n_mem, MEM_HEADS, MEM_HD), F32)] * 2
        + [jax.ShapeDtypeStruct((batch * n_mem, D_MODEL), BF16)] * 2,
        compiler_params=_params("arbitrary"),
        name="memory_kv",
    )(mem, g_mem.reshape(1, -1), w_mkv.astype(BF16), g_mk.reshape(1, -1))


def _half_head_rms(acc, g_ref, scale):
    lane = lax.broadcasted_iota(jnp.int32, (1, LANE), 1)
    lo = lane < DF_HD
    outs = []
    for h in range(DF_HEADS):
        sl = slice(h * LANE, (h + 1) * LANE)
        xh = acc[:, sl]
        sq = xh * xh
        s_lo = jnp.sum(jnp.where(lo, sq, 0.0), axis=-1, keepdims=True)
        s_hi = jnp.sum(jnp.where(lo, 0.0, sq), axis=-1, keepdims=True)
        r = jnp.where(lo, lax.rsqrt(s_lo * (1.0 / DF_HD) + EPS), lax.rsqrt(s_hi * (1.0 / DF_HD) + EPS))
        outs.append(xh * r * (g_ref[:, sl] * scale))
    return outs


def _inproj_kernel(x_ref, gn_ref, w_ref, lbl_ref, gq_ref, gk_ref, gmq_ref,
                   p_ref, logf_ref, k_ref, v_ref, xn_scr, *, k_transposed):
    xn_scr[...] = _rms(x_ref[...], gn_ref[...]).astype(BF16)

    def proj(j):
        return _dot(xn_scr[...], w_ref[j])

    def put(j, val):
        p_ref[:, j * D_MODEL:(j + 1) * D_MODEL] = val.astype(BF16)

    put(CG_HQ, proj(CG_HQ))

    l = lbl_ref[...]
    e = jnp.exp(l - jnp.max(l, axis=0, keepdims=True))
    lb = e[0:1] / jnp.sum(e, axis=0, keepdims=True)
    sp = _sigmoid(proj(CG_HF))
    logf_ref[...] = jnp.log(lb + (1.0 - lb) * sp)
    put(CG_HF, (1.0 - lb) * (1.0 - sp))

    put(CG_HI, proj(CG_HI))

    for j in (CG_ZA, CG_ZB, CG_ZM):
        acc = proj(j)
        put(j, acc * _sigmoid(acc))

    outs = _half_head_rms(proj(CG_DQ), gq_ref, DF_HD ** -0.5 * LOG2E)
    for h in range(DF_HEADS):
        p_ref[:, CG_DQ * D_MODEL + h * LANE:CG_DQ * D_MODEL + (h + 1) * LANE] = outs[h].astype(BF16)

    outs = _half_head_rms(proj(CG_DK), gk_ref, 1.0)
    for h in range(DF_HEADS):
        sl = slice(h * LANE, (h + 1) * LANE)
        p_ref[:, CG_DK * D_MODEL + h * LANE:CG_DK * D_MODEL + (h + 1) * LANE] = outs[h].astype(BF16)
        if k_transposed:
            k_ref[0, sl, :] = outs[h].T
        else:
            k_ref[:, sl] = outs[h]

    acc = proj(CG_DV)
    put(CG_DV, acc)
    v_ref[...] = acc

    acc = proj(CG_MQ)
    for h in range(MEM_HEADS):
        sl = slice(h * MEM_HD, (h + 1) * MEM_HD)
        p_ref[:, CG_MQ * D_MODEL + h * MEM_HD:CG_MQ * D_MODEL + (h + 1) * MEM_HD] = (
            _rms(acc[:, sl], gmq_ref[...]) * (MEM_HD ** -0.5)).astype(BF16)

    for j in range(CG_G0, N_COLGROUPS):
        put(j, _sigmoid(proj(j)))


def _in_proj(x, g_norm, w_in, lb_logits, g_dq, g_dk, g_mq, *, seq, k_transposed):
    n = x.shape[0]
    tm = min(256, seq) if k_transposed else min(256, n)
    row = lambda i: (i, 0)
    const = lambda i: (0, 0)
    if k_transposed:
        per_b = seq // tm
        k_spec = pl.BlockSpec((1, D_MODEL, tm), lambda i: (i // per_b, 0, i % per_b))
        k_shape = jax.ShapeDtypeStruct((n // seq, D_MODEL, seq), F32)
    else:
        k_spec = pl.BlockSpec((tm, D_MODEL), row)
        k_shape = jax.ShapeDtypeStruct((n, D_MODEL), F32)
    return pl.pallas_call(
        functools.partial(_inproj_kernel, k_transposed=k_transposed),
        grid=(n // tm,),
        in_specs=[
            pl.BlockSpec((tm, D_MODEL), row),
            pl.BlockSpec((1, D_MODEL), const),
            pl.BlockSpec((N_COLGROUPS, D_MODEL, D_MODEL), lambda i: (0, 0, 0), pipeline_mode=pl.Buffered(1)),
            pl.BlockSpec((2, D_MODEL), const),
            pl.BlockSpec((1, D_MODEL), const),
            pl.BlockSpec((1, D_MODEL), const),
            pl.BlockSpec((1, MEM_HD), const),
        ],
        out_specs=[
            pl.BlockSpec((tm, N_COLGROUPS * D_MODEL), row),
            pl.BlockSpec((tm, D_MODEL), row),
            k_spec,
            pl.BlockSpec((tm, D_MODEL), row),
        ],
        out_shape=[
            jax.ShapeDtypeStruct((n, N_COLGROUPS * D_MODEL), BF16),
            jax.ShapeDtypeStruct((n, D_MODEL), F32),
            k_shape,
            jax.ShapeDtypeStruct((n, D_MODEL), F32),
        ],
        scratch_shapes=[pltpu.VMEM((tm, D_MODEL), BF16)],
        compiler_params=_params("arbitrary", vmem=VMEM_LIMIT_INPROJ),
        name="in_proj",
    )(x, g_norm.reshape(1, -1), w_in, lb_logits,
      jnp.tile(g_dq, 2 * DF_HEADS).reshape(1, -1), jnp.tile(g_dk, 2 * DF_HEADS).reshape(1, -1), g_mq.reshape(1, -1))


def _split_bf16(x):
    hi = x.astype(BF16)
    return hi, (x - hi.astype(F32)).astype(BF16)


def _hgrn_kernel(q_ref, k_ref, v_ref, z_ref, lf_ref, s0_ref, g_ref, y_ref, s_ref, st_scr, *, seq, tc, hp):
    nb = tc // GLA_BLOCK
    row = lax.broadcasted_iota(jnp.int32, (tc, tc), 0)
    col = lax.broadcasted_iota(jnp.int32, (tc, tc), 1)
    causal = col <= row
    tril = jnp.where(causal, 1.0, 0.0).astype(BF16)
    rowk = lax.broadcasted_iota(jnp.int32, (tc, LANE), 0)

    @pl.when(pl.program_id(1) == 0)
    def _():
        for h in range(hp):
            st_scr[h] = s0_ref[0, h].T

    n_chunks = seq // tc
    unroll = 2 if n_chunks % 2 == 0 else 1

    stacked_att = tc % LANE == 0
    tril2 = jnp.concatenate([tril, tril], axis=1)

    def chunks(c, carry):
        heads = [slice(h * LANE, (h + 1) * LANE) for h in range(hp)]
        for u in range(unroll):
            rows = pl.ds(pl.multiple_of((c * unroll + u) * tc, tc), tc)
            g_hi, g_lo = _split_bf16(lf_ref[rows, :])
            b_all = _dot(tril2, jnp.concatenate([g_hi, g_lo], axis=0))
            bs = [b_all[:, sl] for sl in heads]
            qs = [q_ref[rows, sl].astype(F32) for sl in heads]
            ks = [k_ref[rows, sl].astype(F32) for sl in heads]
            vs = [v_ref[rows, sl] for sl in heads]

            outs = []
            for h in range(hp):
                b, st = bs[h], st_scr[h]
                b_end = b[tc - 1:tc, :]
                outs.append(_dot_nt((qs[h] * jnp.exp(b)).astype(BF16), st.astype(BF16)))
                k_dec = (ks[h] * jnp.exp(b_end - b)).astype(BF16)
                st_scr[h] = jnp.exp(b_end) * st + _dot_tn(vs[h], k_dec)

            atts = []
            for h in range(hp):
                b, q, k = bs[h], qs[h], ks[h]
                starts = [b[n * GLA_BLOCK - 1:n * GLA_BLOCK, :] if n else jnp.zeros((1, LANE), F32) for n in range(nb)]
                k_inv = [(k * jnp.exp(jnp.where(rowk < (n + 1) * GLA_BLOCK, starts[n] - b, 0.0))).astype(BF16)
                         for n in range(nb)]
                if stacked_att:
                    g_rows = jnp.concatenate([jnp.broadcast_to(g, (GLA_BLOCK, LANE)) for g in starts], axis=0)
                    q_dec = (q * jnp.exp(b - g_rows)).astype(BF16)
                    wide = _dot_nt(q_dec, jnp.concatenate(k_inv, axis=0))
                    att = [wide[n * GLA_BLOCK:(n + 1) * GLA_BLOCK, n * tc:(n + 1) * tc] for n in range(nb)]
                else:
                    att = [_dot_nt((q[n * GLA_BLOCK:(n + 1) * GLA_BLOCK]
                                    * jnp.exp(b[n * GLA_BLOCK:(n + 1) * GLA_BLOCK] - starts[n])).astype(BF16), k_inv[n])
                           for n in range(nb)]
                atts.append(jnp.where(causal, jnp.concatenate(att, axis=0), 0.0).astype(BF16))

            for h in range(hp):
                o = outs[h] + _dot(atts[h], vs[h])
                y = _rms(o, g_ref[...]) * z_ref[rows, heads[h]].astype(F32)
                y_ref[rows, heads[h]] = y.astype(BF16)
        return carry

    lax.fori_loop(0, n_chunks // unroll, chunks, 0)

    @pl.when(pl.program_id(1) == pl.num_programs(1) - 1)
    def _():
        for h in range(hp):
            s_ref[0, h] = st_scr[h].T


def _hgrn(p, logf, s0, g_hg_out, batch, seq):
    hp = HG_HEADS
    ts = min(512, seq)
    tc = min(128, ts)
    per_b = seq // ts

    def col(cg):
        return pl.BlockSpec((ts, D_MODEL), lambda b, t: (b * per_b + t, cg))

    st = pl.BlockSpec((1, hp, HG_D, HG_D), lambda b, t: (b, 0, 0, 0))
    return pl.pallas_call(
        functools.partial(_hgrn_kernel, seq=ts, tc=tc, hp=hp),
        grid=(batch, per_b),
        in_specs=[col(CG_HQ), col(CG_HF), col(CG_HI), col(CG_ZA), col(0), st,
                  pl.BlockSpec((1, HG_D), lambda b, t: (0, 0))],
        out_specs=[col(0), st],
        out_shape=[jax.ShapeDtypeStruct((batch * seq, D_MODEL), BF16),
                   jax.ShapeDtypeStruct((batch, HG_HEADS, HG_D, HG_D), F32)],
        scratch_shapes=[pltpu.VMEM((hp, HG_D, HG_D), F32)],
        compiler_params=_params("arbitrary", "arbitrary"),
        name="hgrn",
    )(p, p, p, p, logf, s0, g_hg_out.reshape(1, -1))


def _lambda(lam_ref, lam_init):
    lv = lam_ref[...]
    return (jnp.exp(jnp.sum(lv[0:1] * lv[1:2], axis=-1, keepdims=True))
            - jnp.exp(jnp.sum(lv[2:3] * lv[3:4], axis=-1, keepdims=True)) + lam_init)


def _split_maps(q):
    first = lax.broadcasted_iota(jnp.int32, (1, LANE), 1) < DF_HD
    zero = jnp.zeros((), q.dtype)
    return jnp.concatenate([jnp.where(first, q, zero), jnp.where(first, zero, q)], axis=0)


def _diffattn_kernel(lam_ref, q_ref, k_ref, v_ref, z_ref, g_ref, y_ref, s_scr, p_scr, vt_scr,
                     *, n_q, tq, rc, lam_init):
    w = 2 * tq
    lam = _lambda(lam_ref, lam_init)
    vt_scr[0:DF_VD, :] = v_ref[...].astype(F32).T.astype(BF16)
    vt_scr[DF_VD:, :] = jnp.ones((ONES_ROWS, vt_scr.shape[1]), BF16)

    key = lax.broadcasted_iota(jnp.int32, (tq, w), 0)
    qry = lax.broadcasted_iota(jnp.int32, (tq, w), 1) & (tq - 1)
    bias = jnp.where((key // CHUNK) <= (qry // CHUNK), 0.0, -jnp.inf)

    def fold8(x, op):
        return op(x.reshape(rc // 8, 8, w), axis=0)

    def scores(i):
        n_vis = (i + 1) * tq
        s_buf = s_scr.at[i % 2]
        s_buf[0:n_vis, :] = _dot_nt(k_ref[0:n_vis, :], _split_maps(q_ref[i * tq:(i + 1) * tq, :]))
        s_buf[n_vis - tq:n_vis, :] = s_buf[n_vis - tq:n_vis, :] + bias

    scores(0)
    for i in range(n_q):
        n_vis = (i + 1) * tq
        s_buf, p_buf = s_scr.at[i % 2], p_scr.at[i % 2]
        rows_q = slice(i * tq, (i + 1) * tq)
        chunks = [slice(c * rc, (c + 1) * rc) for c in range(n_vis // rc)]
        if i + 1 < n_q:
            scores(i + 1)

        m8 = fold8(s_buf[chunks[0], :], jnp.max)
        for rows in chunks[1:]:
            m8 = jnp.maximum(m8, fold8(s_buf[rows, :], jnp.max))
        m = jnp.max(m8, axis=0, keepdims=True)

        for rows in chunks:
            p_buf[rows, :] = jnp.exp2(s_buf[rows, :] - m).astype(BF16)
        pv = _dot(vt_scr[:, 0:n_vis], p_buf[0:n_vis, :])
        rinv = 1.0 / pv[DF_VD:DF_VD + 1, :]
        o = (pv[:DF_VD, :tq] * rinv[:, :tq] - pv[:DF_VD, tq:] * (rinv[:, tq:] * lam)).T
        y = _rms(o, g_ref[...]) * (1.0 - lam_init) * z_ref[rows_q, :].astype(F32)
        y_ref[rows_q, :] = y.astype(BF16)


def _diff_attn(lam_vecs, p, g_dsub, *, batch, seq, lam_init):
    tq = rc = 256

    def col(cg):
        return pl.BlockSpec((seq, LANE), lambda b, h: (b, cg * DF_HEADS + h))

    return pl.pallas_call(
        functools.partial(_diffattn_kernel, n_q=seq // tq, tq=tq, rc=rc, lam_init=lam_init),
        grid=(batch, DF_HEADS),
        in_specs=[pl.BlockSpec((4, DF_HD), lambda b, h: (0, 0)),
                  col(CG_DQ), col(CG_DK), col(CG_DV), col(CG_ZB),
                  pl.BlockSpec((1, DF_VD), lambda b, h: (0, 0))],
        out_specs=pl.BlockSpec((seq, LANE), lambda b, h: (b, h)),
        out_shape=jax.ShapeDtypeStruct((batch * seq, D_MODEL), BF16),
        scratch_shapes=[pltpu.VMEM((2, seq, 2 * tq), F32), pltpu.VMEM((2, seq, 2 * tq), BF16),
                        pltpu.VMEM((DF_VD + ONES_ROWS, seq), BF16)],
        compiler_params=_params("arbitrary", "arbitrary"),
        name="diff_attn",
    )(lam_vecs, p, p, p, p, g_dsub.reshape(1, -1))


def _cached_diffattn_kernel(lam_ref, q_ref, kn_ref, vn_ref, z_ref, g_ref, kt_ref, vc_ref, y_ref, *, tq, lam_init):
    lam = _lambda(lam_ref, lam_init)
    heads = [slice(h * LANE, (h + 1) * LANE) for h in range(DF_HEADS)]
    scores = []
    for sl in heads:
        qs = _split_maps(q_ref[:, sl])
        scores.append((_dot(qs, kt_ref[0, sl, :].astype(BF16)),
                       _dot_nt(qs, kn_ref[:, sl])))
    for h, sl in enumerate(heads):
        s_c, s_n = scores[h]
        m = jnp.maximum(jnp.max(s_c, axis=-1, keepdims=True), jnp.max(s_n, axis=-1, keepdims=True))
        p_c = jnp.exp2(s_c - m)
        p_n = jnp.exp2(s_n - m)
        rinv = 1.0 / (jnp.sum(p_c, axis=-1, keepdims=True) + jnp.sum(p_n, axis=-1, keepdims=True))
        r1 = rinv[:tq]
        r2 = rinv[tq:] * lam
        a_c = (p_c[:tq] * r1 - p_c[tq:] * r2).astype(BF16)
        a_n = (p_n[:tq] * r1 - p_n[tq:] * r2).astype(BF16)
        o = _dot(a_c, vc_ref[0, :, h, :].astype(BF16)) + _dot(a_n, vn_ref[:, sl])
        y = _rms(o, g_ref[...]) * (1.0 - lam_init) * z_ref[:, sl].astype(F32)
        y_ref[:, sl] = y.astype(BF16)


def _cached_diff_attn(lam_vecs, p, g_dsub, kt_cache, v_cache, *, batch, seq, lam_init):
    past = v_cache.shape[1]

    def col(cg):
        return pl.BlockSpec((seq, D_MODEL), lambda b: (b, cg))

    return pl.pallas_call(
        functools.partial(_cached_diffattn_kernel, tq=seq, lam_init=lam_init),
        grid=(batch,),
        in_specs=[pl.BlockSpec((4, DF_HD), lambda b: (0, 0)),
                  col(CG_DQ), col(CG_DK), col(CG_DV), col(CG_ZB),
                  pl.BlockSpec((1, DF_VD), lambda b: (0, 0)),
                  pl.BlockSpec((1, D_MODEL, past), lambda b: (b, 0, 0)),
                  pl.BlockSpec((1, past, DF_HEADS, DF_VD), lambda b: (b, 0, 0, 0))],
        out_specs=pl.BlockSpec((seq, D_MODEL), lambda b: (b, 0)),
        out_shape=jax.ShapeDtypeStruct((batch * seq, D_MODEL), BF16),
        compiler_params=_params("arbitrary"),
        name="cached_diff_attn",
    )(lam_vecs, p, p, p, p, g_dsub.reshape(1, -1), kt_cache, v_cache)


def _merge_kernel(x_ref, ya_ref, yb_ref, qm_ref, zm_ref, g0_ref, g1_ref, g2_ref, mk_ref, mv_ref,
                  wb_ref, wo_ref, y_ref):
    heads = [slice(h * MEM_HD, (h + 1) * MEM_HD) for h in range(MEM_HEADS)]
    scores = [_dot_nt(qm_ref[:, sl], mk_ref[:, sl]) for sl in heads]
    hsum = (g0_ref[...].astype(F32) * _dot(ya_ref[...], wb_ref[0])
            + g1_ref[...].astype(F32) * _dot(yb_ref[...], wb_ref[1]))
    ym = []
    for s, sl in zip(scores, heads):
        p = jnp.exp(s - jnp.max(s, axis=-1, keepdims=True))
        rinv = 1.0 / jnp.sum(p, axis=-1, keepdims=True)
        o = _dot(p.astype(BF16), mv_ref[:, sl]) * rinv
        ym.append((o * zm_ref[:, sl].astype(F32)).astype(BF16))
    hsum = hsum + g2_ref[...].astype(F32) * _dot(jnp.concatenate(ym, axis=-1), wb_ref[2])
    y_ref[...] = x_ref[...] + _dot(hsum.astype(BF16), wo_ref[...])


def _merge(x, ya, yb, p, mem_k, mem_v, w_branch, w_out, *, seq, n_mem):
    n = x.shape[0]
    tm = min(256, seq)
    per_b = seq // tm
    row = lambda i: (i, 0)

    def col(cg):
        return pl.BlockSpec((tm, D_MODEL), lambda i: (i, cg))

    mem = pl.BlockSpec((n_mem, D_MODEL), lambda i: (i // per_b, 0))
    return pl.pallas_call(
        _merge_kernel,
        grid=(n // tm,),
        in_specs=[pl.BlockSpec((tm, D_MODEL), row), pl.BlockSpec((tm, D_MODEL), row), pl.BlockSpec((tm, D_MODEL), row),
                  col(CG_MQ), col(CG_ZM), col(CG_G0), col(CG_G0 + 1), col(CG_G0 + 2), mem, mem,
                  pl.BlockSpec((N_BRANCH, D_MODEL, D_MODEL), lambda i: (0, 0, 0), pipeline_mode=pl.Buffered(1)),
                  pl.BlockSpec((D_MODEL, D_MODEL), lambda i: (0, 0), pipeline_mode=pl.Buffered(1))],
        out_specs=pl.BlockSpec((tm, D_MODEL), row),
        out_shape=jax.ShapeDtypeStruct((n, D_MODEL), F32),
        compiler_params=_params("arbitrary"),
        name="merge",
    )(x, ya, yb, p, p, p, p, p, mem_k, mem_v, w_branch, w_out)


def _layer(x, s0, past_k, past_v, mem_k, mem_v, lam_init, w, n_mem):
    batch, seq, _ = x.shape
    n = batch * seq
    xf = x.reshape(n, D_MODEL)
    prompt = past_k is None
    p, logf, k32, v32 = _in_proj(xf, w["g_norm"], w["w_in"], w["lb_logits"], w["g_dq"], w["g_dk"], w["g_mq"],
                                 seq=seq, k_transposed=prompt)
    ya, s_new = _hgrn(p, logf, s0, w["g_hg_out"], batch, seq)
    if prompt:
        yb = _diff_attn(w["lam_vecs"], p, w["g_dsub"], batch=batch, seq=seq, lam_init=lam_init)
    else:
        past = past_k.shape[1]
        kt_cache = jnp.transpose(past_k, (0, 2, 3, 4, 1)).reshape(batch, D_MODEL, past)
        yb = _cached_diff_attn(w["lam_vecs"], p, w["g_dsub"], kt_cache, past_v, batch=batch, seq=seq,
                               lam_init=lam_init)
    y = _merge(xf, ya, yb, p, mem_k, mem_v, w["w_branch"], w["w_out"], seq=seq, n_mem=n_mem)
    if prompt:
        k_out = jnp.transpose(k32.reshape(batch, DF_HEADS, 2, DF_HD, seq), (0, 4, 1, 2, 3))
    else:
        k_out = k32.reshape(batch, seq, DF_HEADS, 2, DF_HD)
    return y.reshape(batch, seq, D_MODEL), s_new, k_out, v32.reshape(batch, seq, DF_HEADS, DF_VD)


def kernel(x_prompt, x_sample, mem_prompt, cache_diff_k, cache_diff_v, cache_mem_k, cache_mem_v, state_hgrn,
           g_norm, w_in, hg_lb_logits, g_hg_out, g_dq, g_dk, lam_q1, lam_k1, lam_q2, lam_k2, g_dsub, g_mem,
           w_mkv, g_mq, g_mk, w_branch, w_out):
    depth = g_norm.shape[0]
    assert depth == 1 and hg_lb_logits.shape[0] == 2, "single-layer step only"
    batch, n_mem = mem_prompt.shape[:2]
    lam_init = 0.8 - 0.6
    w_in_groups = w_in[0].astype(BF16).reshape(D_MODEL, N_COLGROUPS, D_MODEL).transpose(1, 0, 2)
    w = dict(g_norm=g_norm[0], w_in=w_in_groups, lb_logits=hg_lb_logits, g_hg_out=g_hg_out[0],
             g_dq=g_dq[0], g_dk=g_dk[0], g_mq=g_mq[0], g_dsub=g_dsub[0],
             lam_vecs=jnp.stack([lam_q1[0], lam_k1[0], lam_q2[0], lam_k2[0]]),
             w_branch=w_branch[0].astype(BF16), w_out=w_out[0].astype(BF16))

    mem_k, mem_v, mem_kb, mem_vb = _memory_kv(mem_prompt.reshape(batch * n_mem, D_MODEL), g_mem[0], w_mkv[0], g_mk[0],
                                              batch=batch, n_mem=n_mem)
    s0 = jnp.zeros((batch, HG_HEADS, HG_D, HG_D), F32)
    y_p, s_p, k_p, v_p = _layer(x_prompt, s0, None, None, mem_kb, mem_vb, lam_init, w, n_mem)
    y_s, s_s, k_s, v_s = _layer(x_sample, state_hgrn[0], cache_diff_k[0], cache_diff_v[0],
                                cache_mem_k[0].reshape(-1, D_MODEL).astype(BF16),
                                cache_mem_v[0].reshape(-1, D_MODEL).astype(BF16), lam_init, w, n_mem)
    return (y_p, y_s, s_p[None], s_s[None], k_p[None], v_p[None], k_s[None], v_s[None], mem_k[None], mem_v[None])
```

```python
import functools

import jax
import jax.numpy as jnp
from jax import lax
from jax.experimental import pallas as pl
from jax.experimental.pallas import tpu as pltpu

F32 = jnp.float32
BF16 = jnp.bfloat16

EPS = 1e-6
LOG2E = 1.4426950408889634
D_MODEL = 1024
CHUNK = 64
GLA_BLOCK = 16
HG_HEADS, HG_D = 8, 128
DF_HEADS, DF_HD, DF_VD = 8, 64, 128
MEM_HEADS, MEM_HD = 4, 256
N_BRANCH = 3
N_COLGROUPS = 13
(CG_HQ, CG_HF, CG_HI, CG_ZA, CG_DQ, CG_DK, CG_DV, CG_ZB, CG_MQ, CG_ZM, CG_G0) = range(11)

LANE = 128
ONES_ROWS = 16
VMEM_LIMIT = 52 * 1024 * 1024
VMEM_LIMIT_INPROJ = 60 * 1024 * 1024

NT = (((1,), (1,)), ((), ()))
TN = (((0,), (0,)), ((), ()))


def _dot(a, b):
    return jnp.dot(a, b, preferred_element_type=F32)


def _dot_nt(a, b):
    return lax.dot_general(a, b, NT, preferred_element_type=F32)


def _dot_tn(a, b):
    return lax.dot_general(a, b, TN, preferred_element_type=F32)


def _sigmoid(x):
    return 0.5 * jnp.tanh(0.5 * x) + 0.5


def _rms(x, g):
    return x * lax.rsqrt(jnp.mean(x * x, axis=-1, keepdims=True) + EPS) * g


def _params(*sem, vmem=VMEM_LIMIT):
    return pltpu.CompilerParams(dimension_semantics=sem, vmem_limit_bytes=vmem)


def _memkv_kernel(mem_ref, g_ref, w_ref, gk_ref, k_ref, v_ref, kb_ref, vb_ref):
    xn = _rms(mem_ref[...], g_ref[...]).astype(BF16)
    kv = _dot(xn, w_ref[...])
    for h in range(MEM_HEADS):
        sl = slice(h * MEM_HD, (h + 1) * MEM_HD)
        k = _rms(kv[:, sl], gk_ref[...])
        k_ref[0, :, h, :] = k
        kb_ref[:, sl] = k.astype(BF16)
        v_ref[0, :, h, :] = kv[:, D_MODEL + h * MEM_HD:D_MODEL + (h + 1) * MEM_HD]
    vb_ref[...] = kv[:, D_MODEL:].astype(BF16)


def _memory_kv(mem, g_mem, w_mkv, g_mk, *, batch, n_mem):
    out = pl.BlockSpec((1, n_mem, MEM_HEADS, MEM_HD), lambda i: (i, 0, 0, 0))
    flat = pl.BlockSpec((n_mem, D_MODEL), lambda i: (i, 0))
    return pl.pallas_call(
        _memkv_kernel,
        grid=(batch,),
        in_specs=[
            pl.BlockSpec((n_mem, D_MODEL), lambda i: (i, 0)),
            pl.BlockSpec((1, D_MODEL), lambda i: (0, 0)),
            pl.BlockSpec((D_MODEL, 2 * D_MODEL), lambda i: (0, 0)),
            pl.BlockSpec((1, MEM_HD), lambda i: (0, 0)),
        ],
        out_specs=[out, out, flat, flat],
        out_shape=[jax.ShapeDtypeStruct((batch, n_mem, MEM_HEADS, MEM_HD), F32)] * 2
        + [jax.ShapeDtypeStruct((batch * n_mem, D_MODEL), BF16)] * 2,
        compiler_params=_params("arbitrary"),
        name="memory_kv",
    )(mem, g_mem.reshape(1, -1), w_mkv.astype(BF16), g_mk.reshape(1, -1))


def _half_head_rms(acc, g_ref, scale):
    lane = lax.broadcasted_iota(jnp.int32, (1, LANE), 1)
    lo = lane < DF_HD
    outs = []
    for h in range(DF_HEADS):
        sl = slice(h * LANE, (h + 1) * LANE)
        xh = acc[:, sl]
        sq = xh * xh
        s_lo = jnp.sum(jnp.where(lo, sq, 0.0), axis=-1, keepdims=True)
        s_hi = jnp.sum(jnp.where(lo, 0.0, sq), axis=-1, keepdims=True)
        r = jnp.where(lo, lax.rsqrt(s_lo * (1.0 / DF_HD) + EPS), lax.rsqrt(s_hi * (1.0 / DF_HD) + EPS))
        outs.append(xh * r * (g_ref[:, sl] * scale))
    return outs


def _inproj_kernel(x_ref, gn_ref, w_ref, lbl_ref, gq_ref, gk_ref, gmq_ref,
                   p_ref, logf_ref, k_ref, v_ref, xn_scr, *, k_transposed):
    xn_scr[...] = _rms(x_ref[...], gn_ref[...]).astype(BF16)

    def proj(j):
        return _dot(xn_scr[...], w_ref[:, j * D_MODEL:(j + 1) * D_MODEL])

    def put(j, val):
        p_ref[:, j * D_MODEL:(j + 1) * D_MODEL] = val.astype(BF16)

    put(CG_HQ, proj(CG_HQ))

    l = lbl_ref[...]
    e = jnp.exp(l - jnp.max(l, axis=0, keepdims=True))
    lb = e[0:1] / jnp.sum(e, axis=0, keepdims=True)
    sp = _sigmoid(proj(CG_HF))
    logf_ref[...] = jnp.log(lb + (1.0 - lb) * sp)
    put(CG_HF, (1.0 - lb) * (1.0 - sp))

    put(CG_HI, proj(CG_HI))

    for j in (CG_ZA, CG_ZB, CG_ZM):
        acc = proj(j)
        put(j, acc * _sigmoid(acc))

    outs = _half_head_rms(proj(CG_DQ), gq_ref, DF_HD ** -0.5 * LOG2E)
    for h in range(DF_HEADS):
        p_ref[:, CG_DQ * D_MODEL + h * LANE:CG_DQ * D_MODEL + (h + 1) * LANE] = outs[h].astype(BF16)

    outs = _half_head_rms(proj(CG_DK), gk_ref, 1.0)
    for h in range(DF_HEADS):
        sl = slice(h * LANE, (h + 1) * LANE)
        p_ref[:, CG_DK * D_MODEL + h * LANE:CG_DK * D_MODEL + (h + 1) * LANE] = outs[h].astype(BF16)
        if k_transposed:
            k_ref[0, sl, :] = outs[h].T
        else:
            k_ref[:, sl] = outs[h]

    acc = proj(CG_DV)
    put(CG_DV, acc)
    v_ref[...] = acc

    acc = proj(CG_MQ)
    for h in range(MEM_HEADS):
        sl = slice(h * MEM_HD, (h + 1) * MEM_HD)
        p_ref[:, CG_MQ * D_MODEL + h * MEM_HD:CG_MQ * D_MODEL + (h + 1) * MEM_HD] = (
            _rms(acc[:, sl], gmq_ref[...]) * (MEM_HD ** -0.5)).astype(BF16)

    for j in range(CG_G0, N_COLGROUPS):
        put(j, _sigmoid(proj(j)))


def _in_proj(x, g_norm, w_in, lb_logits, g_dq, g_dk, g_mq, *, seq, k_transposed):
    n = x.shape[0]
    tm = min(256, seq) if k_transposed else min(256, n)
    row = lambda i: (i, 0)
    const = lambda i: (0, 0)
    if k_transposed:
        per_b = seq // tm
        k_spec = pl.BlockSpec((1, D_MODEL, tm), lambda i: (i // per_b, 0, i % per_b))
        k_shape = jax.ShapeDtypeStruct((n // seq, D_MODEL, seq), F32)
    else:
        k_spec = pl.BlockSpec((tm, D_MODEL), row)
        k_shape = jax.ShapeDtypeStruct((n, D_MODEL), F32)
    return pl.pallas_call(
        functools.partial(_inproj_kernel, k_transposed=k_transposed),
        grid=(n // tm,),
        in_specs=[
            pl.BlockSpec((tm, D_MODEL), row),
            pl.BlockSpec((1, D_MODEL), const),
            pl.BlockSpec((D_MODEL, N_COLGROUPS * D_MODEL), lambda i: (0, 0), pipeline_mode=pl.Buffered(1)),
            pl.BlockSpec((2, D_MODEL), const),
            pl.BlockSpec((1, D_MODEL), const),
            pl.BlockSpec((1, D_MODEL), const),
            pl.BlockSpec((1, MEM_HD), const),
        ],
        out_specs=[
            pl.BlockSpec((tm, N_COLGROUPS * D_MODEL), row),
            pl.BlockSpec((tm, D_MODEL), row),
            k_spec,
            pl.BlockSpec((tm, D_MODEL), row),
        ],
        out_shape=[
            jax.ShapeDtypeStruct((n, N_COLGROUPS * D_MODEL), BF16),
            jax.ShapeDtypeStruct((n, D_MODEL), F32),
            k_shape,
            jax.ShapeDtypeStruct((n, D_MODEL), F32),
        ],
        scratch_shapes=[pltpu.VMEM((tm, D_MODEL), BF16)],
        compiler_params=_params("arbitrary", vmem=VMEM_LIMIT_INPROJ),
        name="in_proj",
    )(x, g_norm.reshape(1, -1), w_in, lb_logits,
      jnp.tile(g_dq, 2 * DF_HEADS).reshape(1, -1), jnp.tile(g_dk, 2 * DF_HEADS).reshape(1, -1), g_mq.reshape(1, -1))


def _split_bf16(x):
    hi = x.astype(BF16)
    return hi, (x - hi.astype(F32)).astype(BF16)


def _hgrn_kernel(q_ref, k_ref, v_ref, z_ref, lf_ref, s0_ref, g_ref, y_ref, s_ref, st_scr, *, seq, tc, hp):
    nb = tc // GLA_BLOCK
    row = lax.broadcasted_iota(jnp.int32, (tc, tc), 0)
    col = lax.broadcasted_iota(jnp.int32, (tc, tc), 1)
    causal = col <= row
    tril = jnp.where(causal, 1.0, 0.0).astype(BF16)
    rowk = lax.broadcasted_iota(jnp.int32, (tc, LANE), 0)

    @pl.when(pl.program_id(1) == 0)
    def _():
        for h in range(hp):
            st_scr[h] = s0_ref[0, h].T

    n_chunks = seq // tc
    unroll = 2 if n_chunks % 2 == 0 else 1

    stacked_att = tc % LANE == 0
    tril2 = jnp.concatenate([tril, tril], axis=1)

    def chunks(c, carry):
        heads = [slice(h * LANE, (h + 1) * LANE) for h in range(hp)]
        for u in range(unroll):
            rows = pl.ds(pl.multiple_of((c * unroll + u) * tc, tc), tc)
            g_hi, g_lo = _split_bf16(lf_ref[rows, :])
            b_all = _dot(tril2, jnp.concatenate([g_hi, g_lo], axis=0))
            bs = [b_all[:, sl] for sl in heads]
            qs = [q_ref[rows, sl].astype(F32) for sl in heads]
            ks = [k_ref[rows, sl].astype(F32) for sl in heads]
            vs = [v_ref[rows, sl] for sl in heads]

            outs = []
            for h in range(hp):
                b, st = bs[h], st_scr[h]
                b_end = b[tc - 1:tc, :]
                outs.append(_dot_nt((qs[h] * jnp.exp(b)).astype(BF16), st.astype(BF16)))
                k_dec = (ks[h] * jnp.exp(b_end - b)).astype(BF16)
                st_scr[h] = jnp.exp(b_end) * st + _dot_tn(vs[h], k_dec)

            atts = []
            for h in range(hp):
                b, q, k = bs[h], qs[h], ks[h]
                starts = [b[n * GLA_BLOCK - 1:n * GLA_BLOCK, :] if n else jnp.zeros((1, LANE), F32) for n in range(nb)]
                k_inv = [(k * jnp.exp(jnp.where(rowk < (n + 1) * GLA_BLOCK, starts[n] - b, 0.0))).astype(BF16)
                         for n in range(nb)]
                if stacked_att:
                    g_rows = jnp.concatenate([jnp.broadcast_to(g, (GLA_BLOCK, LANE)) for g in starts], axis=0)
                    q_dec = (q * jnp.exp(b - g_rows)).astype(BF16)
                    wide = _dot_nt(q_dec, jnp.concatenate(k_inv, axis=0))
                    att = [wide[n * GLA_BLOCK:(n + 1) * GLA_BLOCK, n * tc:(n + 1) * tc] for n in range(nb)]
                else:
                    att = [_dot_nt((q[n * GLA_BLOCK:(n + 1) * GLA_BLOCK]
                                    * jnp.exp(b[n * GLA_BLOCK:(n + 1) * GLA_BLOCK] - starts[n])).astype(BF16), k_inv[n])
                           for n in range(nb)]
                atts.append(jnp.where(causal, jnp.concatenate(att, axis=0), 0.0).astype(BF16))

            for h in range(hp):
                o = outs[h] + _dot(atts[h], vs[h])
                y = _rms(o, g_ref[...]) * z_ref[rows, heads[h]].astype(F32)
                y_ref[rows, heads[h]] = y.astype(BF16)
        return carry

    lax.fori_loop(0, n_chunks // unroll, chunks, 0)

    @pl.when(pl.program_id(1) == pl.num_programs(1) - 1)
    def _():
        for h in range(hp):
            s_ref[0, h] = st_scr[h].T


def _hgrn(p, logf, s0, g_hg_out, batch, seq):
    hp = HG_HEADS
    ts = min(512, seq)
    tc = min(128, ts)
    per_b = seq // ts

    def col(cg):
        return pl.BlockSpec((ts, D_MODEL), lambda b, t: (b * per_b + t, cg))

    st = pl.BlockSpec((1, hp, HG_D, HG_D), lambda b, t: (b, 0, 0, 0))
    return pl.pallas_call(
        functools.partial(_hgrn_kernel, seq=ts, tc=tc, hp=hp),
        grid=(batch, per_b),
        in_specs=[col(CG_HQ), col(CG_HF), col(CG_HI), col(CG_ZA), col(0), st,
                  pl.BlockSpec((1, HG_D), lambda b, t: (0, 0))],
        out_specs=[col(0), st],
        out_shape=[jax.ShapeDtypeStruct((batch * seq, D_MODEL), BF16),
                   jax.ShapeDtypeStruct((batch, HG_HEADS, HG_D, HG_D), F32)],
        scratch_shapes=[pltpu.VMEM((hp, HG_D, HG_D), F32)],
        compiler_params=_params("arbitrary", "arbitrary"),
        name="hgrn",
    )(p, p, p, p, logf, s0, g_hg_out.reshape(1, -1))


def _lambda(lam_ref, lam_init):
    lv = lam_ref[...]
    return (jnp.exp(jnp.sum(lv[0:1] * lv[1:2], axis=-1, keepdims=True))
            - jnp.exp(jnp.sum(lv[2:3] * lv[3:4], axis=-1, keepdims=True)) + lam_init)


def _split_maps(q):
    first = lax.broadcasted_iota(jnp.int32, (1, LANE), 1) < DF_HD
    zero = jnp.zeros((), q.dtype)
    return jnp.concatenate([jnp.where(first, q, zero), jnp.where(first, zero, q)], axis=0)


def _diffattn_kernel(lam_ref, q_ref, k_ref, v_ref, z_ref, g_ref, y_ref, s_scr, p_scr, vt_scr,
                     *, n_q, tq, rc, lam_init):
    w = 2 * tq
    lam = _lambda(lam_ref, lam_init)
    vt_scr[0:DF_VD, :] = v_ref[...].astype(F32).T.astype(BF16)
    vt_scr[DF_VD:, :] = jnp.ones((ONES_ROWS, vt_scr.shape[1]), BF16)

    key = lax.broadcasted_iota(jnp.int32, (tq, w), 0)
    qry = lax.broadcasted_iota(jnp.int32, (tq, w), 1) & (tq - 1)
    bias = jnp.where((key // CHUNK) <= (qry // CHUNK), 0.0, -jnp.inf)

    def fold8(x, op):
        return op(x.reshape(rc // 8, 8, w), axis=0)

    def scores(i):
        n_vis = (i + 1) * tq
        s_buf = s_scr.at[i % 2]
        s = _dot_nt(k_ref[0:n_vis, :], _split_maps(q_ref[i * tq:(i + 1) * tq, :]))
        m8 = None
        for c in range(n_vis // rc):
            blk = s[c * rc:(c + 1) * rc]
            if (c + 1) * rc > n_vis - tq:
                blk = blk + bias[c * rc - (n_vis - tq):(c + 1) * rc - (n_vis - tq)]
            s_buf[c * rc:(c + 1) * rc, :] = blk
            m8 = fold8(blk, jnp.max) if m8 is None else jnp.maximum(m8, fold8(blk, jnp.max))
        return jnp.max(m8, axis=0, keepdims=True)

    def probs(i, m):
        s_buf, p_buf = s_scr.at[i % 2], p_scr.at[i % 2]
        for c in range((i + 1) * tq // rc):
            rows = slice(c * rc, (c + 1) * rc)
            p_buf[rows, :] = jnp.exp2(s_buf[rows, :] - m).astype(BF16)

    maxes = {0: scores(0)}
    if n_q > 1:
        maxes[1] = scores(1)
    probs(0, maxes[0])
    for i in range(n_q):
        n_vis = (i + 1) * tq
        p_buf = p_scr.at[i % 2]
        rows_q = slice(i * tq, (i + 1) * tq)
        if i + 2 < n_q:
            maxes[i + 2] = scores(i + 2)
        pv = _dot(vt_scr[:, 0:n_vis], p_buf[0:n_vis, :])
        if i + 1 < n_q:
            probs(i + 1, maxes[i + 1])
        rinv = 1.0 / pv[DF_VD:DF_VD + 1, :]
        o = (pv[:DF_VD, :tq] * rinv[:, :tq] - pv[:DF_VD, tq:] * (rinv[:, tq:] * lam)).T
        y = _rms(o, g_ref[...]) * (1.0 - lam_init) * z_ref[rows_q, :].astype(F32)
        y_ref[rows_q, :] = y.astype(BF16)


def _diff_attn(lam_vecs, p, g_dsub, *, batch, seq, lam_init):
    tq = rc = 256

    def col(cg):
        return pl.BlockSpec((seq, LANE), lambda b, h: (b, cg * DF_HEADS + h))

    return pl.pallas_call(
        functools.partial(_diffattn_kernel, n_q=seq // tq, tq=tq, rc=rc, lam_init=lam_init),
        grid=(batch, DF_HEADS),
        in_specs=[pl.BlockSpec((4, DF_HD), lambda b, h: (0, 0)),
                  col(CG_DQ), col(CG_DK), col(CG_DV), col(CG_ZB),
                  pl.BlockSpec((1, DF_VD), lambda b, h: (0, 0))],
        out_specs=pl.BlockSpec((seq, LANE), lambda b, h: (b, h)),
        out_shape=jax.ShapeDtypeStruct((batch * seq, D_MODEL), BF16),
        scratch_shapes=[pltpu.VMEM((2, seq, 2 * tq), F32), pltpu.VMEM((2, seq, 2 * tq), BF16),
                        pltpu.VMEM((DF_VD + ONES_ROWS, seq), BF16)],
        compiler_params=_params("arbitrary", "arbitrary"),
        name="diff_attn",
    )(lam_vecs, p, p, p, p, g_dsub.reshape(1, -1))


def _cached_diffattn_kernel(lam_ref, q_ref, kn_ref, vn_ref, z_ref, g_ref, kt_ref, vc_ref, y_ref, *, tq, lam_init):
    lam = _lambda(lam_ref, lam_init)
    heads = [slice(h * LANE, (h + 1) * LANE) for h in range(DF_HEADS)]
    scores = []
    for sl in heads:
        qs = _split_maps(q_ref[:, sl])
        scores.append((_dot(qs, kt_ref[0, sl, :].astype(BF16)),
                       _dot_nt(qs, kn_ref[:, sl])))
    for h, sl in enumerate(heads):
        s_c, s_n = scores[h]
        m = jnp.maximum(jnp.max(s_c, axis=-1, keepdims=True), jnp.max(s_n, axis=-1, keepdims=True))
        p_c = jnp.exp2(s_c - m)
        p_n = jnp.exp2(s_n - m)
        rinv = 1.0 / (jnp.sum(p_c, axis=-1, keepdims=True) + jnp.sum(p_n, axis=-1, keepdims=True))
        r1 = rinv[:tq]
        r2 = rinv[tq:] * lam
        a_c = (p_c[:tq] * r1 - p_c[tq:] * r2).astype(BF16)
        a_n = (p_n[:tq] * r1 - p_n[tq:] * r2).astype(BF16)
        o = _dot(a_c, vc_ref[0, :, h, :].astype(BF16)) + _dot(a_n, vn_ref[:, sl])
        y = _rms(o, g_ref[...]) * (1.0 - lam_init) * z_ref[:, sl].astype(F32)
        y_ref[:, sl] = y.astype(BF16)


def _cached_diff_attn(lam_vecs, p, g_dsub, kt_cache, v_cache, *, batch, seq, lam_init):
    past = v_cache.shape[1]

    def col(cg):
        return pl.BlockSpec((seq, D_MODEL), lambda b: (b, cg))

    return pl.pallas_call(
        functools.partial(_cached_diffattn_kernel, tq=seq, lam_init=lam_init),
        grid=(batch,),
        in_specs=[pl.BlockSpec((4, DF_HD), lambda b: (0, 0)),
                  col(CG_DQ), col(CG_DK), col(CG_DV), col(CG_ZB),
                  pl.BlockSpec((1, DF_VD), lambda b: (0, 0)),
                  pl.BlockSpec((1, D_MODEL, past), lambda b: (b, 0, 0)),
                  pl.BlockSpec((1, past, DF_HEADS, DF_VD), lambda b: (b, 0, 0, 0))],
        out_specs=pl.BlockSpec((seq, D_MODEL), lambda b: (b, 0)),
        out_shape=jax.ShapeDtypeStruct((batch * seq, D_MODEL), BF16),
        compiler_params=_params("arbitrary"),
        name="cached_diff_attn",
    )(lam_vecs, p, p, p, p, g_dsub.reshape(1, -1), kt_cache, v_cache)


def _merge_kernel(x_ref, ya_ref, yb_ref, qm_ref, zm_ref, g0_ref, g1_ref, g2_ref, mk_ref, mv_ref,
                  wb_ref, wo_ref, y_ref):
    heads = [slice(h * MEM_HD, (h + 1) * MEM_HD) for h in range(MEM_HEADS)]
    scores = [_dot_nt(qm_ref[:, sl], mk_ref[:, sl]) for sl in heads]
    hsum = (g0_ref[...].astype(F32) * _dot(ya_ref[...], wb_ref[0])
            + g1_ref[...].astype(F32) * _dot(yb_ref[...], wb_ref[1]))
    ym = []
    for s, sl in zip(scores, heads):
        p = jnp.exp(s - jnp.max(s, axis=-1, keepdims=True))
        rinv = 1.0 / jnp.sum(p, axis=-1, keepdims=True)
        o = _dot(p.astype(BF16), mv_ref[:, sl]) * rinv
        ym.append((o * zm_ref[:, sl].astype(F32)).astype(BF16))
    hsum = hsum + g2_ref[...].astype(F32) * _dot(jnp.concatenate(ym, axis=-1), wb_ref[2])
    y_ref[...] = x_ref[...] + _dot(hsum.astype(BF16), wo_ref[...])


def _merge(x, ya, yb, p, mem_k, mem_v, w_branch, w_out, *, seq, n_mem):
    n = x.shape[0]
    tm = min(256, seq)
    per_b = seq // tm
    row = lambda i: (i, 0)

    def col(cg):
        return pl.BlockSpec((tm, D_MODEL), lambda i: (i, cg))

    mem = pl.BlockSpec((n_mem, D_MODEL), lambda i: (i // per_b, 0))
    return pl.pallas_call(
        _merge_kernel,
        grid=(n // tm,),
        in_specs=[pl.BlockSpec((tm, D_MODEL), row), pl.BlockSpec((tm, D_MODEL), row), pl.BlockSpec((tm, D_MODEL), row),
                  col(CG_MQ), col(CG_ZM), col(CG_G0), col(CG_G0 + 1), col(CG_G0 + 2), mem, mem,
                  pl.BlockSpec((N_BRANCH, D_MODEL, D_MODEL), lambda i: (0, 0, 0), pipeline_mode=pl.Buffered(1)),
                  pl.BlockSpec((D_MODEL, D_MODEL), lambda i: (0, 0), pipeline_mode=pl.Buffered(1))],
        out_specs=pl.BlockSpec((tm, D_MODEL), row),
        out_shape=jax.ShapeDtypeStruct((n, D_MODEL), F32),
        compiler_params=_params("arbitrary"),
        name="merge",
    )(x, ya, yb, p, p, p, p, p, mem_k, mem_v, w_branch, w_out)


def _layer(x, s0, past_k, past_v, mem_k, mem_v, lam_init, w, n_mem):
    batch, seq, _ = x.shape
    n = batch * seq
    xf = x.reshape(n, D_MODEL)
    prompt = past_k is None
    p, logf, k32, v32 = _in_proj(xf, w["g_norm"], w["w_in"], w["lb_logits"], w["g_dq"], w["g_dk"], w["g_mq"],
                                 seq=seq, k_transposed=prompt)
    ya, s_new = _hgrn(p, logf, s0, w["g_hg_out"], batch, seq)
    if prompt:
        yb = _diff_attn(w["lam_vecs"], p, w["g_dsub"], batch=batch, seq=seq, lam_init=lam_init)
    else:
        past = past_k.shape[1]
        kt_cache = jnp.transpose(past_k, (0, 2, 3, 4, 1)).reshape(batch, D_MODEL, past)
        yb = _cached_diff_attn(w["lam_vecs"], p, w["g_dsub"], kt_cache, past_v, batch=batch, seq=seq,
                               lam_init=lam_init)
    y = _merge(xf, ya, yb, p, mem_k, mem_v, w["w_branch"], w["w_out"], seq=seq, n_mem=n_mem)
    if prompt:
        k_out = jnp.transpose(k32.reshape(batch, DF_HEADS, 2, DF_HD, seq), (0, 4, 1, 2, 3))
    else:
        k_out = k32.reshape(batch, seq, DF_HEADS, 2, DF_HD)
    return y.reshape(batch, seq, D_MODEL), s_new, k_out, v32.reshape(batch, seq, DF_HEADS, DF_VD)


def kernel(x_prompt, x_sample, mem_prompt, cache_diff_k, cache_diff_v, cache_mem_k, cache_mem_v, state_hgrn,
           g_norm, w_in, hg_lb_logits, g_hg_out, g_dq, g_dk, lam_q1, lam_k1, lam_q2, lam_k2, g_dsub, g_mem,
           w_mkv, g_mq, g_mk, w_branch, w_out):
    depth = g_norm.shape[0]
    assert depth == 1 and hg_lb_logits.shape[0] == 2, "single-layer step only"
    batch, n_mem = mem_prompt.shape[:2]
    lam_init = 0.8 - 0.6
    w = dict(g_norm=g_norm[0], w_in=w_in[0].astype(BF16), lb_logits=hg_lb_logits, g_hg_out=g_hg_out[0],
             g_dq=g_dq[0], g_dk=g_dk[0], g_mq=g_mq[0], g_dsub=g_dsub[0],
             lam_vecs=jnp.stack([lam_q1[0], lam_k1[0], lam_q2[0], lam_k2[0]]),
             w_branch=w_branch[0].astype(BF16), w_out=w_out[0].astype(BF16))

    mem_k, mem_v, mem_kb, mem_vb = _memory_kv(mem_prompt.reshape(batch * n_mem, D_MODEL), g_mem[0], w_mkv[0], g_mk[0],
                                              batch=batch, n_mem=n_mem)
    s0 = jnp.zeros((batch, HG_HEADS, HG_D, HG_D), F32)
    y_p, s_p, k_p, v_p = _layer(x_prompt, s0, None, None, mem_kb, mem_vb, lam_init, w, n_mem)
    y_s, s_s, k_s, v_s = _layer(x_sample, state_hgrn[0], cache_diff_k[0], cache_diff_v[0],
                                cache_mem_k[0].reshape(-1, D_MODEL).astype(BF16),
                                cache_mem_v[0].reshape(-1, D_MODEL).astype(BF16), lam_init, w, n_mem)
    return (y_p, y_s, s_p[None], s_s[None], k_p[None], v_p[None], k_s[None], v_s[None], mem_k[None], mem_v[None])
```

```python
import functools

import jax
import jax.numpy as jnp
from jax import lax
from jax.experimental import pallas as pl
from jax.experimental.pallas import tpu as pltpu

F32 = jnp.float32
BF16 = jnp.bfloat16

EPS = 1e-6
LOG2E = 1.4426950408889634
D_MODEL = 1024
CHUNK = 64
GLA_BLOCK = 16
HG_HEADS, HG_D = 8, 128
DF_HEADS, DF_HD, DF_VD = 8, 64, 128
MEM_HEADS, MEM_HD = 4, 256
N_BRANCH = 3
N_COLGROUPS = 13
(CG_HQ, CG_HF, CG_HI, CG_ZA, CG_DQ, CG_DK, CG_DV, CG_ZB, CG_MQ, CG_ZM, CG_G0) = range(11)

LANE = 128
ONES_ROWS = 16
VMEM_LIMIT = 52 * 1024 * 1024
VMEM_LIMIT_INPROJ = 60 * 1024 * 1024

NT = (((1,), (1,)), ((), ()))
TN = (((0,), (0,)), ((), ()))


def _dot(a, b):
    return jnp.dot(a, b, preferred_element_type=F32)


def _dot_nt(a, b):
    return lax.dot_general(a, b, NT, preferred_element_type=F32)


def _dot_tn(a, b):
    return lax.dot_general(a, b, TN, preferred_element_type=F32)


def _sigmoid(x):
    return 0.5 * jnp.tanh(0.5 * x) + 0.5


def _rms(x, g):
    return x * lax.rsqrt(jnp.mean(x * x, axis=-1, keepdims=True) + EPS) * g


def _params(*sem, vmem=VMEM_LIMIT):
    return pltpu.CompilerParams(dimension_semantics=sem, vmem_limit_bytes=vmem)


def _memkv_kernel(mem_ref, g_ref, w_ref, gk_ref, k_ref, v_ref, kb_ref, vb_ref):
    xn = _rms(mem_ref[...], g_ref[...]).astype(BF16)
    kv = _dot(xn, w_ref[...])
    for h in range(MEM_HEADS):
        sl = slice(h * MEM_HD, (h + 1) * MEM_HD)
        k = _rms(kv[:, sl], gk_ref[...])
        k_ref[0, :, h, :] = k
        kb_ref[:, sl] = k.astype(BF16)
        v_ref[0, :, h, :] = kv[:, D_MODEL + h * MEM_HD:D_MODEL + (h + 1) * MEM_HD]
    vb_ref[...] = kv[:, D_MODEL:].astype(BF16)


def _memory_kv(mem, g_mem, w_mkv, g_mk, *, batch, n_mem):
    out = pl.BlockSpec((1, n_mem, MEM_HEADS, MEM_HD), lambda i: (i, 0, 0, 0))
    flat = pl.BlockSpec((n_mem, D_MODEL), lambda i: (i, 0))
    return pl.pallas_call(
        _memkv_kernel,
        grid=(batch,),
        in_specs=[
            pl.BlockSpec((n_mem, D_MODEL), lambda i: (i, 0)),
            pl.BlockSpec((1, D_MODEL), lambda i: (0, 0)),
            pl.BlockSpec((D_MODEL, 2 * D_MODEL), lambda i: (0, 0)),
            pl.BlockSpec((1, MEM_HD), lambda i: (0, 0)),
        ],
        out_specs=[out, out, flat, flat],
        out_shape=[jax.ShapeDtypeStruct((batch, n_mem, MEM_HEADS, MEM_HD), F32)] * 2
        + [jax.ShapeDtypeStruct((batch * n_mem, D_MODEL), BF16)] * 2,
        compiler_params=_params("arbitrary"),
        name="memory_kv",
    )(mem, g_mem.reshape(1, -1), w_mkv.astype(BF16), g_mk.reshape(1, -1))


def _half_head_rms(acc, g_ref, scale):
    lane = lax.broadcasted_iota(jnp.int32, (1, LANE), 1)
    lo = lane < DF_HD
    outs = []
    for h in range(DF_HEADS):
        sl = slice(h * LANE, (h + 1) * LANE)
        xh = acc[:, sl]
        sq = xh * xh
        s_lo = jnp.sum(jnp.where(lo, sq, 0.0), axis=-1, keepdims=True)
        s_hi = jnp.sum(jnp.where(lo, 0.0, sq), axis=-1, keepdims=True)
        r = jnp.where(lo, lax.rsqrt(s_lo * (1.0 / DF_HD) + EPS), lax.rsqrt(s_hi * (1.0 / DF_HD) + EPS))
        outs.append(xh * r * (g_ref[:, sl] * scale))
    return outs


def _inproj_kernel(x_ref, gn_ref, w_ref, lbl_ref, gq_ref, gk_ref, gmq_ref,
                   p_ref, logf_ref, k_ref, v_ref, xn_scr, *, k_transposed):
    xn_scr[...] = _rms(x_ref[...], gn_ref[...]).astype(BF16)

    def proj(j):
        return _dot(xn_scr[...], w_ref[:, j * D_MODEL:(j + 1) * D_MODEL])

    def put(j, val):
        p_ref[:, j * D_MODEL:(j + 1) * D_MODEL] = val.astype(BF16)

    put(CG_HQ, proj(CG_HQ))

    l = lbl_ref[...]
    e = jnp.exp(l - jnp.max(l, axis=0, keepdims=True))
    lb = e[0:1] / jnp.sum(e, axis=0, keepdims=True)
    sp = _sigmoid(proj(CG_HF))
    logf_ref[...] = jnp.log(lb + (1.0 - lb) * sp)
    put(CG_HF, (1.0 - lb) * (1.0 - sp))

    put(CG_HI, proj(CG_HI))

    for j in (CG_ZA, CG_ZB, CG_ZM):
        acc = proj(j)
        put(j, acc * _sigmoid(acc))

    outs = _half_head_rms(proj(CG_DQ), gq_ref, DF_HD ** -0.5 * LOG2E)
    for h in range(DF_HEADS):
        p_ref[:, CG_DQ * D_MODEL + h * LANE:CG_DQ * D_MODEL + (h + 1) * LANE] = outs[h].astype(BF16)

    outs = _half_head_rms(proj(CG_DK), gk_ref, 1.0)
    for h in range(DF_HEADS):
        sl = slice(h * LANE, (h + 1) * LANE)
        p_ref[:, CG_DK * D_MODEL + h * LANE:CG_DK * D_MODEL + (h + 1) * LANE] = outs[h].astype(BF16)
        if k_transposed:
            k_ref[0, sl, :] = outs[h].T
        else:
            k_ref[:, sl] = outs[h]

    acc = proj(CG_DV)
    put(CG_DV, acc)
    v_ref[...] = acc

    acc = proj(CG_MQ)
    for h in range(MEM_HEADS):
        sl = slice(h * MEM_HD, (h + 1) * MEM_HD)
        p_ref[:, CG_MQ * D_MODEL + h * MEM_HD:CG_MQ * D_MODEL + (h + 1) * MEM_HD] = (
            _rms(acc[:, sl], gmq_ref[...]) * (MEM_HD ** -0.5)).astype(BF16)

    for j in range(CG_G0, N_COLGROUPS):
        put(j, _sigmoid(proj(j)))


def _in_proj(x, g_norm, w_in, lb_logits, g_dq, g_dk, g_mq, *, seq, k_transposed):
    n = x.shape[0]
    tm = min(256, seq) if k_transposed else min(256, n)
    row = lambda i: (i, 0)
    const = lambda i: (0, 0)
    if k_transposed:
        per_b = seq // tm
        k_spec = pl.BlockSpec((1, D_MODEL, tm), lambda i: (i // per_b, 0, i % per_b))
        k_shape = jax.ShapeDtypeStruct((n // seq, D_MODEL, seq), F32)
    else:
        k_spec = pl.BlockSpec((tm, D_MODEL), row)
        k_shape = jax.ShapeDtypeStruct((n, D_MODEL), F32)
    return pl.pallas_call(
        functools.partial(_inproj_kernel, k_transposed=k_transposed),
        grid=(n // tm,),
        in_specs=[
            pl.BlockSpec((tm, D_MODEL), row),
            pl.BlockSpec((1, D_MODEL), const),
            pl.BlockSpec((D_MODEL, N_COLGROUPS * D_MODEL), lambda i: (0, 0), pipeline_mode=pl.Buffered(1)),
            pl.BlockSpec((2, D_MODEL), const),
            pl.BlockSpec((1, D_MODEL), const),
            pl.BlockSpec((1, D_MODEL), const),
            pl.BlockSpec((1, MEM_HD), const),
        ],
        out_specs=[
            pl.BlockSpec((tm, N_COLGROUPS * D_MODEL), row),
            pl.BlockSpec((tm, D_MODEL), row),
            k_spec,
            pl.BlockSpec((tm, D_MODEL), row),
        ],
        out_shape=[
            jax.ShapeDtypeStruct((n, N_COLGROUPS * D_MODEL), BF16),
            jax.ShapeDtypeStruct((n, D_MODEL), F32),
            k_shape,
            jax.ShapeDtypeStruct((n, D_MODEL), F32),
        ],
        scratch_shapes=[pltpu.VMEM((tm, D_MODEL), BF16)],
        compiler_params=_params("arbitrary", vmem=VMEM_LIMIT_INPROJ),
        name="in_proj",
    )(x, g_norm.reshape(1, -1), w_in, lb_logits,
      jnp.tile(g_dq, 2 * DF_HEADS).reshape(1, -1), jnp.tile(g_dk, 2 * DF_HEADS).reshape(1, -1), g_mq.reshape(1, -1))


def _split_bf16(x):
    hi = x.astype(BF16)
    return hi, (x - hi.astype(F32)).astype(BF16)


def _hgrn_kernel(q_ref, k_ref, v_ref, z_ref, lf_ref, s0_ref, g_ref, y_ref, s_ref, st_scr, *, seq, tc, hp):
    nb = tc // GLA_BLOCK
    row = lax.broadcasted_iota(jnp.int32, (tc, tc), 0)
    col = lax.broadcasted_iota(jnp.int32, (tc, tc), 1)
    causal = col <= row
    tril = jnp.where(causal, 1.0, 0.0).astype(BF16)
    rowk = lax.broadcasted_iota(jnp.int32, (tc, LANE), 0)

    @pl.when(pl.program_id(1) == 0)
    def _():
        for h in range(hp):
            st_scr[h] = s0_ref[0, h].T

    n_chunks = seq // tc
    unroll = 2 if n_chunks % 2 == 0 else 1

    stacked_att = tc % LANE == 0
    tril2 = jnp.concatenate([tril, tril], axis=1)

    def chunks(c, carry):
        heads = [slice(h * LANE, (h + 1) * LANE) for h in range(hp)]
        for u in range(unroll):
            rows = pl.ds(pl.multiple_of((c * unroll + u) * tc, tc), tc)
            g_hi, g_lo = _split_bf16(lf_ref[rows, :])
            b_all = _dot(tril2, jnp.concatenate([g_hi, g_lo], axis=0))
            bs = [b_all[:, sl] for sl in heads]
            qs = [q_ref[rows, sl].astype(F32) for sl in heads]
            ks = [k_ref[rows, sl].astype(F32) for sl in heads]
            vs = [v_ref[rows, sl] for sl in heads]

            outs = []
            for h in range(hp):
                b, st = bs[h], st_scr[h]
                b_end = b[tc - 1:tc, :]
                outs.append(_dot_nt((qs[h] * jnp.exp(b)).astype(BF16), st.astype(BF16)))
                k_dec = (ks[h] * jnp.exp(b_end - b)).astype(BF16)
                st_scr[h] = jnp.exp(b_end) * st + _dot_tn(vs[h], k_dec)

            atts = []
            for h in range(hp):
                b, q, k = bs[h], qs[h], ks[h]
                starts = [b[n * GLA_BLOCK - 1:n * GLA_BLOCK, :] if n else jnp.zeros((1, LANE), F32) for n in range(nb)]
                k_inv = [(k * jnp.exp(jnp.where(rowk < (n + 1) * GLA_BLOCK, starts[n] - b, 0.0))).astype(BF16)
                         for n in range(nb)]
                if stacked_att:
                    g_rows = jnp.concatenate([jnp.broadcast_to(g, (GLA_BLOCK, LANE)) for g in starts], axis=0)
                    q_dec = (q * jnp.exp(b - g_rows)).astype(BF16)
                    wide = _dot_nt(q_dec, jnp.concatenate(k_inv, axis=0))
                    att = [wide[n * GLA_BLOCK:(n + 1) * GLA_BLOCK, n * tc:(n + 1) * tc] for n in range(nb)]
                else:
                    att = [_dot_nt((q[n * GLA_BLOCK:(n + 1) * GLA_BLOCK]
                                    * jnp.exp(b[n * GLA_BLOCK:(n + 1) * GLA_BLOCK] - starts[n])).astype(BF16), k_inv[n])
                           for n in range(nb)]
                atts.append(jnp.where(causal, jnp.concatenate(att, axis=0), 0.0).astype(BF16))

            for h in range(hp):
                o = outs[h] + _dot(atts[h], vs[h])
                y = _rms(o, g_ref[...]) * z_ref[rows, heads[h]].astype(F32)
                y_ref[rows, heads[h]] = y.astype(BF16)
        return carry

    lax.fori_loop(0, n_chunks // unroll, chunks, 0)

    @pl.when(pl.program_id(1) == pl.num_programs(1) - 1)
    def _():
        for h in range(hp):
            s_ref[0, h] = st_scr[h].T


def _hgrn(p, logf, s0, g_hg_out, batch, seq):
    hp = HG_HEADS
    ts = min(512, seq)
    tc = min(128, ts)
    per_b = seq // ts

    def col(cg):
        return pl.BlockSpec((ts, D_MODEL), lambda b, t: (b * per_b + t, cg))

    st = pl.BlockSpec((1, hp, HG_D, HG_D), lambda b, t: (b, 0, 0, 0))
    return pl.pallas_call(
        functools.partial(_hgrn_kernel, seq=ts, tc=tc, hp=hp),
        grid=(batch, per_b),
        in_specs=[col(CG_HQ), col(CG_HF), col(CG_HI), col(CG_ZA), col(0), st,
                  pl.BlockSpec((1, HG_D), lambda b, t: (0, 0))],
        out_specs=[col(0), st],
        out_shape=[jax.ShapeDtypeStruct((batch * seq, D_MODEL), BF16),
                   jax.ShapeDtypeStruct((batch, HG_HEADS, HG_D, HG_D), F32)],
        scratch_shapes=[pltpu.VMEM((hp, HG_D, HG_D), F32)],
        compiler_params=_params("arbitrary", "arbitrary"),
        name="hgrn",
    )(p, p, p, p, logf, s0, g_hg_out.reshape(1, -1))


def _lambda(lam_ref, lam_init):
    lv = lam_ref[...]
    return (jnp.exp(jnp.sum(lv[0:1] * lv[1:2], axis=-1, keepdims=True))
            - jnp.exp(jnp.sum(lv[2:3] * lv[3:4], axis=-1, keepdims=True)) + lam_init)


def _split_maps(q):
    first = lax.broadcasted_iota(jnp.int32, (1, LANE), 1) < DF_HD
    zero = jnp.zeros((), q.dtype)
    return jnp.concatenate([jnp.where(first, q, zero), jnp.where(first, zero, q)], axis=0)


def _diffattn_kernel(lam_ref, q_ref, k_ref, v_ref, z_ref, g_ref, y_ref, s_scr, p_scr, vt_scr,
                     *, n_q, tq, rc, hp, lam_init):
    w = 2 * tq
    lam = _lambda(lam_ref, lam_init)
    heads = [slice(h * LANE, (h + 1) * LANE) for h in range(hp)]
    for h, sl in enumerate(heads):
        vt_scr[h, 0:DF_VD, :] = v_ref[:, sl].astype(F32).T.astype(BF16)
        vt_scr[h, DF_VD:, :] = jnp.ones((ONES_ROWS, vt_scr.shape[2]), BF16)

    key = lax.broadcasted_iota(jnp.int32, (tq, w), 0)
    qry = lax.broadcasted_iota(jnp.int32, (tq, w), 1) & (tq - 1)
    bias = jnp.where((key // CHUNK) <= (qry // CHUNK), 0.0, -jnp.inf)

    def fold8(x, op):
        return op(x.reshape(rc // 8, 8, w), axis=0)

    def scores(h, i):
        n_vis = (i + 1) * tq
        s_buf = s_scr.at[2 * h + i % 2]
        s = _dot_nt(k_ref[0:n_vis, heads[h]], _split_maps(q_ref[i * tq:(i + 1) * tq, heads[h]]))
        m8 = None
        for c in range(n_vis // rc):
            blk = s[c * rc:(c + 1) * rc]
            if (c + 1) * rc > n_vis - tq:
                blk = blk + bias[c * rc - (n_vis - tq):(c + 1) * rc - (n_vis - tq)]
            s_buf[c * rc:(c + 1) * rc, :] = blk
            m8 = fold8(blk, jnp.max) if m8 is None else jnp.maximum(m8, fold8(blk, jnp.max))
        return jnp.max(m8, axis=0, keepdims=True)

    def probs(h, i, m):
        s_buf, p_buf = s_scr.at[2 * h + i % 2], p_scr.at[2 * h + i % 2]
        for c in range((i + 1) * tq // rc):
            rows = slice(c * rc, (c + 1) * rc)
            p_buf[rows, :] = jnp.exp2(s_buf[rows, :] - m).astype(BF16)

    maxes = {}
    for i in range(min(2, n_q)):
        for h in range(hp):
            maxes[h, i] = scores(h, i)
    for h in range(hp):
        probs(h, 0, maxes[h, 0])
    for i in range(n_q):
        n_vis = (i + 1) * tq
        rows_q = slice(i * tq, (i + 1) * tq)
        for h in range(hp):
            if i + 2 < n_q:
                maxes[h, i + 2] = scores(h, i + 2)
            pv = _dot(vt_scr[h, :, 0:n_vis], p_scr[2 * h + i % 2, 0:n_vis, :])
            if i + 1 < n_q:
                probs(h, i + 1, maxes[h, i + 1])
            rinv = 1.0 / pv[DF_VD:DF_VD + 1, :]
            o = (pv[:DF_VD, :tq] * rinv[:, :tq] - pv[:DF_VD, tq:] * (rinv[:, tq:] * lam)).T
            y = _rms(o, g_ref[...]) * (1.0 - lam_init) * z_ref[rows_q, heads[h]].astype(F32)
            y_ref[rows_q, heads[h]] = y.astype(BF16)


def _diff_attn(lam_vecs, p, g_dsub, *, batch, seq, lam_init):
    tq = rc = 256
    hp = 1
    nhp = DF_HEADS // hp

    def col(cg):
        return pl.BlockSpec((seq, hp * LANE), lambda b, h: (b, cg * nhp + h))

    return pl.pallas_call(
        functools.partial(_diffattn_kernel, n_q=seq // tq, tq=tq, rc=rc, hp=hp, lam_init=lam_init),
        grid=(batch, nhp),
        in_specs=[pl.BlockSpec((4, DF_HD), lambda b, h: (0, 0)),
                  col(CG_DQ), col(CG_DK), col(CG_DV), col(CG_ZB),
                  pl.BlockSpec((1, DF_VD), lambda b, h: (0, 0))],
        out_specs=pl.BlockSpec((seq, hp * LANE), lambda b, h: (b, h)),
        out_shape=jax.ShapeDtypeStruct((batch * seq, D_MODEL), BF16),
        scratch_shapes=[pltpu.VMEM((2 * hp, seq, 2 * tq), F32), pltpu.VMEM((2 * hp, seq, 2 * tq), BF16),
                        pltpu.VMEM((hp, DF_VD + ONES_ROWS, seq), BF16)],
        compiler_params=_params("arbitrary", "arbitrary"),
        name="diff_attn",
    )(lam_vecs, p, p, p, p, g_dsub.reshape(1, -1))


def _cached_diffattn_kernel(lam_ref, q_ref, kn_ref, vn_ref, z_ref, g_ref, kt_ref, vc_ref, y_ref,
                            *, tq, past, lam_init):
    lam = _lambda(lam_ref, lam_init)
    heads = [slice(h * LANE, (h + 1) * LANE) for h in range(DF_HEADS)]
    scores = []
    for sl in heads:
        qs = _split_maps(q_ref[:, sl])
        scores.append((_dot(qs, kt_ref[0, sl, :].astype(BF16)),
                       _dot_nt(qs, kn_ref[:, sl])))
    for h, sl in enumerate(heads):
        s_c, s_n = scores[h]
        m = jnp.maximum(jnp.max(s_c, axis=-1, keepdims=True), jnp.max(s_n, axis=-1, keepdims=True))
        p_c = jnp.exp2(s_c - m)
        p_n = jnp.exp2(s_n - m)
        rinv = 1.0 / (jnp.sum(p_c, axis=-1, keepdims=True) + jnp.sum(p_n, axis=-1, keepdims=True))
        r1 = rinv[:tq]
        r2 = rinv[tq:] * lam
        a_c = (p_c[:tq] * r1 - p_c[tq:] * r2).astype(BF16)
        a_n = (p_n[:tq] * r1 - p_n[tq:] * r2).astype(BF16)
        v_c = vc_ref[0, pl.ds(h, past, stride=DF_HEADS), :].astype(BF16)
        o = _dot(a_c, v_c) + _dot(a_n, vn_ref[:, sl])
        y = _rms(o, g_ref[...]) * (1.0 - lam_init) * z_ref[:, sl].astype(F32)
        y_ref[:, sl] = y.astype(BF16)


def _cached_diff_attn(lam_vecs, p, g_dsub, kt_cache, v_cache, *, batch, seq, lam_init):
    past = v_cache.shape[1]

    def col(cg):
        return pl.BlockSpec((seq, D_MODEL), lambda b: (b, cg))

    return pl.pallas_call(
        functools.partial(_cached_diffattn_kernel, tq=seq, past=past, lam_init=lam_init),
        grid=(batch,),
        in_specs=[pl.BlockSpec((4, DF_HD), lambda b: (0, 0)),
                  col(CG_DQ), col(CG_DK), col(CG_DV), col(CG_ZB),
                  pl.BlockSpec((1, DF_VD), lambda b: (0, 0)),
                  pl.BlockSpec((1, D_MODEL, past), lambda b: (b, 0, 0)),
                  pl.BlockSpec((1, past * DF_HEADS, DF_VD), lambda b: (b, 0, 0))],
        out_specs=pl.BlockSpec((seq, D_MODEL), lambda b: (b, 0)),
        out_shape=jax.ShapeDtypeStruct((batch * seq, D_MODEL), BF16),
        compiler_params=_params("arbitrary"),
        name="cached_diff_attn",
    )(lam_vecs, p, p, p, p, g_dsub.reshape(1, -1), kt_cache, v_cache.reshape(batch, past * DF_HEADS, DF_VD))


def _merge_kernel(x_ref, ya_ref, yb_ref, qm_ref, zm_ref, g0_ref, g1_ref, g2_ref, mk_ref, mv_ref,
                  wb_ref, wo_ref, y_ref):
    heads = [slice(h * MEM_HD, (h + 1) * MEM_HD) for h in range(MEM_HEADS)]
    scores = [_dot_nt(qm_ref[:, sl], mk_ref[:, sl]) for sl in heads]
    hsum = (g0_ref[...].astype(F32) * _dot(ya_ref[...], wb_ref[0])
            + g1_ref[...].astype(F32) * _dot(yb_ref[...], wb_ref[1]))
    ym = []
    for s, sl in zip(scores, heads):
        p = jnp.exp(s - jnp.max(s, axis=-1, keepdims=True))
        rinv = 1.0 / jnp.sum(p, axis=-1, keepdims=True)
        o = _dot(p.astype(BF16), mv_ref[:, sl]) * rinv
        ym.append((o * zm_ref[:, sl].astype(F32)).astype(BF16))
    hsum = hsum + g2_ref[...].astype(F32) * _dot(jnp.concatenate(ym, axis=-1), wb_ref[2])
    y_ref[...] = x_ref[...] + _dot(hsum.astype(BF16), wo_ref[...])


def _merge(x, ya, yb, p, mem_k, mem_v, w_branch, w_out, *, seq, n_mem):
    n = x.shape[0]
    tm = min(256, seq)
    per_b = seq // tm
    row = lambda i: (i, 0)

    def col(cg):
        return pl.BlockSpec((tm, D_MODEL), lambda i: (i, cg))

    mem = pl.BlockSpec((n_mem, D_MODEL), lambda i: (i // per_b, 0))
    return pl.pallas_call(
        _merge_kernel,
        grid=(n // tm,),
        in_specs=[pl.BlockSpec((tm, D_MODEL), row), pl.BlockSpec((tm, D_MODEL), row), pl.BlockSpec((tm, D_MODEL), row),
                  col(CG_MQ), col(CG_ZM), col(CG_G0), col(CG_G0 + 1), col(CG_G0 + 2), mem, mem,
                  pl.BlockSpec((N_BRANCH, D_MODEL, D_MODEL), lambda i: (0, 0, 0), pipeline_mode=pl.Buffered(1)),
                  pl.BlockSpec((D_MODEL, D_MODEL), lambda i: (0, 0), pipeline_mode=pl.Buffered(1))],
        out_specs=pl.BlockSpec((tm, D_MODEL), row),
        out_shape=jax.ShapeDtypeStruct((n, D_MODEL), F32),
        compiler_params=_params("arbitrary"),
        name="merge",
    )(x, ya, yb, p, p, p, p, p, mem_k, mem_v, w_branch, w_out)


def _layer(x, s0, past_k, past_v, mem_k, mem_v, lam_init, w, n_mem):
    batch, seq, _ = x.shape
    n = batch * seq
    xf = x.reshape(n, D_MODEL)
    prompt = past_k is None
    p, logf, k32, v32 = _in_proj(xf, w["g_norm"], w["w_in"], w["lb_logits"], w["g_dq"], w["g_dk"], w["g_mq"],
                                 seq=seq, k_transposed=prompt)
    ya, s_new = _hgrn(p, logf, s0, w["g_hg_out"], batch, seq)
    if prompt:
        yb = _diff_attn(w["lam_vecs"], p, w["g_dsub"], batch=batch, seq=seq, lam_init=lam_init)
    else:
        past = past_k.shape[1]
        kt_cache = jnp.transpose(past_k, (0, 2, 3, 4, 1)).reshape(batch, D_MODEL, past)
        yb = _cached_diff_attn(w["lam_vecs"], p, w["g_dsub"], kt_cache, past_v, batch=batch, seq=seq,
                               lam_init=lam_init)
    y = _merge(xf, ya, yb, p, mem_k, mem_v, w["w_branch"], w["w_out"], seq=seq, n_mem=n_mem)
    if prompt:
        k_out = jnp.transpose(k32.reshape(batch, DF_HEADS, 2, DF_HD, seq), (0, 4, 1, 2, 3))
    else:
        k_out = k32.reshape(batch, seq, DF_HEADS, 2, DF_HD)
    return y.reshape(batch, seq, D_MODEL), s_new, k_out, v32.reshape(batch, seq, DF_HEADS, DF_VD)


def kernel(x_prompt, x_sample, mem_prompt, cache_diff_k, cache_diff_v, cache_mem_k, cache_mem_v, state_hgrn,
           g_norm, w_in, hg_lb_logits, g_hg_out, g_dq, g_dk, lam_q1, lam_k1, lam_q2, lam_k2, g_dsub, g_mem,
           w_mkv, g_mq, g_mk, w_branch, w_out):
    depth = g_norm.shape[0]
    assert depth == 1 and hg_lb_logits.shape[0] == 2, "single-layer step only"
    batch, n_mem = mem_prompt.shape[:2]
    lam_init = 0.8 - 0.6
    w = dict(g_norm=g_norm[0], w_in=w_in[0].astype(BF16), lb_logits=hg_lb_logits, g_hg_out=g_hg_out[0],
             g_dq=g_dq[0], g_dk=g_dk[0], g_mq=g_mq[0], g_dsub=g_dsub[0],
             lam_vecs=jnp.stack([lam_q1[0], lam_k1[0], lam_q2[0], lam_k2[0]]),
             w_branch=w_branch[0].astype(BF16), w_out=w_out[0].astype(BF16))

    mem_k, mem_v, mem_kb, mem_vb = _memory_kv(mem_prompt.reshape(batch * n_mem, D_MODEL), g_mem[0], w_mkv[0], g_mk[0],
                                              batch=batch, n_mem=n_mem)
    s0 = jnp.zeros((batch, HG_HEADS, HG_D, HG_D), F32)
    y_p, s_p, k_p, v_p = _layer(x_prompt, s0, None, None, mem_kb, mem_vb, lam_init, w, n_mem)
    y_s, s_s, k_s, v_s = _layer(x_sample, state_hgrn[0], cache_diff_k[0], cache_diff_v[0],
                                cache_mem_k[0].reshape(-1, D_MODEL).astype(BF16),
                                cache_mem_v[0].reshape(-1, D_MODEL).astype(BF16), lam_init, w, n_mem)
    return (y_p, y_s, s_p[None], s_s[None], k_p[None], v_p[None], k_s[None], v_s[None], mem_k[None], mem_v[None])
```

```python
import functools

import jax
import jax.numpy as jnp
from jax import lax
from jax.experimental import pallas as pl
from jax.experimental.pallas import tpu as pltpu

F32 = jnp.float32
BF16 = jnp.bfloat16

EPS = 1e-6
LOG2E = 1.4426950408889634
D_MODEL = 1024
CHUNK = 64
GLA_BLOCK = 16
HG_HEADS, HG_D = 8, 128
DF_HEADS, DF_HD, DF_VD = 8, 64, 128
MEM_HEADS, MEM_HD = 4, 256
N_BRANCH = 3
N_COLGROUPS = 13
(CG_HQ, CG_HF, CG_HI, CG_ZA, CG_DQ, CG_DK, CG_DV, CG_ZB, CG_MQ, CG_ZM, CG_G0) = range(11)

LANE = 128
ONES_ROWS = 16
VMEM_LIMIT = 52 * 1024 * 1024
VMEM_LIMIT_INPROJ = 60 * 1024 * 1024

NT = (((1,), (1,)), ((), ()))
TN = (((0,), (0,)), ((), ()))


def _dot(a, b):
    return jnp.dot(a, b, preferred_element_type=F32)


def _dot_nt(a, b):
    return lax.dot_general(a, b, NT, preferred_element_type=F32)


def _dot_tn(a, b):
    return lax.dot_general(a, b, TN, preferred_element_type=F32)


def _sigmoid(x):
    return 0.5 * jnp.tanh(0.5 * x) + 0.5


def _rms(x, g):
    return x * lax.rsqrt(jnp.mean(x * x, axis=-1, keepdims=True) + EPS) * g


def _params(*sem, vmem=VMEM_LIMIT):
    return pltpu.CompilerParams(dimension_semantics=sem, vmem_limit_bytes=vmem)


def _memkv_kernel(mem_ref, g_ref, w_ref, gk_ref, k_ref, v_ref, kb_ref, vb_ref):
    xn = _rms(mem_ref[...], g_ref[...]).astype(BF16)
    kv = _dot(xn, w_ref[...])
    for h in range(MEM_HEADS):
        sl = slice(h * MEM_HD, (h + 1) * MEM_HD)
        k = _rms(kv[:, sl], gk_ref[...])
        k_ref[0, :, h, :] = k
        kb_ref[:, sl] = k.astype(BF16)
        v_ref[0, :, h, :] = kv[:, D_MODEL + h * MEM_HD:D_MODEL + (h + 1) * MEM_HD]
    vb_ref[...] = kv[:, D_MODEL:].astype(BF16)


def _memory_kv(mem, g_mem, w_mkv, g_mk, *, batch, n_mem):
    out = pl.BlockSpec((1, n_mem, MEM_HEADS, MEM_HD), lambda i: (i, 0, 0, 0))
    flat = pl.BlockSpec((n_mem, D_MODEL), lambda i: (i, 0))
    return pl.pallas_call(
        _memkv_kernel,
        grid=(batch,),
        in_specs=[
            pl.BlockSpec((n_mem, D_MODEL), lambda i: (i, 0)),
            pl.BlockSpec((1, D_MODEL), lambda i: (0, 0)),
            pl.BlockSpec((D_MODEL, 2 * D_MODEL), lambda i: (0, 0)),
            pl.BlockSpec((1, MEM_HD), lambda i: (0, 0)),
        ],
        out_specs=[out, out, flat, flat],
        out_shape=[jax.ShapeDtypeStruct((batch, n_mem, MEM_HEADS, MEM_HD), F32)] * 2
        + [jax.ShapeDtypeStruct((batch * n_mem, D_MODEL), BF16)] * 2,
        compiler_params=_params("arbitrary"),
        name="memory_kv",
    )(mem, g_mem.reshape(1, -1), w_mkv.astype(BF16), g_mk.reshape(1, -1))


def _half_head_rms(acc, g_ref, scale):
    lane = lax.broadcasted_iota(jnp.int32, (1, LANE), 1)
    lo = lane < DF_HD
    outs = []
    for h in range(DF_HEADS):
        sl = slice(h * LANE, (h + 1) * LANE)
        xh = acc[:, sl]
        sq = xh * xh
        s_lo = jnp.sum(jnp.where(lo, sq, 0.0), axis=-1, keepdims=True)
        s_hi = jnp.sum(jnp.where(lo, 0.0, sq), axis=-1, keepdims=True)
        r = jnp.where(lo, lax.rsqrt(s_lo * (1.0 / DF_HD) + EPS), lax.rsqrt(s_hi * (1.0 / DF_HD) + EPS))
        outs.append(xh * r * (g_ref[:, sl] * scale))
    return outs


def _inproj_kernel(x_ref, gn_ref, w_ref, lbl_ref, gq_ref, gk_ref, gmq_ref,
                   p_ref, logf_ref, k_ref, v_ref, xn_scr, *, k_transposed):
    xn_scr[...] = _rms(x_ref[...], gn_ref[...]).astype(BF16)

    def proj(j):
        return _dot(xn_scr[...], w_ref[:, j * D_MODEL:(j + 1) * D_MODEL])

    def put(j, val):
        p_ref[:, j * D_MODEL:(j + 1) * D_MODEL] = val.astype(BF16)

    put(CG_HQ, proj(CG_HQ))

    l = lbl_ref[...]
    e = jnp.exp(l - jnp.max(l, axis=0, keepdims=True))
    lb = e[0:1] / jnp.sum(e, axis=0, keepdims=True)
    sp = _sigmoid(proj(CG_HF))
    logf_ref[...] = jnp.log(lb + (1.0 - lb) * sp)
    put(CG_HF, (1.0 - lb) * (1.0 - sp))

    put(CG_HI, proj(CG_HI))

    for j in (CG_ZA, CG_ZB, CG_ZM):
        acc = proj(j)
        put(j, acc * _sigmoid(acc))

    outs = _half_head_rms(proj(CG_DQ), gq_ref, DF_HD ** -0.5 * LOG2E)
    for h in range(DF_HEADS):
        p_ref[:, CG_DQ * D_MODEL + h * LANE:CG_DQ * D_MODEL + (h + 1) * LANE] = outs[h].astype(BF16)

    outs = _half_head_rms(proj(CG_DK), gk_ref, 1.0)
    for h in range(DF_HEADS):
        sl = slice(h * LANE, (h + 1) * LANE)
        p_ref[:, CG_DK * D_MODEL + h * LANE:CG_DK * D_MODEL + (h + 1) * LANE] = outs[h].astype(BF16)
        if k_transposed:
            k_ref[0, sl, :] = outs[h].T
        else:
            k_ref[:, sl] = outs[h]

    acc = proj(CG_DV)
    put(CG_DV, acc)
    v_ref[...] = acc

    acc = proj(CG_MQ)
    for h in range(MEM_HEADS):
        sl = slice(h * MEM_HD, (h + 1) * MEM_HD)
        p_ref[:, CG_MQ * D_MODEL + h * MEM_HD:CG_MQ * D_MODEL + (h + 1) * MEM_HD] = (
            _rms(acc[:, sl], gmq_ref[...]) * (MEM_HD ** -0.5 * LOG2E)).astype(BF16)

    for j in range(CG_G0, N_COLGROUPS):
        put(j, _sigmoid(proj(j)))


def _in_proj(x, g_norm, w_in, lb_logits, g_dq, g_dk, g_mq, *, seq, k_transposed):
    n = x.shape[0]
    tm = min(256, seq) if k_transposed else min(256, n)
    row = lambda i: (i, 0)
    const = lambda i: (0, 0)
    if k_transposed:
        per_b = seq // tm
        k_spec = pl.BlockSpec((1, D_MODEL, tm), lambda i: (i // per_b, 0, i % per_b))
        k_shape = jax.ShapeDtypeStruct((n // seq, D_MODEL, seq), F32)
    else:
        k_spec = pl.BlockSpec((tm, D_MODEL), row)
        k_shape = jax.ShapeDtypeStruct((n, D_MODEL), F32)
    return pl.pallas_call(
        functools.partial(_inproj_kernel, k_transposed=k_transposed),
        grid=(n // tm,),
        in_specs=[
            pl.BlockSpec((tm, D_MODEL), row),
            pl.BlockSpec((1, D_MODEL), const),
            pl.BlockSpec((D_MODEL, N_COLGROUPS * D_MODEL), lambda i: (0, 0), pipeline_mode=pl.Buffered(1)),
            pl.BlockSpec((2, D_MODEL), const),
            pl.BlockSpec((1, D_MODEL), const),
            pl.BlockSpec((1, D_MODEL), const),
            pl.BlockSpec((1, MEM_HD), const),
        ],
        out_specs=[
            pl.BlockSpec((tm, N_COLGROUPS * D_MODEL), row),
            pl.BlockSpec((tm, D_MODEL), row),
            k_spec,
            pl.BlockSpec((tm, D_MODEL), row),
        ],
        out_shape=[
            jax.ShapeDtypeStruct((n, N_COLGROUPS * D_MODEL), BF16),
            jax.ShapeDtypeStruct((n, D_MODEL), F32),
            k_shape,
            jax.ShapeDtypeStruct((n, D_MODEL), F32),
        ],
        scratch_shapes=[pltpu.VMEM((tm, D_MODEL), BF16)],
        compiler_params=_params("arbitrary", vmem=VMEM_LIMIT_INPROJ),
        name="in_proj",
    )(x, g_norm.reshape(1, -1), w_in, lb_logits,
      jnp.tile(g_dq, 2 * DF_HEADS).reshape(1, -1), jnp.tile(g_dk, 2 * DF_HEADS).reshape(1, -1), g_mq.reshape(1, -1))


def _split_bf16(x):
    hi = x.astype(BF16)
    return hi, (x - hi.astype(F32)).astype(BF16)


def _hgrn_kernel(q_ref, k_ref, v_ref, z_ref, lf_ref, s0_ref, g_ref, y_ref, s_ref, st_scr, *, seq, tc, hp):
    nb = tc // GLA_BLOCK
    row = lax.broadcasted_iota(jnp.int32, (tc, tc), 0)
    col = lax.broadcasted_iota(jnp.int32, (tc, tc), 1)
    causal = col <= row
    tril = jnp.where(causal, 1.0, 0.0).astype(BF16)
    rowk = lax.broadcasted_iota(jnp.int32, (tc, LANE), 0)

    @pl.when(pl.program_id(1) == 0)
    def _():
        for h in range(hp):
            st_scr[h] = s0_ref[0, h].T

    n_chunks = seq // tc
    unroll = 4 if n_chunks % 4 == 0 else 1

    stacked_att = tc % LANE == 0
    tril2 = jnp.concatenate([tril, tril], axis=1)

    def chunks(c, carry):
        heads = [slice(h * LANE, (h + 1) * LANE) for h in range(hp)]
        for u in range(unroll):
            rows = pl.ds(pl.multiple_of((c * unroll + u) * tc, tc), tc)
            g_hi, g_lo = _split_bf16(lf_ref[rows, :])
            b_all = _dot(tril2, jnp.concatenate([g_hi, g_lo], axis=0))
            bs = [b_all[:, sl] for sl in heads]
            qs = [q_ref[rows, sl].astype(F32) for sl in heads]
            ks = [k_ref[rows, sl].astype(F32) for sl in heads]
            vs = [v_ref[rows, sl] for sl in heads]

            q_ins, sv_t = [], []
            for h in range(hp):
                b, st = bs[h], st_scr[h]
                b_end = b[tc - 1:tc, :]
                v_t = vs[h].astype(F32).T.astype(BF16)
                q_ins.append((qs[h] * jnp.exp(b)).astype(BF16))
                sv_t.append(jnp.concatenate([st.astype(BF16), v_t], axis=1))
                k_dec = (ks[h] * jnp.exp(b_end - b)).astype(BF16)
                st_scr[h] = jnp.exp(b_end) * st + _dot(v_t, k_dec)

            atts = []
            for h in range(hp):
                b, q, k = bs[h], qs[h], ks[h]
                starts = [b[n * GLA_BLOCK - 1:n * GLA_BLOCK, :] if n else jnp.zeros((1, LANE), F32) for n in range(nb)]
                k_inv = [(k * jnp.exp(jnp.where(rowk < (n + 1) * GLA_BLOCK, starts[n] - b, 0.0))).astype(BF16)
                         for n in range(nb)]
                if stacked_att:
                    g_rows = jnp.concatenate([jnp.broadcast_to(g, (GLA_BLOCK, LANE)) for g in starts], axis=0)
                    q_dec = (q * jnp.exp(b - g_rows)).astype(BF16)
                    wide = _dot_nt(q_dec, jnp.concatenate(k_inv, axis=0))
                    att = [wide[n * GLA_BLOCK:(n + 1) * GLA_BLOCK, n * tc:(n + 1) * tc] for n in range(nb)]
                else:
                    att = [_dot_nt((q[n * GLA_BLOCK:(n + 1) * GLA_BLOCK]
                                    * jnp.exp(b[n * GLA_BLOCK:(n + 1) * GLA_BLOCK] - starts[n])).astype(BF16), k_inv[n])
                           for n in range(nb)]
                atts.append(jnp.where(causal, jnp.concatenate(att, axis=0), 0.0).astype(BF16))

            for h in range(hp):
                o = _dot_nt(jnp.concatenate([q_ins[h], atts[h]], axis=1), sv_t[h])
                y = _rms(o, g_ref[...]) * z_ref[rows, heads[h]].astype(F32)
                y_ref[rows, heads[h]] = y.astype(BF16)
        return carry

    lax.fori_loop(0, n_chunks // unroll, chunks, 0)

    @pl.when(pl.program_id(1) == pl.num_programs(1) - 1)
    def _():
        for h in range(hp):
            s_ref[0, h] = st_scr[h].T


def _hgrn(p, logf, s0, g_hg_out, batch, seq):
    hp = HG_HEADS
    ts = min(512, seq)
    tc = min(128, ts)
    per_b = seq // ts

    def col(cg):
        return pl.BlockSpec((ts, D_MODEL), lambda b, t: (b * per_b + t, cg))

    st = pl.BlockSpec((1, hp, HG_D, HG_D), lambda b, t: (b, 0, 0, 0))
    return pl.pallas_call(
        functools.partial(_hgrn_kernel, seq=ts, tc=tc, hp=hp),
        grid=(batch, per_b),
        in_specs=[col(CG_HQ), col(CG_HF), col(CG_HI), col(CG_ZA), col(0), st,
                  pl.BlockSpec((1, HG_D), lambda b, t: (0, 0))],
        out_specs=[col(0), st],
        out_shape=[jax.ShapeDtypeStruct((batch * seq, D_MODEL), BF16),
                   jax.ShapeDtypeStruct((batch, HG_HEADS, HG_D, HG_D), F32)],
        scratch_shapes=[pltpu.VMEM((hp, HG_D, HG_D), F32)],
        compiler_params=_params("arbitrary", "arbitrary"),
        name="hgrn",
    )(p, p, p, p, logf, s0, g_hg_out.reshape(1, -1))


def _lambda(lam_ref, lam_init):
    lv = lam_ref[...]
    return (jnp.exp(jnp.sum(lv[0:1] * lv[1:2], axis=-1, keepdims=True))
            - jnp.exp(jnp.sum(lv[2:3] * lv[3:4], axis=-1, keepdims=True)) + lam_init)


def _split_maps(q):
    first = lax.broadcasted_iota(jnp.int32, (1, LANE), 1) < DF_HD
    zero = jnp.zeros((), q.dtype)
    return jnp.concatenate([jnp.where(first, q, zero), jnp.where(first, zero, q)], axis=0)


def _diffattn_kernel(lam_ref, q_ref, k_ref, v_ref, z_ref, g_ref, y_ref, s_scr, p_scr, vt_scr,
                     *, n_q, tq, rc, hp, lam_init):
    w = 2 * tq
    lam = _lambda(lam_ref, lam_init)
    heads = [slice(h * LANE, (h + 1) * LANE) for h in range(hp)]
    for h, sl in enumerate(heads):
        vt_scr[h, 0:DF_VD, :] = v_ref[:, sl].astype(F32).T.astype(BF16)
        vt_scr[h, DF_VD:, :] = jnp.ones((ONES_ROWS, vt_scr.shape[2]), BF16)

    key = lax.broadcasted_iota(jnp.int32, (tq, w), 0)
    qry = lax.broadcasted_iota(jnp.int32, (tq, w), 1) & (tq - 1)
    bias = jnp.where((key // CHUNK) <= (qry // CHUNK), 0.0, -jnp.inf)

    def fold8(x, op):
        return op(x.reshape(rc // 8, 8, w), axis=0)

    def scores(h, i):
        n_vis = (i + 1) * tq
        s_buf = s_scr.at[2 * h + i % 2]
        s = _dot_nt(k_ref[0:n_vis, heads[h]], _split_maps(q_ref[i * tq:(i + 1) * tq, heads[h]]))
        m8 = None
        for c in range(n_vis // rc):
            blk = s[c * rc:(c + 1) * rc]
            if (c + 1) * rc > n_vis - tq:
                blk = blk + bias[c * rc - (n_vis - tq):(c + 1) * rc - (n_vis - tq)]
            s_buf[c * rc:(c + 1) * rc, :] = blk
            m8 = fold8(blk, jnp.max) if m8 is None else jnp.maximum(m8, fold8(blk, jnp.max))
        return jnp.max(m8, axis=0, keepdims=True)

    def probs(h, i, m):
        s_buf, p_buf = s_scr.at[2 * h + i % 2], p_scr.at[2 * h + i % 2]
        for c in range((i + 1) * tq // rc):
            rows = slice(c * rc, (c + 1) * rc)
            p_buf[rows, :] = jnp.exp2(s_buf[rows, :] - m).astype(BF16)

    maxes = {}
    for i in range(min(2, n_q)):
        for h in range(hp):
            maxes[h, i] = scores(h, i)
    for h in range(hp):
        probs(h, 0, maxes[h, 0])
    for i in range(n_q):
        n_vis = (i + 1) * tq
        rows_q = slice(i * tq, (i + 1) * tq)
        for h in range(hp):
            if i + 2 < n_q:
                maxes[h, i + 2] = scores(h, i + 2)
            pv = _dot(vt_scr[h, :, 0:n_vis], p_scr[2 * h + i % 2, 0:n_vis, :])
            if i + 1 < n_q:
                probs(h, i + 1, maxes[h, i + 1])
            rinv = 1.0 / pv[DF_VD:DF_VD + 1, :]
            o = (pv[:DF_VD, :tq] * rinv[:, :tq] - pv[:DF_VD, tq:] * (rinv[:, tq:] * lam)).T
            y = _rms(o, g_ref[...]) * (1.0 - lam_init) * z_ref[rows_q, heads[h]].astype(F32)
            y_ref[rows_q, heads[h]] = y.astype(BF16)


def _diff_attn(lam_vecs, p, g_dsub, *, batch, seq, lam_init):
    tq = rc = 256
    hp = 1
    nhp = DF_HEADS // hp

    def col(cg):
        return pl.BlockSpec((seq, hp * LANE), lambda b, h: (b, cg * nhp + h))

    return pl.pallas_call(
        functools.partial(_diffattn_kernel, n_q=seq // tq, tq=tq, rc=rc, hp=hp, lam_init=lam_init),
        grid=(batch, nhp),
        in_specs=[pl.BlockSpec((4, DF_HD), lambda b, h: (0, 0)),
                  col(CG_DQ), col(CG_DK), col(CG_DV), col(CG_ZB),
                  pl.BlockSpec((1, DF_VD), lambda b, h: (0, 0))],
        out_specs=pl.BlockSpec((seq, hp * LANE), lambda b, h: (b, h)),
        out_shape=jax.ShapeDtypeStruct((batch * seq, D_MODEL), BF16),
        scratch_shapes=[pltpu.VMEM((2 * hp, seq, 2 * tq), F32), pltpu.VMEM((2 * hp, seq, 2 * tq), BF16),
                        pltpu.VMEM((hp, DF_VD + ONES_ROWS, seq), BF16)],
        compiler_params=_params("arbitrary", "arbitrary"),
        name="diff_attn",
    )(lam_vecs, p, p, p, p, g_dsub.reshape(1, -1))


def _cached_diffattn_kernel(lam_ref, q_ref, kn_ref, vn_ref, z_ref, g_ref, kt_ref, vc_ref, y_ref,
                            *, tq, past, lam_init):
    lam = _lambda(lam_ref, lam_init)
    heads = [slice(h * LANE, (h + 1) * LANE) for h in range(DF_HEADS)]
    scores = []
    for sl in heads:
        qs = _split_maps(q_ref[:, sl])
        scores.append((_dot(qs, kt_ref[0, sl, :].astype(BF16)),
                       _dot_nt(qs, kn_ref[:, sl])))
    for h, sl in enumerate(heads):
        s_c, s_n = scores[h]
        m = jnp.maximum(jnp.max(s_c, axis=-1, keepdims=True), jnp.max(s_n, axis=-1, keepdims=True))
        p_c = jnp.exp2(s_c - m)
        p_n = jnp.exp2(s_n - m)
        rinv = 1.0 / (jnp.sum(p_c, axis=-1, keepdims=True) + jnp.sum(p_n, axis=-1, keepdims=True))
        r1 = rinv[:tq]
        r2 = rinv[tq:] * lam
        a_c = (p_c[:tq] * r1 - p_c[tq:] * r2).astype(BF16)
        a_n = (p_n[:tq] * r1 - p_n[tq:] * r2).astype(BF16)
        v_c = vc_ref[0, pl.ds(h, past, stride=DF_HEADS), :].astype(BF16)
        o = _dot(a_c, v_c) + _dot(a_n, vn_ref[:, sl])
        y = _rms(o, g_ref[...]) * (1.0 - lam_init) * z_ref[:, sl].astype(F32)
        y_ref[:, sl] = y.astype(BF16)


def _cached_diff_attn(lam_vecs, p, g_dsub, kt_cache, v_cache, *, batch, seq, lam_init):
    past = v_cache.shape[1]

    def col(cg):
        return pl.BlockSpec((seq, D_MODEL), lambda b: (b, cg))

    return pl.pallas_call(
        functools.partial(_cached_diffattn_kernel, tq=seq, past=past, lam_init=lam_init),
        grid=(batch,),
        in_specs=[pl.BlockSpec((4, DF_HD), lambda b: (0, 0)),
                  col(CG_DQ), col(CG_DK), col(CG_DV), col(CG_ZB),
                  pl.BlockSpec((1, DF_VD), lambda b: (0, 0)),
                  pl.BlockSpec((1, D_MODEL, past), lambda b: (b, 0, 0)),
                  pl.BlockSpec((1, past * DF_HEADS, DF_VD), lambda b: (b, 0, 0))],
        out_specs=pl.BlockSpec((seq, D_MODEL), lambda b: (b, 0)),
        out_shape=jax.ShapeDtypeStruct((batch * seq, D_MODEL), BF16),
        compiler_params=_params("arbitrary"),
        name="cached_diff_attn",
    )(lam_vecs, p, p, p, p, g_dsub.reshape(1, -1), kt_cache, v_cache.reshape(batch, past * DF_HEADS, DF_VD))


def _merge_kernel(x_ref, ya_ref, yb_ref, qm_ref, zm_ref, g0_ref, g1_ref, g2_ref, mk_ref, mv_ref,
                  wb_ref, wo_ref, y_ref):
    heads = [slice(h * MEM_HD, (h + 1) * MEM_HD) for h in range(MEM_HEADS)]
    scores = [_dot_nt(qm_ref[:, sl], mk_ref[:, sl]) for sl in heads]
    hsum = (g0_ref[...].astype(F32) * _dot(ya_ref[...], wb_ref[0])
            + g1_ref[...].astype(F32) * _dot(yb_ref[...], wb_ref[1]))
    ym = []
    for s, sl in zip(scores, heads):
        p = jnp.exp2(s - jnp.max(s, axis=-1, keepdims=True))
        rinv = 1.0 / jnp.sum(p, axis=-1, keepdims=True)
        o = _dot(p.astype(BF16), mv_ref[:, sl]) * rinv
        ym.append((o * zm_ref[:, sl].astype(F32)).astype(BF16))
    hsum = hsum + g2_ref[...].astype(F32) * _dot(jnp.concatenate(ym, axis=-1), wb_ref[2])
    y_ref[...] = x_ref[...] + _dot(hsum.astype(BF16), wo_ref[...])


def _merge(x, ya, yb, p, mem_k, mem_v, w_branch, w_out, *, seq, n_mem):
    n = x.shape[0]
    tm = min(256, seq)
    per_b = seq // tm
    row = lambda i: (i, 0)

    def col(cg):
        return pl.BlockSpec((tm, D_MODEL), lambda i: (i, cg))

    mem = pl.BlockSpec((n_mem, D_MODEL), lambda i: (i // per_b, 0))
    return pl.pallas_call(
        _merge_kernel,
        grid=(n // tm,),
        in_specs=[pl.BlockSpec((tm, D_MODEL), row), pl.BlockSpec((tm, D_MODEL), row), pl.BlockSpec((tm, D_MODEL), row),
                  col(CG_MQ), col(CG_ZM), col(CG_G0), col(CG_G0 + 1), col(CG_G0 + 2), mem, mem,
                  pl.BlockSpec((N_BRANCH, D_MODEL, D_MODEL), lambda i: (0, 0, 0), pipeline_mode=pl.Buffered(1)),
                  pl.BlockSpec((D_MODEL, D_MODEL), lambda i: (0, 0), pipeline_mode=pl.Buffered(1))],
        out_specs=pl.BlockSpec((tm, D_MODEL), row),
        out_shape=jax.ShapeDtypeStruct((n, D_MODEL), F32),
        compiler_params=_params("arbitrary"),
        name="merge",
    )(x, ya, yb, p, p, p, p, p, mem_k, mem_v, w_branch, w_out)


def _layer(x, s0, past_k, past_v, mem_k, mem_v, lam_init, w, n_mem):
    batch, seq, _ = x.shape
    n = batch * seq
    xf = x.reshape(n, D_MODEL)
    prompt = past_k is None
    p, logf, k32, v32 = _in_proj(xf, w["g_norm"], w["w_in"], w["lb_logits"], w["g_dq"], w["g_dk"], w["g_mq"],
                                 seq=seq, k_transposed=prompt)
    ya, s_new = _hgrn(p, logf, s0, w["g_hg_out"], batch, seq)
    if prompt:
        yb = _diff_attn(w["lam_vecs"], p, w["g_dsub"], batch=batch, seq=seq, lam_init=lam_init)
    else:
        past = past_k.shape[1]
        kt_cache = jnp.transpose(past_k, (0, 2, 3, 4, 1)).reshape(batch, D_MODEL, past)
        yb = _cached_diff_attn(w["lam_vecs"], p, w["g_dsub"], kt_cache, past_v, batch=batch, seq=seq,
                               lam_init=lam_init)
    y = _merge(xf, ya, yb, p, mem_k, mem_v, w["w_branch"], w["w_out"], seq=seq, n_mem=n_mem)
    if prompt:
        k_out = jnp.transpose(k32.reshape(batch, DF_HEADS, 2, DF_HD, seq), (0, 4, 1, 2, 3))
    else:
        k_out = k32.reshape(batch, seq, DF_HEADS, 2, DF_HD)
    return y.reshape(batch, seq, D_MODEL), s_new, k_out, v32.reshape(batch, seq, DF_HEADS, DF_VD)


def kernel(x_prompt, x_sample, mem_prompt, cache_diff_k, cache_diff_v, cache_mem_k, cache_mem_v, state_hgrn,
           g_norm, w_in, hg_lb_logits, g_hg_out, g_dq, g_dk, lam_q1, lam_k1, lam_q2, lam_k2, g_dsub, g_mem,
           w_mkv, g_mq, g_mk, w_branch, w_out):
    depth = g_norm.shape[0]
    assert depth == 1 and hg_lb_logits.shape[0] == 2, "single-layer step only"
    batch, n_mem = mem_prompt.shape[:2]
    lam_init = 0.8 - 0.6
    w = dict(g_norm=g_norm[0], w_in=w_in[0].astype(BF16), lb_logits=hg_lb_logits, g_hg_out=g_hg_out[0],
             g_dq=g_dq[0], g_dk=g_dk[0], g_mq=g_mq[0], g_dsub=g_dsub[0],
             lam_vecs=jnp.stack([lam_q1[0], lam_k1[0], lam_q2[0], lam_k2[0]]),
             w_branch=w_branch[0].astype(BF16), w_out=w_out[0].astype(BF16))

    mem_k, mem_v, mem_kb, mem_vb = _memory_kv(mem_prompt.reshape(batch * n_mem, D_MODEL), g_mem[0], w_mkv[0], g_mk[0],
                                              batch=batch, n_mem=n_mem)
    s0 = jnp.zeros((batch, HG_HEADS, HG_D, HG_D), F32)
    y_p, s_p, k_p, v_p = _layer(x_prompt, s0, None, None, mem_kb, mem_vb, lam_init, w, n_mem)
    y_s, s_s, k_s, v_s = _layer(x_sample, state_hgrn[0], cache_diff_k[0], cache_diff_v[0],
                                cache_mem_k[0].reshape(-1, D_MODEL).astype(BF16),
                                cache_mem_v[0].reshape(-1, D_MODEL).astype(BF16), lam_init, w, n_mem)
    return (y_p, y_s, s_p[None], s_s[None], k_p[None], v_p[None], k_s[None], v_s[None], mem_k[None], mem_v[None])
```

```python
import functools

import jax
import jax.numpy as jnp
from jax import lax
from jax.experimental import pallas as pl
from jax.experimental.pallas import tpu as pltpu

F32 = jnp.float32
BF16 = jnp.bfloat16

EPS = 1e-6
LOG2E = 1.4426950408889634
D_MODEL = 1024
CHUNK = 64
GLA_BLOCK = 16
HG_HEADS, HG_D = 8, 128
DF_HEADS, DF_HD, DF_VD = 8, 64, 128
MEM_HEADS, MEM_HD = 4, 256
N_BRANCH = 3
N_COLGROUPS = 13
(CG_HQ, CG_HF, CG_HI, CG_ZA, CG_DQ, CG_DK, CG_DV, CG_ZB, CG_MQ, CG_ZM, CG_G0) = range(11)

LANE = 128
ONES_ROWS = 16
VMEM_LIMIT = 52 * 1024 * 1024
VMEM_LIMIT_INPROJ = 60 * 1024 * 1024

NT = (((1,), (1,)), ((), ()))
TN = (((0,), (0,)), ((), ()))


def _dot(a, b):
    return jnp.dot(a, b, preferred_element_type=F32)


def _dot_nt(a, b):
    return lax.dot_general(a, b, NT, preferred_element_type=F32)


def _dot_tn(a, b):
    return lax.dot_general(a, b, TN, preferred_element_type=F32)


def _sigmoid(x):
    return 0.5 * jnp.tanh(0.5 * x) + 0.5


def _rms(x, g):
    return x * lax.rsqrt(jnp.mean(x * x, axis=-1, keepdims=True) + EPS) * g


def _params(*sem, vmem=VMEM_LIMIT):
    return pltpu.CompilerParams(dimension_semantics=sem, vmem_limit_bytes=vmem)


def _memkv_kernel(mem_ref, g_ref, w_ref, gk_ref, k_ref, v_ref, kb_ref, vb_ref):
    xn = _rms(mem_ref[...], g_ref[...]).astype(BF16)
    kv = _dot(xn, w_ref[...])
    for h in range(MEM_HEADS):
        sl = slice(h * MEM_HD, (h + 1) * MEM_HD)
        k = _rms(kv[:, sl], gk_ref[...])
        k_ref[0, :, h, :] = k
        kb_ref[:, sl] = k.astype(BF16)
        v_ref[0, :, h, :] = kv[:, D_MODEL + h * MEM_HD:D_MODEL + (h + 1) * MEM_HD]
    vb_ref[...] = kv[:, D_MODEL:].astype(BF16)


def _memory_kv(mem, g_mem, w_mkv, g_mk, *, batch, n_mem):
    out = pl.BlockSpec((1, n_mem, MEM_HEADS, MEM_HD), lambda i: (i, 0, 0, 0))
    flat = pl.BlockSpec((n_mem, D_MODEL), lambda i: (i, 0))
    return pl.pallas_call(
        _memkv_kernel,
        grid=(batch,),
        in_specs=[
            pl.BlockSpec((n_mem, D_MODEL), lambda i: (i, 0)),
            pl.BlockSpec((1, D_MODEL), lambda i: (0, 0)),
            pl.BlockSpec((D_MODEL, 2 * D_MODEL), lambda i: (0, 0)),
            pl.BlockSpec((1, MEM_HD), lambda i: (0, 0)),
        ],
        out_specs=[out, out, flat, flat],
        out_shape=[jax.ShapeDtypeStruct((batch, n_mem, MEM_HEADS, MEM_HD), F32)] * 2
        + [jax.ShapeDtypeStruct((batch * n_mem, D_MODEL), BF16)] * 2,
        compiler_params=_params("arbitrary"),
        name="memory_kv",
    )(mem, g_mem.reshape(1, -1), w_mkv.astype(BF16), g_mk.reshape(1, -1))


def _half_head_rms(acc, g_ref, scale):
    lane = lax.broadcasted_iota(jnp.int32, (1, LANE), 1)
    lo = lane < DF_HD
    outs = []
    for h in range(DF_HEADS):
        sl = slice(h * LANE, (h + 1) * LANE)
        xh = acc[:, sl]
        sq = xh * xh
        s_lo = jnp.sum(jnp.where(lo, sq, 0.0), axis=-1, keepdims=True)
        s_hi = jnp.sum(jnp.where(lo, 0.0, sq), axis=-1, keepdims=True)
        r = jnp.where(lo, lax.rsqrt(s_lo * (1.0 / DF_HD) + EPS), lax.rsqrt(s_hi * (1.0 / DF_HD) + EPS))
        outs.append(xh * r * (g_ref[:, sl] * scale))
    return outs


def _inproj_kernel(x_ref, gn_ref, w_ref, lbl_ref, gq_ref, gk_ref, gmq_ref,
                   p_ref, logf_ref, k_ref, v_ref, xn_scr, *, k_transposed):
    xn_scr[...] = _rms(x_ref[...], gn_ref[...]).astype(BF16)

    def proj(j):
        return _dot(xn_scr[...], w_ref[:, j * D_MODEL:(j + 1) * D_MODEL])

    def put(j, val):
        p_ref[:, j * D_MODEL:(j + 1) * D_MODEL] = val.astype(BF16)

    def passthrough(j):
        put(j, proj(j))

    def forget():
        l = lbl_ref[...]
        e = jnp.exp(l - jnp.max(l, axis=0, keepdims=True))
        lb = e[0:1] / jnp.sum(e, axis=0, keepdims=True)
        sp = _sigmoid(proj(CG_HF))
        logf_ref[...] = jnp.log(lb + (1.0 - lb) * sp)
        put(CG_HF, (1.0 - lb) * (1.0 - sp))

    def silu(j):
        acc = proj(j)
        put(j, acc * _sigmoid(acc))

    def gate(j):
        put(j, _sigmoid(proj(j)))

    def diff_q():
        outs = _half_head_rms(proj(CG_DQ), gq_ref, DF_HD ** -0.5 * LOG2E)
        for h in range(DF_HEADS):
            p_ref[:, CG_DQ * D_MODEL + h * LANE:CG_DQ * D_MODEL + (h + 1) * LANE] = outs[h].astype(BF16)

    def diff_k():
        outs = _half_head_rms(proj(CG_DK), gk_ref, 1.0)
        for h in range(DF_HEADS):
            sl = slice(h * LANE, (h + 1) * LANE)
            p_ref[:, CG_DK * D_MODEL + h * LANE:CG_DK * D_MODEL + (h + 1) * LANE] = outs[h].astype(BF16)
            if k_transposed:
                k_ref[0, sl, :] = outs[h].T
            else:
                k_ref[:, sl] = outs[h]

    def diff_v():
        acc = proj(CG_DV)
        put(CG_DV, acc)
        v_ref[...] = acc

    def mem_q():
        acc = proj(CG_MQ)
        for h in range(MEM_HEADS):
            sl = slice(h * MEM_HD, (h + 1) * MEM_HD)
            p_ref[:, CG_MQ * D_MODEL + h * MEM_HD:CG_MQ * D_MODEL + (h + 1) * MEM_HD] = (
                _rms(acc[:, sl], gmq_ref[...]) * (MEM_HD ** -0.5 * LOG2E)).astype(BF16)

    diff_k()
    passthrough(CG_HI)
    diff_q()
    diff_v()
    forget()
    gate(CG_G0)
    mem_q()
    silu(CG_ZA)
    gate(CG_G0 + 1)
    silu(CG_ZB)
    gate(CG_G0 + 2)
    silu(CG_ZM)
    passthrough(CG_HQ)


def _in_proj(x, g_norm, w_in, lb_logits, g_dq, g_dk, g_mq, *, seq, k_transposed):
    n = x.shape[0]
    tm = min(256, seq) if k_transposed else min(256, n)
    row = lambda i: (i, 0)
    const = lambda i: (0, 0)
    if k_transposed:
        per_b = seq // tm
        k_spec = pl.BlockSpec((1, D_MODEL, tm), lambda i: (i // per_b, 0, i % per_b))
        k_shape = jax.ShapeDtypeStruct((n // seq, D_MODEL, seq), F32)
    else:
        k_spec = pl.BlockSpec((tm, D_MODEL), row)
        k_shape = jax.ShapeDtypeStruct((n, D_MODEL), F32)
    return pl.pallas_call(
        functools.partial(_inproj_kernel, k_transposed=k_transposed),
        grid=(n // tm,),
        in_specs=[
            pl.BlockSpec((tm, D_MODEL), row),
            pl.BlockSpec((1, D_MODEL), const),
            pl.BlockSpec((D_MODEL, N_COLGROUPS * D_MODEL), lambda i: (0, 0), pipeline_mode=pl.Buffered(1)),
            pl.BlockSpec((2, D_MODEL), const),
            pl.BlockSpec((1, D_MODEL), const),
            pl.BlockSpec((1, D_MODEL), const),
            pl.BlockSpec((1, MEM_HD), const),
        ],
        out_specs=[
            pl.BlockSpec((tm, N_COLGROUPS * D_MODEL), row),
            pl.BlockSpec((tm, D_MODEL), row),
            k_spec,
            pl.BlockSpec((tm, D_MODEL), row),
        ],
        out_shape=[
            jax.ShapeDtypeStruct((n, N_COLGROUPS * D_MODEL), BF16),
            jax.ShapeDtypeStruct((n, D_MODEL), F32),
            k_shape,
            jax.ShapeDtypeStruct((n, D_MODEL), F32),
        ],
        scratch_shapes=[pltpu.VMEM((tm, D_MODEL), BF16)],
        compiler_params=_params("arbitrary", vmem=VMEM_LIMIT_INPROJ),
        name="in_proj",
    )(x, g_norm.reshape(1, -1), w_in, lb_logits,
      jnp.tile(g_dq, 2 * DF_HEADS).reshape(1, -1), jnp.tile(g_dk, 2 * DF_HEADS).reshape(1, -1), g_mq.reshape(1, -1))


def _split_bf16(x):
    hi = x.astype(BF16)
    return hi, (x - hi.astype(F32)).astype(BF16)


def _hgrn_kernel(q_ref, k_ref, v_ref, z_ref, lf_ref, s0_ref, g_ref, y_ref, s_ref, st_scr, *, seq, tc, hp):
    nb = tc // GLA_BLOCK
    row = lax.broadcasted_iota(jnp.int32, (tc, tc), 0)
    col = lax.broadcasted_iota(jnp.int32, (tc, tc), 1)
    causal = col <= row
    tril = jnp.where(causal, 1.0, 0.0).astype(BF16)
    rowk = lax.broadcasted_iota(jnp.int32, (tc, LANE), 0)

    @pl.when(pl.program_id(1) == 0)
    def _():
        for h in range(hp):
            st_scr[h] = s0_ref[0, h].T

    n_chunks = seq // tc
    unroll = 4 if n_chunks % 4 == 0 else 1

    stacked_att = tc % LANE == 0
    tril2 = jnp.concatenate([tril, tril], axis=1)

    def chunks(c, carry):
        heads = [slice(h * LANE, (h + 1) * LANE) for h in range(hp)]
        for u in range(unroll):
            rows = pl.ds(pl.multiple_of((c * unroll + u) * tc, tc), tc)
            g_hi, g_lo = _split_bf16(lf_ref[rows, :])
            b_all = _dot(tril2, jnp.concatenate([g_hi, g_lo], axis=0))
            bs = [b_all[:, sl] for sl in heads]
            qs = [q_ref[rows, sl].astype(F32) for sl in heads]
            ks = [k_ref[rows, sl].astype(F32) for sl in heads]
            vs = [v_ref[rows, sl] for sl in heads]

            q_ins, sv_t = [], []
            for h in range(hp):
                b, st = bs[h], st_scr[h]
                b_end = b[tc - 1:tc, :]
                v_t = vs[h].astype(F32).T.astype(BF16)
                q_ins.append((qs[h] * jnp.exp(b)).astype(BF16))
                sv_t.append(jnp.concatenate([st.astype(BF16), v_t], axis=1))
                k_dec = (ks[h] * jnp.exp(b_end - b)).astype(BF16)
                st_scr[h] = jnp.exp(b_end) * st + _dot(v_t, k_dec)

            atts = []
            for h in range(hp):
                b, q, k = bs[h], qs[h], ks[h]
                starts = [b[n * GLA_BLOCK - 1:n * GLA_BLOCK, :] if n else jnp.zeros((1, LANE), F32) for n in range(nb)]
                k_inv = [(k * jnp.exp(jnp.where(rowk < (n + 1) * GLA_BLOCK, starts[n] - b, 0.0))).astype(BF16)
                         for n in range(nb)]
                if stacked_att:
                    g_rows = jnp.concatenate([jnp.broadcast_to(g, (GLA_BLOCK, LANE)) for g in starts], axis=0)
                    q_dec = (q * jnp.exp(b - g_rows)).astype(BF16)
                    wide = _dot_nt(q_dec, jnp.concatenate(k_inv, axis=0))
                    att = [wide[n * GLA_BLOCK:(n + 1) * GLA_BLOCK, n * tc:(n + 1) * tc] for n in range(nb)]
                else:
                    att = [_dot_nt((q[n * GLA_BLOCK:(n + 1) * GLA_BLOCK]
                                    * jnp.exp(b[n * GLA_BLOCK:(n + 1) * GLA_BLOCK] - starts[n])).astype(BF16), k_inv[n])
                           for n in range(nb)]
                atts.append(jnp.where(causal, jnp.concatenate(att, axis=0), 0.0).astype(BF16))

            for h in range(hp):
                o = _dot_nt(jnp.concatenate([q_ins[h], atts[h]], axis=1), sv_t[h])
                y = _rms(o, g_ref[...]) * z_ref[rows, heads[h]].astype(F32)
                y_ref[rows, heads[h]] = y.astype(BF16)
        return carry

    lax.fori_loop(0, n_chunks // unroll, chunks, 0)

    @pl.when(pl.program_id(1) == pl.num_programs(1) - 1)
    def _():
        for h in range(hp):
            s_ref[0, h] = st_scr[h].T


def _hgrn(p, logf, s0, g_hg_out, batch, seq):
    hp = HG_HEADS
    ts = min(1024, seq)
    tc = min(128, ts)
    per_b = seq // ts

    def col(cg):
        return pl.BlockSpec((ts, D_MODEL), lambda b, t: (b * per_b + t, cg))

    st = pl.BlockSpec((1, hp, HG_D, HG_D), lambda b, t: (b, 0, 0, 0))
    return pl.pallas_call(
        functools.partial(_hgrn_kernel, seq=ts, tc=tc, hp=hp),
        grid=(batch, per_b),
        in_specs=[col(CG_HQ), col(CG_HF), col(CG_HI), col(CG_ZA), col(0), st,
                  pl.BlockSpec((1, HG_D), lambda b, t: (0, 0))],
        out_specs=[col(0), st],
        out_shape=[jax.ShapeDtypeStruct((batch * seq, D_MODEL), BF16),
                   jax.ShapeDtypeStruct((batch, HG_HEADS, HG_D, HG_D), F32)],
        scratch_shapes=[pltpu.VMEM((hp, HG_D, HG_D), F32)],
        compiler_params=_params("arbitrary", "arbitrary"),
        name="hgrn",
    )(p, p, p, p, logf, s0, g_hg_out.reshape(1, -1))


def _lambda(lam_ref, lam_init):
    lv = lam_ref[...]
    return (jnp.exp(jnp.sum(lv[0:1] * lv[1:2], axis=-1, keepdims=True))
            - jnp.exp(jnp.sum(lv[2:3] * lv[3:4], axis=-1, keepdims=True)) + lam_init)


def _split_maps(q):
    first = lax.broadcasted_iota(jnp.int32, (1, LANE), 1) < DF_HD
    zero = jnp.zeros((), q.dtype)
    return jnp.concatenate([jnp.where(first, q, zero), jnp.where(first, zero, q)], axis=0)


def _diffattn_kernel(lam_ref, q_ref, k_ref, v_ref, z_ref, g_ref, y_ref, s_scr, p_scr, vt_scr,
                     *, n_q, tq, rc, hp, lam_init):
    w = 2 * tq
    lam = _lambda(lam_ref, lam_init)
    heads = [slice(h * LANE, (h + 1) * LANE) for h in range(hp)]
    for h, sl in enumerate(heads):
        vt_scr[h, 0:DF_VD, :] = v_ref[:, sl].astype(F32).T.astype(BF16)
        vt_scr[h, DF_VD:, :] = jnp.ones((ONES_ROWS, vt_scr.shape[2]), BF16)

    key = lax.broadcasted_iota(jnp.int32, (tq, w), 0)
    qry = lax.broadcasted_iota(jnp.int32, (tq, w), 1) & (tq - 1)
    bias = jnp.where((key // CHUNK) <= (qry // CHUNK), 0.0, -jnp.inf)

    def fold8(x, op):
        return op(x.reshape(rc // 8, 8, w), axis=0)

    def scores(h, i):
        n_vis = (i + 1) * tq
        s_buf = s_scr.at[2 * h + i % 2]
        s = _dot_nt(k_ref[0:n_vis, heads[h]], _split_maps(q_ref[i * tq:(i + 1) * tq, heads[h]]))
        m8 = None
        for c in range(n_vis // rc):
            blk = s[c * rc:(c + 1) * rc]
            if (c + 1) * rc > n_vis - tq:
                blk = blk + bias[c * rc - (n_vis - tq):(c + 1) * rc - (n_vis - tq)]
            s_buf[c * rc:(c + 1) * rc, :] = blk
            m8 = fold8(blk, jnp.max) if m8 is None else jnp.maximum(m8, fold8(blk, jnp.max))
        return jnp.max(m8, axis=0, keepdims=True)

    def probs(h, i, m):
        s_buf, p_buf = s_scr.at[2 * h + i % 2], p_scr.at[2 * h + i % 2]
        for c in range((i + 1) * tq // rc):
            rows = slice(c * rc, (c + 1) * rc)
            p_buf[rows, :] = jnp.exp2(s_buf[rows, :] - m).astype(BF16)

    maxes = {}
    for i in range(min(2, n_q)):
        for h in range(hp):
            maxes[h, i] = scores(h, i)
    for h in range(hp):
        probs(h, 0, maxes[h, 0])
    for i in range(n_q):
        n_vis = (i + 1) * tq
        rows_q = slice(i * tq, (i + 1) * tq)
        for h in range(hp):
            if i + 2 < n_q:
                maxes[h, i + 2] = scores(h, i + 2)
            pv = _dot(vt_scr[h, :, 0:n_vis], p_scr[2 * h + i % 2, 0:n_vis, :])
            if i + 1 < n_q:
                probs(h, i + 1, maxes[h, i + 1])
            rinv = 1.0 / pv[DF_VD:DF_VD + 1, :]
            o = (pv[:DF_VD, :tq] * rinv[:, :tq] - pv[:DF_VD, tq:] * (rinv[:, tq:] * lam)).T
            y = _rms(o, g_ref[...]) * (1.0 - lam_init) * z_ref[rows_q, heads[h]].astype(F32)
            y_ref[rows_q, heads[h]] = y.astype(BF16)


def _diff_attn(lam_vecs, p, g_dsub, *, batch, seq, lam_init):
    tq = rc = 256
    hp = 1
    nhp = DF_HEADS // hp

    def col(cg):
        return pl.BlockSpec((seq, hp * LANE), lambda b, h: (b, cg * nhp + h))

    return pl.pallas_call(
        functools.partial(_diffattn_kernel, n_q=seq // tq, tq=tq, rc=rc, hp=hp, lam_init=lam_init),
        grid=(batch, nhp),
        in_specs=[pl.BlockSpec((4, DF_HD), lambda b, h: (0, 0)),
                  col(CG_DQ), col(CG_DK), col(CG_DV), col(CG_ZB),
                  pl.BlockSpec((1, DF_VD), lambda b, h: (0, 0))],
        out_specs=pl.BlockSpec((seq, hp * LANE), lambda b, h: (b, h)),
        out_shape=jax.ShapeDtypeStruct((batch * seq, D_MODEL), BF16),
        scratch_shapes=[pltpu.VMEM((2 * hp, seq, 2 * tq), F32), pltpu.VMEM((2 * hp, seq, 2 * tq), BF16),
                        pltpu.VMEM((hp, DF_VD + ONES_ROWS, seq), BF16)],
        compiler_params=_params("arbitrary", "arbitrary"),
        name="diff_attn",
    )(lam_vecs, p, p, p, p, g_dsub.reshape(1, -1))


def _cached_diffattn_kernel(lam_ref, q_ref, kn_ref, vn_ref, z_ref, g_ref, kt_ref, vc_ref, y_ref,
                            *, tq, past, lam_init):
    lam = _lambda(lam_ref, lam_init)
    heads = [slice(h * LANE, (h + 1) * LANE) for h in range(DF_HEADS)]
    scores = []
    for sl in heads:
        qs = _split_maps(q_ref[:, sl])
        scores.append((_dot(qs, kt_ref[0, sl, :].astype(BF16)),
                       _dot_nt(qs, kn_ref[:, sl])))
    for h, sl in enumerate(heads):
        s_c, s_n = scores[h]
        m = jnp.maximum(jnp.max(s_c, axis=-1, keepdims=True), jnp.max(s_n, axis=-1, keepdims=True))
        p_c = jnp.exp2(s_c - m)
        p_n = jnp.exp2(s_n - m)
        rinv = 1.0 / (jnp.sum(p_c, axis=-1, keepdims=True) + jnp.sum(p_n, axis=-1, keepdims=True))
        r1 = rinv[:tq]
        r2 = rinv[tq:] * lam
        a_c = (p_c[:tq] * r1 - p_c[tq:] * r2).astype(BF16)
        a_n = (p_n[:tq] * r1 - p_n[tq:] * r2).astype(BF16)
        v_c = vc_ref[0, pl.ds(h, past, stride=DF_HEADS), :].astype(BF16)
        o = _dot(a_c, v_c) + _dot(a_n, vn_ref[:, sl])
        y = _rms(o, g_ref[...]) * (1.0 - lam_init) * z_ref[:, sl].astype(F32)
        y_ref[:, sl] = y.astype(BF16)


def _cached_diff_attn(lam_vecs, p, g_dsub, kt_cache, v_cache, *, batch, seq, lam_init):
    past = v_cache.shape[1]

    def col(cg):
        return pl.BlockSpec((seq, D_MODEL), lambda b: (b, cg))

    return pl.pallas_call(
        functools.partial(_cached_diffattn_kernel, tq=seq, past=past, lam_init=lam_init),
        grid=(batch,),
        in_specs=[pl.BlockSpec((4, DF_HD), lambda b: (0, 0)),
                  col(CG_DQ), col(CG_DK), col(CG_DV), col(CG_ZB),
                  pl.BlockSpec((1, DF_VD), lambda b: (0, 0)),
                  pl.BlockSpec((1, D_MODEL, past), lambda b: (b, 0, 0)),
                  pl.BlockSpec((1, past * DF_HEADS, DF_VD), lambda b: (b, 0, 0))],
        out_specs=pl.BlockSpec((seq, D_MODEL), lambda b: (b, 0)),
        out_shape=jax.ShapeDtypeStruct((batch * seq, D_MODEL), BF16),
        compiler_params=_params("arbitrary"),
        name="cached_diff_attn",
    )(lam_vecs, p, p, p, p, g_dsub.reshape(1, -1), kt_cache, v_cache.reshape(batch, past * DF_HEADS, DF_VD))


def _merge_kernel(x_ref, ya_ref, yb_ref, qm_ref, zm_ref, g0_ref, g1_ref, g2_ref, mk_ref, mv_ref,
                  wb_ref, wo_ref, y_ref):
    heads = [slice(h * MEM_HD, (h + 1) * MEM_HD) for h in range(MEM_HEADS)]
    scores = [_dot_nt(qm_ref[:, sl], mk_ref[:, sl]) for sl in heads]
    hsum = (g0_ref[...].astype(F32) * _dot(ya_ref[...], wb_ref[0])
            + g1_ref[...].astype(F32) * _dot(yb_ref[...], wb_ref[1]))
    ym = []
    for s, sl in zip(scores, heads):
        p = jnp.exp2(s - jnp.max(s, axis=-1, keepdims=True))
        rinv = 1.0 / jnp.sum(p, axis=-1, keepdims=True)
        o = _dot(p.astype(BF16), mv_ref[:, sl]) * rinv
        ym.append((o * zm_ref[:, sl].astype(F32)).astype(BF16))
    hsum = hsum + g2_ref[...].astype(F32) * _dot(jnp.concatenate(ym, axis=-1), wb_ref[2])
    y_ref[...] = x_ref[...] + _dot(hsum.astype(BF16), wo_ref[...])


def _merge(x, ya, yb, p, mem_k, mem_v, w_branch, w_out, *, seq, n_mem):
    n = x.shape[0]
    tm = min(256, seq)
    per_b = seq // tm
    row = lambda i: (i, 0)

    def col(cg):
        return pl.BlockSpec((tm, D_MODEL), lambda i: (i, cg))

    mem = pl.BlockSpec((n_mem, D_MODEL), lambda i: (i // per_b, 0))
    return pl.pallas_call(
        _merge_kernel,
        grid=(n // tm,),
        in_specs=[pl.BlockSpec((tm, D_MODEL), row), pl.BlockSpec((tm, D_MODEL), row), pl.BlockSpec((tm, D_MODEL), row),
                  col(CG_MQ), col(CG_ZM), col(CG_G0), col(CG_G0 + 1), col(CG_G0 + 2), mem, mem,
                  pl.BlockSpec((N_BRANCH, D_MODEL, D_MODEL), lambda i: (0, 0, 0), pipeline_mode=pl.Buffered(1)),
                  pl.BlockSpec((D_MODEL, D_MODEL), lambda i: (0, 0), pipeline_mode=pl.Buffered(1))],
        out_specs=pl.BlockSpec((tm, D_MODEL), row),
        out_shape=jax.ShapeDtypeStruct((n, D_MODEL), F32),
        compiler_params=_params("arbitrary"),
        name="merge",
    )(x, ya, yb, p, p, p, p, p, mem_k, mem_v, w_branch, w_out)


def _layer(x, s0, past_k, past_v, mem_k, mem_v, lam_init, w, n_mem):
    batch, seq, _ = x.shape
    n = batch * seq
    xf = x.reshape(n, D_MODEL)
    prompt = past_k is None
    p, logf, k32, v32 = _in_proj(xf, w["g_norm"], w["w_in"], w["lb_logits"], w["g_dq"], w["g_dk"], w["g_mq"],
                                 seq=seq, k_transposed=prompt)
    ya, s_new = _hgrn(p, logf, s0, w["g_hg_out"], batch, seq)
    if prompt:
        yb = _diff_attn(w["lam_vecs"], p, w["g_dsub"], batch=batch, seq=seq, lam_init=lam_init)
    else:
        past = past_k.shape[1]
        kt_cache = jnp.transpose(past_k, (0, 2, 3, 4, 1)).reshape(batch, D_MODEL, past)
        yb = _cached_diff_attn(w["lam_vecs"], p, w["g_dsub"], kt_cache, past_v, batch=batch, seq=seq,
                               lam_init=lam_init)
    y = _merge(xf, ya, yb, p, mem_k, mem_v, w["w_branch"], w["w_out"], seq=seq, n_mem=n_mem)
    if prompt:
        k_out = jnp.transpose(k32.reshape(batch, DF_HEADS, 2, DF_HD, seq), (0, 4, 1, 2, 3))
    else:
        k_out = k32.reshape(batch, seq, DF_HEADS, 2, DF_HD)
    return y.reshape(batch, seq, D_MODEL), s_new, k_out, v32.reshape(batch, seq, DF_HEADS, DF_VD)


def kernel(x_prompt, x_sample, mem_prompt, cache_diff_k, cache_diff_v, cache_mem_k, cache_mem_v, state_hgrn,
           g_norm, w_in, hg_lb_logits, g_hg_out, g_dq, g_dk, lam_q1, lam_k1, lam_q2, lam_k2, g_dsub, g_mem,
           w_mkv, g_mq, g_mk, w_branch, w_out):
    depth = g_norm.shape[0]
    assert depth == 1 and hg_lb_logits.shape[0] == 2, "single-layer step only"
    batch, n_mem = mem_prompt.shape[:2]
    lam_init = 0.8 - 0.6
    w = dict(g_norm=g_norm[0], w_in=w_in[0].astype(BF16), lb_logits=hg_lb_logits, g_hg_out=g_hg_out[0],
             g_dq=g_dq[0], g_dk=g_dk[0], g_mq=g_mq[0], g_dsub=g_dsub[0],
             lam_vecs=jnp.stack([lam_q1[0], lam_k1[0], lam_q2[0], lam_k2[0]]),
             w_branch=w_branch[0].astype(BF16), w_out=w_out[0].astype(BF16))

    mem_k, mem_v, mem_kb, mem_vb = _memory_kv(mem_prompt.reshape(batch * n_mem, D_MODEL), g_mem[0], w_mkv[0], g_mk[0],
                                              batch=batch, n_mem=n_mem)
    s0 = jnp.zeros((batch, HG_HEADS, HG_D, HG_D), F32)
    y_p, s_p, k_p, v_p = _layer(x_prompt, s0, None, None, mem_kb, mem_vb, lam_init, w, n_mem)
    y_s, s_s, k_s, v_s = _layer(x_sample, state_hgrn[0], cache_diff_k[0], cache_diff_v[0],
                                cache_mem_k[0].reshape(-1, D_MODEL).astype(BF16),
                                cache_mem_v[0].reshape(-1, D_MODEL).astype(BF16), lam_init, w, n_mem)
    return (y_p, y_s, s_p[None], s_s[None], k_p[None], v_p[None], k_s[None], v_s[None], mem_k[None], mem_v[None])
```

```python
import functools

import jax
import jax.numpy as jnp
from jax import lax
from jax.experimental import pallas as pl
from jax.experimental.pallas import tpu as pltpu

F32 = jnp.float32
BF16 = jnp.bfloat16

EPS = 1e-6
LOG2E = 1.4426950408889634
D_MODEL = 1024
CHUNK = 64
GLA_BLOCK = 16
HG_HEADS, HG_D = 8, 128
DF_HEADS, DF_HD, DF_VD = 8, 64, 128
MEM_HEADS, MEM_HD = 4, 256
N_BRANCH = 3
N_COLGROUPS = 13
(CG_HQ, CG_HF, CG_HI, CG_ZA, CG_DQ, CG_DK, CG_DV, CG_ZB, CG_MQ, CG_ZM, CG_G0) = range(11)

LANE = 128
ONES_ROWS = 16
VMEM_LIMIT = 52 * 1024 * 1024
VMEM_LIMIT_INPROJ = 60 * 1024 * 1024

NT = (((1,), (1,)), ((), ()))
TN = (((0,), (0,)), ((), ()))


def _dot(a, b):
    return jnp.dot(a, b, preferred_element_type=F32)


def _dot_nt(a, b):
    return lax.dot_general(a, b, NT, preferred_element_type=F32)


def _dot_tn(a, b):
    return lax.dot_general(a, b, TN, preferred_element_type=F32)


def _sigmoid(x):
    return 0.5 * jnp.tanh(0.5 * x) + 0.5


def _rms(x, g):
    return x * lax.rsqrt(jnp.mean(x * x, axis=-1, keepdims=True) + EPS) * g


def _params(*sem, vmem=VMEM_LIMIT):
    return pltpu.CompilerParams(dimension_semantics=sem, vmem_limit_bytes=vmem)


def _memkv_kernel(mem_ref, g_ref, w_ref, gk_ref, k_ref, v_ref, kb_ref, vb_ref):
    xn = _rms(mem_ref[...], g_ref[...]).astype(BF16)
    kv = _dot(xn, w_ref[...])
    for h in range(MEM_HEADS):
        sl = slice(h * MEM_HD, (h + 1) * MEM_HD)
        k = _rms(kv[:, sl], gk_ref[...])
        k_ref[0, :, h, :] = k
        kb_ref[:, sl] = k.astype(BF16)
        v_ref[0, :, h, :] = kv[:, D_MODEL + h * MEM_HD:D_MODEL + (h + 1) * MEM_HD]
    vb_ref[...] = kv[:, D_MODEL:].astype(BF16)


def _memory_kv(mem, g_mem, w_mkv, g_mk, *, batch, n_mem):
    out = pl.BlockSpec((1, n_mem, MEM_HEADS, MEM_HD), lambda i: (i, 0, 0, 0))
    flat = pl.BlockSpec((n_mem, D_MODEL), lambda i: (i, 0))
    return pl.pallas_call(
        _memkv_kernel,
        grid=(batch,),
        in_specs=[
            pl.BlockSpec((n_mem, D_MODEL), lambda i: (i, 0)),
            pl.BlockSpec((1, D_MODEL), lambda i: (0, 0)),
            pl.BlockSpec((D_MODEL, 2 * D_MODEL), lambda i: (0, 0)),
            pl.BlockSpec((1, MEM_HD), lambda i: (0, 0)),
        ],
        out_specs=[out, out, flat, flat],
        out_shape=[jax.ShapeDtypeStruct((batch, n_mem, MEM_HEADS, MEM_HD), F32)] * 2
        + [jax.ShapeDtypeStruct((batch * n_mem, D_MODEL), BF16)] * 2,
        compiler_params=_params("arbitrary"),
        name="memory_kv",
    )(mem, g_mem.reshape(1, -1), w_mkv.astype(BF16), g_mk.reshape(1, -1))


def _half_head_rms(acc, g_ref, scale):
    lane = lax.broadcasted_iota(jnp.int32, (1, LANE), 1)
    lo = lane < DF_HD
    outs = []
    for h in range(DF_HEADS):
        sl = slice(h * LANE, (h + 1) * LANE)
        xh = acc[:, sl]
        sq = xh * xh
        s_lo = jnp.sum(jnp.where(lo, sq, 0.0), axis=-1, keepdims=True)
        s_hi = jnp.sum(jnp.where(lo, 0.0, sq), axis=-1, keepdims=True)
        r = jnp.where(lo, lax.rsqrt(s_lo * (1.0 / DF_HD) + EPS), lax.rsqrt(s_hi * (1.0 / DF_HD) + EPS))
        outs.append(xh * r * (g_ref[:, sl] * scale))
    return outs


def _inproj_kernel(x_ref, gn_ref, w_ref, lbl_ref, gq_ref, gk_ref, gmq_ref,
                   p_ref, logf_ref, k_ref, v_ref, xn_scr, *, k_transposed):
    xn_scr[...] = _rms(x_ref[...], gn_ref[...]).astype(BF16)

    def proj(j):
        return _dot(xn_scr[...], w_ref[:, j * D_MODEL:(j + 1) * D_MODEL])

    def put(j, val):
        p_ref[:, j * D_MODEL:(j + 1) * D_MODEL] = val.astype(BF16)

    def passthrough(j):
        put(j, proj(j))

    def forget():
        l = lbl_ref[...]
        e = jnp.exp(l - jnp.max(l, axis=0, keepdims=True))
        lb = e[0:1] / jnp.sum(e, axis=0, keepdims=True)
        sp = _sigmoid(proj(CG_HF))
        logf_ref[...] = jnp.log(lb + (1.0 - lb) * sp)
        put(CG_HF, (1.0 - lb) * (1.0 - sp))

    def silu(j):
        acc = proj(j)
        put(j, acc * _sigmoid(acc))

    def gate(j):
        put(j, _sigmoid(proj(j)))

    def diff_q():
        outs = _half_head_rms(proj(CG_DQ), gq_ref, DF_HD ** -0.5 * LOG2E)
        for h in range(DF_HEADS):
            p_ref[:, CG_DQ * D_MODEL + h * LANE:CG_DQ * D_MODEL + (h + 1) * LANE] = outs[h].astype(BF16)

    def diff_k():
        outs = _half_head_rms(proj(CG_DK), gk_ref, 1.0)
        for h in range(DF_HEADS):
            sl = slice(h * LANE, (h + 1) * LANE)
            p_ref[:, CG_DK * D_MODEL + h * LANE:CG_DK * D_MODEL + (h + 1) * LANE] = outs[h].astype(BF16)
            if k_transposed:
                k_ref[0, sl, :] = outs[h].T
            else:
                k_ref[:, sl] = outs[h]

    def diff_v():
        acc = proj(CG_DV)
        put(CG_DV, acc)
        v_ref[...] = acc

    def mem_q():
        acc = proj(CG_MQ)
        for h in range(MEM_HEADS):
            sl = slice(h * MEM_HD, (h + 1) * MEM_HD)
            p_ref[:, CG_MQ * D_MODEL + h * MEM_HD:CG_MQ * D_MODEL + (h + 1) * MEM_HD] = (
                _rms(acc[:, sl], gmq_ref[...]) * (MEM_HD ** -0.5 * LOG2E)).astype(BF16)

    diff_k()
    passthrough(CG_HI)
    diff_q()
    diff_v()
    forget()
    gate(CG_G0)
    mem_q()
    silu(CG_ZA)
    gate(CG_G0 + 1)
    silu(CG_ZB)
    gate(CG_G0 + 2)
    silu(CG_ZM)
    passthrough(CG_HQ)


def _in_proj(x, g_norm, w_in, lb_logits, g_dq, g_dk, g_mq, *, seq, k_transposed):
    n = x.shape[0]
    tm = min(256, seq) if k_transposed else min(256, n)
    row = lambda i: (i, 0)
    const = lambda i: (0, 0)
    if k_transposed:
        per_b = seq // tm
        k_spec = pl.BlockSpec((1, D_MODEL, tm), lambda i: (i // per_b, 0, i % per_b))
        k_shape = jax.ShapeDtypeStruct((n // seq, D_MODEL, seq), F32)
    else:
        k_spec = pl.BlockSpec((tm, D_MODEL), row)
        k_shape = jax.ShapeDtypeStruct((n, D_MODEL), F32)
    return pl.pallas_call(
        functools.partial(_inproj_kernel, k_transposed=k_transposed),
        grid=(n // tm,),
        in_specs=[
            pl.BlockSpec((tm, D_MODEL), row),
            pl.BlockSpec((1, D_MODEL), const),
            pl.BlockSpec((D_MODEL, N_COLGROUPS * D_MODEL), lambda i: (0, 0), pipeline_mode=pl.Buffered(1)),
            pl.BlockSpec((2, D_MODEL), const),
            pl.BlockSpec((1, D_MODEL), const),
            pl.BlockSpec((1, D_MODEL), const),
            pl.BlockSpec((1, MEM_HD), const),
        ],
        out_specs=[
            pl.BlockSpec((tm, N_COLGROUPS * D_MODEL), row),
            pl.BlockSpec((tm, D_MODEL), row),
            k_spec,
            pl.BlockSpec((tm, D_MODEL), row),
        ],
        out_shape=[
            jax.ShapeDtypeStruct((n, N_COLGROUPS * D_MODEL), BF16),
            jax.ShapeDtypeStruct((n, D_MODEL), F32),
            k_shape,
            jax.ShapeDtypeStruct((n, D_MODEL), F32),
        ],
        scratch_shapes=[pltpu.VMEM((tm, D_MODEL), BF16)],
        compiler_params=_params("arbitrary", vmem=VMEM_LIMIT_INPROJ),
        name="in_proj",
    )(x, g_norm.reshape(1, -1), w_in, lb_logits,
      jnp.tile(g_dq, 2 * DF_HEADS).reshape(1, -1), jnp.tile(g_dk, 2 * DF_HEADS).reshape(1, -1), g_mq.reshape(1, -1))


def _split_bf16(x):
    hi = x.astype(BF16)
    return hi, (x - hi.astype(F32)).astype(BF16)


def _hgrn_kernel(q_ref, k_ref, v_ref, z_ref, lf_ref, s0_ref, g_ref, y_ref, s_ref, st_scr, *, seq, tc, hp):
    nb = tc // GLA_BLOCK
    row = lax.broadcasted_iota(jnp.int32, (tc, tc), 0)
    col = lax.broadcasted_iota(jnp.int32, (tc, tc), 1)
    causal = col <= row
    tril = jnp.where(causal, 1.0, 0.0).astype(BF16)
    rowk = lax.broadcasted_iota(jnp.int32, (tc, LANE), 0)

    @pl.when(pl.program_id(1) == 0)
    def _():
        for h in range(hp):
            st_scr[h] = s0_ref[0, h].T

    n_chunks = seq // tc
    unroll = 4 if n_chunks % 4 == 0 else 1

    stacked_att = tc % LANE == 0
    tril2 = jnp.concatenate([tril, tril], axis=1)

    def chunks(c, carry):
        heads = [slice(h * LANE, (h + 1) * LANE) for h in range(hp)]
        for u in range(unroll):
            rows = pl.ds(pl.multiple_of((c * unroll + u) * tc, tc), tc)
            g_hi, g_lo = _split_bf16(lf_ref[rows, :])
            b_all = _dot(tril2, jnp.concatenate([g_hi, g_lo], axis=0))
            bs = [b_all[:, sl] for sl in heads]
            qs = [q_ref[rows, sl].astype(F32) for sl in heads]
            ks = [k_ref[rows, sl].astype(F32) for sl in heads]
            vs = [v_ref[rows, sl] for sl in heads]

            q_ins, sv_t = [], []
            for h in range(hp):
                b, st = bs[h], st_scr[h]
                b_end = b[tc - 1:tc, :]
                v_t = vs[h].astype(F32).T.astype(BF16)
                q_ins.append((qs[h] * jnp.exp(b)).astype(BF16))
                sv_t.append(jnp.concatenate([st.astype(BF16), v_t], axis=1))
                k_dec = (ks[h] * jnp.exp(b_end - b)).astype(BF16)
                st_scr[h] = jnp.exp(b_end) * st + _dot(v_t, k_dec)

            atts = []
            for h in range(hp):
                b, q, k = bs[h], qs[h], ks[h]
                starts = [b[n * GLA_BLOCK - 1:n * GLA_BLOCK, :] if n else jnp.zeros((1, LANE), F32) for n in range(nb)]
                k_inv = [(k * jnp.exp(jnp.where(rowk < (n + 1) * GLA_BLOCK, starts[n] - b, 0.0))).astype(BF16)
                         for n in range(nb)]
                if stacked_att:
                    g_rows = jnp.concatenate([jnp.broadcast_to(g, (GLA_BLOCK, LANE)) for g in starts], axis=0)
                    q_dec = (q * jnp.exp(b - g_rows)).astype(BF16)
                    wide = _dot_nt(q_dec, jnp.concatenate(k_inv, axis=0))
                    att = [wide[n * GLA_BLOCK:(n + 1) * GLA_BLOCK, n * tc:(n + 1) * tc] for n in range(nb)]
                else:
                    att = [_dot_nt((q[n * GLA_BLOCK:(n + 1) * GLA_BLOCK]
                                    * jnp.exp(b[n * GLA_BLOCK:(n + 1) * GLA_BLOCK] - starts[n])).astype(BF16), k_inv[n])
                           for n in range(nb)]
                atts.append(jnp.where(causal, jnp.concatenate(att, axis=0), 0.0).astype(BF16))

            for h in range(hp):
                o = _dot_nt(jnp.concatenate([q_ins[h], atts[h]], axis=1), sv_t[h])
                y = _rms(o, g_ref[...]) * z_ref[rows, heads[h]].astype(F32)
                y_ref[rows, heads[h]] = y.astype(BF16)
        return carry

    lax.fori_loop(0, n_chunks // unroll, chunks, 0)

    @pl.when(pl.program_id(1) == pl.num_programs(1) - 1)
    def _():
        for h in range(hp):
            s_ref[0, h] = st_scr[h].T


def _hgrn(p, logf, s0, g_hg_out, batch, seq):
    hp = HG_HEADS
    ts = min(1024, seq)
    tc = min(128, ts)
    per_b = seq // ts

    def col(cg):
        return pl.BlockSpec((ts, D_MODEL), lambda b, t: (b * per_b + t, cg))

    st = pl.BlockSpec((1, hp, HG_D, HG_D), lambda b, t: (b, 0, 0, 0))
    return pl.pallas_call(
        functools.partial(_hgrn_kernel, seq=ts, tc=tc, hp=hp),
        grid=(batch, per_b),
        in_specs=[col(CG_HQ), col(CG_HF), col(CG_HI), col(CG_ZA), col(0), st,
                  pl.BlockSpec((1, HG_D), lambda b, t: (0, 0))],
        out_specs=[col(0), st],
        out_shape=[jax.ShapeDtypeStruct((batch * seq, D_MODEL), BF16),
                   jax.ShapeDtypeStruct((batch, HG_HEADS, HG_D, HG_D), F32)],
        scratch_shapes=[pltpu.VMEM((hp, HG_D, HG_D), F32)],
        compiler_params=_params("arbitrary", "arbitrary"),
        name="hgrn",
    )(p, p, p, p, logf, s0, g_hg_out.reshape(1, -1))


def _lambda(lam_ref, lam_init):
    lv = lam_ref[...]
    return (jnp.exp(jnp.sum(lv[0:1] * lv[1:2], axis=-1, keepdims=True))
            - jnp.exp(jnp.sum(lv[2:3] * lv[3:4], axis=-1, keepdims=True)) + lam_init)


def _split_maps(q):
    first = lax.broadcasted_iota(jnp.int32, (1, LANE), 1) < DF_HD
    zero = jnp.zeros((), q.dtype)
    return jnp.concatenate([jnp.where(first, q, zero), jnp.where(first, zero, q)], axis=0)


def _diffattn_kernel(lam_ref, q_ref, k_ref, v_ref, z_ref, g_ref, y_ref, s_scr, p_scr, vt_scr,
                     *, n_q, tq, rc, hp, lam_init):
    w = 2 * tq
    lam = _lambda(lam_ref, lam_init)
    heads = [slice(h * LANE, (h + 1) * LANE) for h in range(hp)]
    for h, sl in enumerate(heads):
        vt_scr[h, 0:DF_VD, :] = v_ref[:, sl].astype(F32).T.astype(BF16)
        vt_scr[h, DF_VD:, :] = jnp.ones((ONES_ROWS, vt_scr.shape[2]), BF16)

    key = lax.broadcasted_iota(jnp.int32, (tq, w), 0)
    qry = lax.broadcasted_iota(jnp.int32, (tq, w), 1) & (tq - 1)
    bias = jnp.where((key // CHUNK) <= (qry // CHUNK), 0.0, -jnp.inf)

    def fold8(x, op):
        return op(x.reshape(rc // 8, 8, w), axis=0)

    def scores(h, i):
        n_vis = (i + 1) * tq
        s_buf = s_scr.at[2 * h + i % 2]
        s = _dot_nt(k_ref[0:n_vis, heads[h]], _split_maps(q_ref[i * tq:(i + 1) * tq, heads[h]]))
        m8 = None
        for c in range(n_vis // rc):
            blk = s[c * rc:(c + 1) * rc]
            if (c + 1) * rc > n_vis - tq:
                blk = blk + bias[c * rc - (n_vis - tq):(c + 1) * rc - (n_vis - tq)]
            s_buf[c * rc:(c + 1) * rc, :] = blk
            m8 = fold8(blk, jnp.max) if m8 is None else jnp.maximum(m8, fold8(blk, jnp.max))
        return jnp.max(m8, axis=0, keepdims=True)

    def probs(h, i, m):
        s_buf, p_buf = s_scr.at[2 * h + i % 2], p_scr.at[2 * h + i % 2]
        for c in range((i + 1) * tq // rc):
            rows = slice(c * rc, (c + 1) * rc)
            p_buf[rows, :] = jnp.exp2(s_buf[rows, :] - m).astype(BF16)

    maxes = {}
    for i in range(min(2, n_q)):
        for h in range(hp):
            maxes[h, i] = scores(h, i)
    for h in range(hp):
        probs(h, 0, maxes[h, 0])
    for i in range(n_q):
        n_vis = (i + 1) * tq
        rows_q = slice(i * tq, (i + 1) * tq)
        for h in range(hp):
            if i + 2 < n_q:
                maxes[h, i + 2] = scores(h, i + 2)
            pv = _dot(vt_scr[h, :, 0:n_vis], p_scr[2 * h + i % 2, 0:n_vis, :])
            if i + 1 < n_q:
                probs(h, i + 1, maxes[h, i + 1])
            rinv = 1.0 / pv[DF_VD:DF_VD + 1, :]
            o = (pv[:DF_VD, :tq] * rinv[:, :tq] - pv[:DF_VD, tq:] * (rinv[:, tq:] * lam)).T
            y = _rms(o, g_ref[...]) * (1.0 - lam_init) * z_ref[rows_q, heads[h]].astype(F32)
            y_ref[rows_q, heads[h]] = y.astype(BF16)


def _diff_attn(lam_vecs, p, g_dsub, *, batch, seq, lam_init):
    tq = rc = 256
    hp = 1
    nhp = DF_HEADS // hp

    def col(cg):
        return pl.BlockSpec((seq, hp * LANE), lambda b, h: (b, cg * nhp + h))

    return pl.pallas_call(
        functools.partial(_diffattn_kernel, n_q=seq // tq, tq=tq, rc=rc, hp=hp, lam_init=lam_init),
        grid=(batch, nhp),
        in_specs=[pl.BlockSpec((4, DF_HD), lambda b, h: (0, 0)),
                  col(CG_DQ), col(CG_DK), col(CG_DV), col(CG_ZB),
                  pl.BlockSpec((1, DF_VD), lambda b, h: (0, 0))],
        out_specs=pl.BlockSpec((seq, hp * LANE), lambda b, h: (b, h)),
        out_shape=jax.ShapeDtypeStruct((batch * seq, D_MODEL), BF16),
        scratch_shapes=[pltpu.VMEM((2 * hp, seq, 2 * tq), F32), pltpu.VMEM((2 * hp, seq, 2 * tq), BF16),
                        pltpu.VMEM((hp, DF_VD + ONES_ROWS, seq), BF16)],
        compiler_params=_params("arbitrary", "arbitrary"),
        name="diff_attn",
    )(lam_vecs, p, p, p, p, g_dsub.reshape(1, -1))


def _cached_diffattn_kernel(lam_ref, q_ref, kn_ref, vn_ref, z_ref, g_ref, kt_ref, vc_ref, y_ref,
                            *, tq, past, lam_init):
    lam = _lambda(lam_ref, lam_init)
    heads = [slice(h * LANE, (h + 1) * LANE) for h in range(DF_HEADS)]
    scores = []
    for sl in heads:
        qs = _split_maps(q_ref[:, sl])
        scores.append((_dot(qs, kt_ref[0, sl, :].astype(BF16)),
                       _dot_nt(qs, kn_ref[:, sl])))
    for h, sl in enumerate(heads):
        s_c, s_n = scores[h]
        m = jnp.maximum(jnp.max(s_c, axis=-1, keepdims=True), jnp.max(s_n, axis=-1, keepdims=True))
        p_c = jnp.exp2(s_c - m)
        p_n = jnp.exp2(s_n - m)
        rinv = 1.0 / (jnp.sum(p_c, axis=-1, keepdims=True) + jnp.sum(p_n, axis=-1, keepdims=True))
        r1 = rinv[:tq]
        r2 = rinv[tq:] * lam
        a_c = (p_c[:tq] * r1 - p_c[tq:] * r2).astype(BF16)
        a_n = (p_n[:tq] * r1 - p_n[tq:] * r2).astype(BF16)
        v_c = vc_ref[0, pl.ds(h, past, stride=DF_HEADS), :].astype(BF16)
        o = _dot(a_c, v_c) + _dot(a_n, vn_ref[:, sl])
        y = _rms(o, g_ref[...]) * (1.0 - lam_init) * z_ref[:, sl].astype(F32)
        y_ref[:, sl] = y.astype(BF16)


def _cached_diff_attn(lam_vecs, p, g_dsub, kt_cache, v_cache, *, batch, seq, lam_init):
    past = v_cache.shape[1]

    def col(cg):
        return pl.BlockSpec((seq, D_MODEL), lambda b: (b, cg))

    return pl.pallas_call(
        functools.partial(_cached_diffattn_kernel, tq=seq, past=past, lam_init=lam_init),
        grid=(batch,),
        in_specs=[pl.BlockSpec((4, DF_HD), lambda b: (0, 0)),
                  col(CG_DQ), col(CG_DK), col(CG_DV), col(CG_ZB),
                  pl.BlockSpec((1, DF_VD), lambda b: (0, 0)),
                  pl.BlockSpec((1, D_MODEL, past), lambda b: (b, 0, 0)),
                  pl.BlockSpec((1, past * DF_HEADS, DF_VD), lambda b: (b, 0, 0))],
        out_specs=pl.BlockSpec((seq, D_MODEL), lambda b: (b, 0)),
        out_shape=jax.ShapeDtypeStruct((batch * seq, D_MODEL), BF16),
        compiler_params=_params("arbitrary"),
        name="cached_diff_attn",
    )(lam_vecs, p, p, p, p, g_dsub.reshape(1, -1), kt_cache, v_cache.reshape(batch, past * DF_HEADS, DF_VD))


def _merge_kernel(x_ref, ya_ref, yb_ref, qm_ref, zm_ref, g0_ref, g1_ref, g2_ref, mk_ref, mv_ref,
                  wb_ref, wo_ref, y_ref):
    heads = [slice(h * MEM_HD, (h + 1) * MEM_HD) for h in range(MEM_HEADS)]
    scores = [_dot_nt(qm_ref[:, sl], mk_ref[:, sl]) for sl in heads]
    hsum = (g0_ref[...].astype(F32) * _dot(ya_ref[...], wb_ref[0])
            + g1_ref[...].astype(F32) * _dot(yb_ref[...], wb_ref[1]))
    ym = []
    for s, sl in zip(scores, heads):
        p = jnp.exp2(s - jnp.max(s, axis=-1, keepdims=True))
        rinv = 1.0 / jnp.sum(p, axis=-1, keepdims=True)
        o = _dot(p.astype(BF16), mv_ref[:, sl]) * rinv
        ym.append((o * zm_ref[:, sl].astype(F32)).astype(BF16))
    hsum = hsum + g2_ref[...].astype(F32) * _dot(jnp.concatenate(ym, axis=-1), wb_ref[2])
    y_ref[...] = x_ref[...] + _dot(hsum.astype(BF16), wo_ref[...])


def _merge(x, ya, yb, p, mem_k, mem_v, w_branch, w_out, *, seq, n_mem):
    n = x.shape[0]
    tm = min(512, seq)
    per_b = seq // tm
    row = lambda i: (i, 0)

    def col(cg):
        return pl.BlockSpec((tm, D_MODEL), lambda i: (i, cg))

    mem = pl.BlockSpec((n_mem, D_MODEL), lambda i: (i // per_b, 0))
    return pl.pallas_call(
        _merge_kernel,
        grid=(n // tm,),
        in_specs=[pl.BlockSpec((tm, D_MODEL), row), pl.BlockSpec((tm, D_MODEL), row), pl.BlockSpec((tm, D_MODEL), row),
                  col(CG_MQ), col(CG_ZM), col(CG_G0), col(CG_G0 + 1), col(CG_G0 + 2), mem, mem,
                  pl.BlockSpec((N_BRANCH, D_MODEL, D_MODEL), lambda i: (0, 0, 0), pipeline_mode=pl.Buffered(1)),
                  pl.BlockSpec((D_MODEL, D_MODEL), lambda i: (0, 0), pipeline_mode=pl.Buffered(1))],
        out_specs=pl.BlockSpec((tm, D_MODEL), row),
        out_shape=jax.ShapeDtypeStruct((n, D_MODEL), F32),
        compiler_params=_params("arbitrary"),
        name="merge",
    )(x, ya, yb, p, p, p, p, p, mem_k, mem_v, w_branch, w_out)


def _layer(x, s0, past_k, past_v, mem_k, mem_v, lam_init, w, n_mem):
    batch, seq, _ = x.shape
    n = batch * seq
    xf = x.reshape(n, D_MODEL)
    prompt = past_k is None
    p, logf, k32, v32 = _in_proj(xf, w["g_norm"], w["w_in"], w["lb_logits"], w["g_dq"], w["g_dk"], w["g_mq"],
                                 seq=seq, k_transposed=prompt)
    ya, s_new = _hgrn(p, logf, s0, w["g_hg_out"], batch, seq)
    if prompt:
        yb = _diff_attn(w["lam_vecs"], p, w["g_dsub"], batch=batch, seq=seq, lam_init=lam_init)
    else:
        past = past_k.shape[1]
        kt_cache = jnp.transpose(past_k, (0, 2, 3, 4, 1)).reshape(batch, D_MODEL, past)
        yb = _cached_diff_attn(w["lam_vecs"], p, w["g_dsub"], kt_cache, past_v, batch=batch, seq=seq,
                               lam_init=lam_init)
    y = _merge(xf, ya, yb, p, mem_k, mem_v, w["w_branch"], w["w_out"], seq=seq, n_mem=n_mem)
    if prompt:
        k_out = jnp.transpose(k32.reshape(batch, DF_HEADS, 2, DF_HD, seq), (0, 4, 1, 2, 3))
    else:
        k_out = k32.reshape(batch, seq, DF_HEADS, 2, DF_HD)
    return y.reshape(batch, seq, D_MODEL), s_new, k_out, v32.reshape(batch, seq, DF_HEADS, DF_VD)


def kernel(x_prompt, x_sample, mem_prompt, cache_diff_k, cache_diff_v, cache_mem_k, cache_mem_v, state_hgrn,
           g_norm, w_in, hg_lb_logits, g_hg_out, g_dq, g_dk, lam_q1, lam_k1, lam_q2, lam_k2, g_dsub, g_mem,
           w_mkv, g_mq, g_mk, w_branch, w_out):
    depth = g_norm.shape[0]
    assert depth == 1 and hg_lb_logits.shape[0] == 2, "single-layer step only"
    batch, n_mem = mem_prompt.shape[:2]
    lam_init = 0.8 - 0.6
    w = dict(g_norm=g_norm[0], w_in=w_in[0].astype(BF16), lb_logits=hg_lb_logits, g_hg_out=g_hg_out[0],
             g_dq=g_dq[0], g_dk=g_dk[0], g_mq=g_mq[0], g_dsub=g_dsub[0],
             lam_vecs=jnp.stack([lam_q1[0], lam_k1[0], lam_q2[0], lam_k2[0]]),
             w_branch=w_branch[0].astype(BF16), w_out=w_out[0].astype(BF16))

    mem_k, mem_v, mem_kb, mem_vb = _memory_kv(mem_prompt.reshape(batch * n_mem, D_MODEL), g_mem[0], w_mkv[0], g_mk[0],
                                              batch=batch, n_mem=n_mem)
    s0 = jnp.zeros((batch, HG_HEADS, HG_D, HG_D), F32)
    y_p, s_p, k_p, v_p = _layer(x_prompt, s0, None, None, mem_kb, mem_vb, lam_init, w, n_mem)
    y_s, s_s, k_s, v_s = _layer(x_sample, state_hgrn[0], cache_diff_k[0], cache_diff_v[0],
                                cache_mem_k[0].reshape(-1, D_MODEL).astype(BF16),
                                cache_mem_v[0].reshape(-1, D_MODEL).astype(BF16), lam_init, w, n_mem)
    return (y_p, y_s, s_p[None], s_s[None], k_p[None], v_p[None], k_s[None], v_s[None], mem_k[None], mem_v[None])
```

```python
import functools

import jax
import jax.numpy as jnp
from jax import lax
from jax.experimental import pallas as pl
from jax.experimental.pallas import tpu as pltpu

F32 = jnp.float32
BF16 = jnp.bfloat16

EPS = 1e-6
LOG2E = 1.4426950408889634
D_MODEL = 1024
CHUNK = 64
GLA_BLOCK = 16
HG_HEADS, HG_D = 8, 128
DF_HEADS, DF_HD, DF_VD = 8, 64, 128
MEM_HEADS, MEM_HD = 4, 256
N_BRANCH = 3
N_COLGROUPS = 13
(CG_HQ, CG_HF, CG_HI, CG_ZA, CG_DQ, CG_DK, CG_DV, CG_ZB, CG_MQ, CG_ZM, CG_G0) = range(11)

LANE = 128
ONES_ROWS = 16
VMEM_LIMIT = 52 * 1024 * 1024
VMEM_LIMIT_INPROJ = 60 * 1024 * 1024

NT = (((1,), (1,)), ((), ()))
TN = (((0,), (0,)), ((), ()))


def _dot(a, b):
    return jnp.dot(a, b, preferred_element_type=F32)


def _dot_nt(a, b):
    return lax.dot_general(a, b, NT, preferred_element_type=F32)


def _dot_tn(a, b):
    return lax.dot_general(a, b, TN, preferred_element_type=F32)


def _sigmoid(x):
    return 0.5 * jnp.tanh(0.5 * x) + 0.5


def _rms(x, g):
    return x * lax.rsqrt(jnp.mean(x * x, axis=-1, keepdims=True) + EPS) * g


def _params(*sem, vmem=VMEM_LIMIT):
    return pltpu.CompilerParams(dimension_semantics=sem, vmem_limit_bytes=vmem)


def _memkv_kernel(mem_ref, g_ref, w_ref, gk_ref, k_ref, v_ref, kb_ref, vb_ref):
    xn = _rms(mem_ref[...], g_ref[...]).astype(BF16)
    kv = _dot(xn, w_ref[...])
    for h in range(MEM_HEADS):
        sl = slice(h * MEM_HD, (h + 1) * MEM_HD)
        k = _rms(kv[:, sl], gk_ref[...])
        k_ref[0, :, h, :] = k
        kb_ref[:, sl] = k.astype(BF16)
        v_ref[0, :, h, :] = kv[:, D_MODEL + h * MEM_HD:D_MODEL + (h + 1) * MEM_HD]
    vb_ref[...] = kv[:, D_MODEL:].astype(BF16)


def _memory_kv(mem, g_mem, w_mkv, g_mk, *, batch, n_mem):
    out = pl.BlockSpec((1, n_mem, MEM_HEADS, MEM_HD), lambda i: (i, 0, 0, 0))
    flat = pl.BlockSpec((n_mem, D_MODEL), lambda i: (i, 0))
    return pl.pallas_call(
        _memkv_kernel,
        grid=(batch,),
        in_specs=[
            pl.BlockSpec((n_mem, D_MODEL), lambda i: (i, 0)),
            pl.BlockSpec((1, D_MODEL), lambda i: (0, 0)),
            pl.BlockSpec((D_MODEL, 2 * D_MODEL), lambda i: (0, 0)),
            pl.BlockSpec((1, MEM_HD), lambda i: (0, 0)),
        ],
        out_specs=[out, out, flat, flat],
        out_shape=[jax.ShapeDtypeStruct((batch, n_mem, MEM_HEADS, MEM_HD), F32)] * 2
        + [jax.ShapeDtypeStruct((batch * n_mem, D_MODEL), BF16)] * 2,
        compiler_params=_params("arbitrary"),
        name="memory_kv",
    )(mem, g_mem.reshape(1, -1), w_mkv.astype(BF16), g_mk.reshape(1, -1))


def _half_head_rms(acc, g_ref, scale):
    lane = lax.broadcasted_iota(jnp.int32, (1, LANE), 1)
    lo = lane < DF_HD
    outs = []
    for h in range(DF_HEADS):
        sl = slice(h * LANE, (h + 1) * LANE)
        xh = acc[:, sl]
        sq = xh * xh
        s_lo = jnp.sum(jnp.where(lo, sq, 0.0), axis=-1, keepdims=True)
        s_hi = jnp.sum(jnp.where(lo, 0.0, sq), axis=-1, keepdims=True)
        r = jnp.where(lo, lax.rsqrt(s_lo * (1.0 / DF_HD) + EPS), lax.rsqrt(s_hi * (1.0 / DF_HD) + EPS))
        outs.append(xh * r * (g_ref[:, sl] * scale))
    return outs


def _inproj_kernel(x_ref, gn_ref, w_ref, lbl_ref, gq_ref, gk_ref, gmq_ref,
                   p_ref, logf_ref, k_ref, v_ref, xn_scr, *, k_transposed):
    xn_scr[...] = _rms(x_ref[...], gn_ref[...]).astype(BF16)

    def proj(j):
        return _dot(xn_scr[...], w_ref[:, j * D_MODEL:(j + 1) * D_MODEL])

    def put(j, val):
        p_ref[:, j * D_MODEL:(j + 1) * D_MODEL] = val.astype(BF16)

    def passthrough(j):
        put(j, proj(j))

    def forget():
        l = lbl_ref[...]
        e = jnp.exp(l - jnp.max(l, axis=0, keepdims=True))
        lb = e[0:1] / jnp.sum(e, axis=0, keepdims=True)
        sp = _sigmoid(proj(CG_HF))
        logf_ref[...] = jnp.log(lb + (1.0 - lb) * sp)
        put(CG_HF, (1.0 - lb) * (1.0 - sp))

    def silu(j):
        acc = proj(j)
        put(j, acc * _sigmoid(acc))

    def gate(j):
        put(j, _sigmoid(proj(j)))

    def diff_q():
        outs = _half_head_rms(proj(CG_DQ), gq_ref, DF_HD ** -0.5 * LOG2E)
        for h in range(DF_HEADS):
            p_ref[:, CG_DQ * D_MODEL + h * LANE:CG_DQ * D_MODEL + (h + 1) * LANE] = outs[h].astype(BF16)

    def diff_k():
        outs = _half_head_rms(proj(CG_DK), gk_ref, 1.0)
        for h in range(DF_HEADS):
            sl = slice(h * LANE, (h + 1) * LANE)
            p_ref[:, CG_DK * D_MODEL + h * LANE:CG_DK * D_MODEL + (h + 1) * LANE] = outs[h].astype(BF16)
            if k_transposed:
                k_ref[0, sl, :] = outs[h].T
            else:
                k_ref[:, sl] = outs[h]

    def diff_v():
        acc = proj(CG_DV)
        put(CG_DV, acc)
        v_ref[...] = acc

    def mem_q():
        acc = proj(CG_MQ)
        for h in range(MEM_HEADS):
            sl = slice(h * MEM_HD, (h + 1) * MEM_HD)
            p_ref[:, CG_MQ * D_MODEL + h * MEM_HD:CG_MQ * D_MODEL + (h + 1) * MEM_HD] = (
                _rms(acc[:, sl], gmq_ref[...]) * (MEM_HD ** -0.5 * LOG2E)).astype(BF16)

    diff_k()
    passthrough(CG_HI)
    diff_q()
    diff_v()
    forget()
    gate(CG_G0)
    mem_q()
    silu(CG_ZA)
    gate(CG_G0 + 1)
    silu(CG_ZB)
    gate(CG_G0 + 2)
    silu(CG_ZM)
    passthrough(CG_HQ)


def _in_proj(x, g_norm, w_in, lb_logits, g_dq, g_dk, g_mq, *, seq, k_transposed):
    n = x.shape[0]
    tm = min(256, seq) if k_transposed else min(256, n)
    row = lambda i: (i, 0)
    const = lambda i: (0, 0)
    if k_transposed:
        per_b = seq // tm
        k_spec = pl.BlockSpec((1, D_MODEL, tm), lambda i: (i // per_b, 0, i % per_b))
        k_shape = jax.ShapeDtypeStruct((n // seq, D_MODEL, seq), F32)
    else:
        k_spec = pl.BlockSpec((tm, D_MODEL), row)
        k_shape = jax.ShapeDtypeStruct((n, D_MODEL), F32)
    return pl.pallas_call(
        functools.partial(_inproj_kernel, k_transposed=k_transposed),
        grid=(n // tm,),
        in_specs=[
            pl.BlockSpec((tm, D_MODEL), row),
            pl.BlockSpec((1, D_MODEL), const),
            pl.BlockSpec((D_MODEL, N_COLGROUPS * D_MODEL), lambda i: (0, 0), pipeline_mode=pl.Buffered(1)),
            pl.BlockSpec((2, D_MODEL), const),
            pl.BlockSpec((1, D_MODEL), const),
            pl.BlockSpec((1, D_MODEL), const),
            pl.BlockSpec((1, MEM_HD), const),
        ],
        out_specs=[
            pl.BlockSpec((tm, N_COLGROUPS * D_MODEL), row),
            pl.BlockSpec((tm, D_MODEL), row),
            k_spec,
            pl.BlockSpec((tm, D_MODEL), row),
        ],
        out_shape=[
            jax.ShapeDtypeStruct((n, N_COLGROUPS * D_MODEL), BF16),
            jax.ShapeDtypeStruct((n, D_MODEL), F32),
            k_shape,
            jax.ShapeDtypeStruct((n, D_MODEL), F32),
        ],
        scratch_shapes=[pltpu.VMEM((tm, D_MODEL), BF16)],
        compiler_params=_params("arbitrary", vmem=VMEM_LIMIT_INPROJ),
        name="in_proj",
    )(x, g_norm.reshape(1, -1), w_in, lb_logits,
      jnp.tile(g_dq, 2 * DF_HEADS).reshape(1, -1), jnp.tile(g_dk, 2 * DF_HEADS).reshape(1, -1), g_mq.reshape(1, -1))


def _split_bf16(x):
    hi = x.astype(BF16)
    return hi, (x - hi.astype(F32)).astype(BF16)


def _hgrn_kernel(q_ref, k_ref, v_ref, z_ref, lf_ref, s0_ref, g_ref, y_ref, s_ref, st_scr, *, seq, tc, hp):
    nb = tc // GLA_BLOCK
    row = lax.broadcasted_iota(jnp.int32, (tc, tc), 0)
    col = lax.broadcasted_iota(jnp.int32, (tc, tc), 1)
    causal = col <= row
    tril = jnp.where(causal, 1.0, 0.0).astype(BF16)
    rowk = lax.broadcasted_iota(jnp.int32, (tc, LANE), 0)

    @pl.when(pl.program_id(1) == 0)
    def _():
        for h in range(hp):
            st_scr[h] = s0_ref[0, h].T

    n_chunks = seq // tc
    unroll = 4 if n_chunks % 4 == 0 else 1

    stacked_att = tc % LANE == 0
    tril2 = jnp.concatenate([tril, tril], axis=1)

    def chunks(c, carry):
        heads = [slice(h * LANE, (h + 1) * LANE) for h in range(hp)]
        for u in range(unroll):
            rows = pl.ds(pl.multiple_of((c * unroll + u) * tc, tc), tc)
            g_hi, g_lo = _split_bf16(lf_ref[rows, :])
            b_all = _dot(tril2, jnp.concatenate([g_hi, g_lo], axis=0))
            bs = [b_all[:, sl] for sl in heads]
            qs = [q_ref[rows, sl].astype(F32) for sl in heads]
            ks = [k_ref[rows, sl].astype(F32) for sl in heads]
            vs = [v_ref[rows, sl] for sl in heads]

            q_ins, sv_t = [], []
            for h in range(hp):
                b, st = bs[h], st_scr[h]
                b_end = b[tc - 1:tc, :]
                v_t = vs[h].astype(F32).T.astype(BF16)
                q_ins.append((qs[h] * jnp.exp(b)).astype(BF16))
                sv_t.append(jnp.concatenate([st.astype(BF16), v_t], axis=1))
                k_dec = (ks[h] * jnp.exp(b_end - b)).astype(BF16)
                st_scr[h] = jnp.exp(b_end) * st + _dot(v_t, k_dec)

            atts = []
            for h in range(hp):
                b, q, k = bs[h], qs[h], ks[h]
                starts = [b[n * GLA_BLOCK - 1:n * GLA_BLOCK, :] if n else jnp.zeros((1, LANE), F32) for n in range(nb)]
                k_inv = [(k * jnp.exp(jnp.where(rowk < (n + 1) * GLA_BLOCK, starts[n] - b, 0.0))).astype(BF16)
                         for n in range(nb)]
                if stacked_att:
                    g_rows = jnp.concatenate([jnp.broadcast_to(g, (GLA_BLOCK, LANE)) for g in starts], axis=0)
                    q_dec = (q * jnp.exp(b - g_rows)).astype(BF16)
                    wide = _dot_nt(q_dec, jnp.concatenate(k_inv, axis=0))
                    att = [wide[n * GLA_BLOCK:(n + 1) * GLA_BLOCK, n * tc:(n + 1) * tc] for n in range(nb)]
                else:
                    att = [_dot_nt((q[n * GLA_BLOCK:(n + 1) * GLA_BLOCK]
                                    * jnp.exp(b[n * GLA_BLOCK:(n + 1) * GLA_BLOCK] - starts[n])).astype(BF16), k_inv[n])
                           for n in range(nb)]
                atts.append(jnp.where(causal, jnp.concatenate(att, axis=0), 0.0).astype(BF16))

            for h in range(hp):
                o = _dot_nt(jnp.concatenate([q_ins[h], atts[h]], axis=1), sv_t[h])
                y = _rms(o, g_ref[...]) * z_ref[rows, heads[h]].astype(F32)
                y_ref[rows, heads[h]] = y.astype(BF16)
        return carry

    lax.fori_loop(0, n_chunks // unroll, chunks, 0)

    @pl.when(pl.program_id(1) == pl.num_programs(1) - 1)
    def _():
        for h in range(hp):
            s_ref[0, h] = st_scr[h].T


def _hgrn(p, logf, s0, g_hg_out, batch, seq):
    hp = HG_HEADS
    ts = min(1024, seq)
    tc = min(128, ts)
    per_b = seq // ts

    def col(cg):
        return pl.BlockSpec((ts, D_MODEL), lambda b, t: (b * per_b + t, cg))

    st = pl.BlockSpec((1, hp, HG_D, HG_D), lambda b, t: (b, 0, 0, 0))
    return pl.pallas_call(
        functools.partial(_hgrn_kernel, seq=ts, tc=tc, hp=hp),
        grid=(batch, per_b),
        in_specs=[col(CG_HQ), col(CG_HF), col(CG_HI), col(CG_ZA), col(0), st,
                  pl.BlockSpec((1, HG_D), lambda b, t: (0, 0))],
        out_specs=[col(0), st],
        out_shape=[jax.ShapeDtypeStruct((batch * seq, D_MODEL), BF16),
                   jax.ShapeDtypeStruct((batch, HG_HEADS, HG_D, HG_D), F32)],
        scratch_shapes=[pltpu.VMEM((hp, HG_D, HG_D), F32)],
        compiler_params=_params("arbitrary", "arbitrary"),
        name="hgrn",
    )(p, p, p, p, logf, s0, g_hg_out.reshape(1, -1))


def _lambda(lam_ref, lam_init):
    lv = lam_ref[...]
    return (jnp.exp(jnp.sum(lv[0:1] * lv[1:2], axis=-1, keepdims=True))
            - jnp.exp(jnp.sum(lv[2:3] * lv[3:4], axis=-1, keepdims=True)) + lam_init)


def _split_maps(q):
    first = lax.broadcasted_iota(jnp.int32, (1, LANE), 1) < DF_HD
    zero = jnp.zeros((), q.dtype)
    return jnp.concatenate([jnp.where(first, q, zero), jnp.where(first, zero, q)], axis=0)


def _diffattn_kernel(lam_ref, q_ref, k_ref, v_ref, z_ref, g_ref, y_ref, s_scr, p_scr, vt_scr,
                     *, n_q, tq, rc, hp, lam_init):
    w = 2 * tq
    lam = _lambda(lam_ref, lam_init)
    heads = [slice(h * LANE, (h + 1) * LANE) for h in range(hp)]
    for h, sl in enumerate(heads):
        vt_scr[h, 0:DF_VD, :] = v_ref[:, sl].astype(F32).T.astype(BF16)
        vt_scr[h, DF_VD:, :] = jnp.ones((ONES_ROWS, vt_scr.shape[2]), BF16)

    key = lax.broadcasted_iota(jnp.int32, (tq, w), 0)
    qry = lax.broadcasted_iota(jnp.int32, (tq, w), 1) & (tq - 1)
    bias = jnp.where((key // CHUNK) <= (qry // CHUNK), 0.0, -jnp.inf)

    def fold8(x, op):
        return op(x.reshape(rc // 8, 8, w), axis=0)

    def scores(h, i):
        n_vis = (i + 1) * tq
        s_buf = s_scr.at[2 * h + i % 2]
        s = _dot_nt(k_ref[0:n_vis, heads[h]], _split_maps(q_ref[i * tq:(i + 1) * tq, heads[h]]))
        m8 = None
        for c in range(n_vis // rc):
            blk = s[c * rc:(c + 1) * rc]
            if (c + 1) * rc > n_vis - tq:
                blk = blk + bias[c * rc - (n_vis - tq):(c + 1) * rc - (n_vis - tq)]
            s_buf[c * rc:(c + 1) * rc, :] = blk
            m8 = fold8(blk, jnp.max) if m8 is None else jnp.maximum(m8, fold8(blk, jnp.max))
        return jnp.max(m8, axis=0, keepdims=True)

    def probs(h, i, m):
        s_buf, p_buf = s_scr.at[2 * h + i % 2], p_scr.at[2 * h + i % 2]
        for c in range((i + 1) * tq // rc):
            rows = slice(c * rc, (c + 1) * rc)
            p_buf[rows, :] = jnp.exp2(s_buf[rows, :] - m).astype(BF16)

    maxes = {}
    for i in range(min(2, n_q)):
        for h in range(hp):
            maxes[h, i] = scores(h, i)
    for h in range(hp):
        probs(h, 0, maxes[h, 0])
    for i in range(n_q):
        n_vis = (i + 1) * tq
        rows_q = slice(i * tq, (i + 1) * tq)
        for h in range(hp):
            if i + 2 < n_q:
                maxes[h, i + 2] = scores(h, i + 2)
            pv = _dot(vt_scr[h, :, 0:n_vis], p_scr[2 * h + i % 2, 0:n_vis, :])
            if i + 1 < n_q:
                probs(h, i + 1, maxes[h, i + 1])
            rinv = 1.0 / pv[DF_VD:DF_VD + 1, :]
            o = (pv[:DF_VD, :tq] * rinv[:, :tq] - pv[:DF_VD, tq:] * (rinv[:, tq:] * lam)).T
            y = _rms(o, g_ref[...]) * (1.0 - lam_init) * z_ref[rows_q, heads[h]].astype(F32)
            y_ref[rows_q, heads[h]] = y.astype(BF16)


def _diff_attn(lam_vecs, p, g_dsub, *, batch, seq, lam_init):
    tq = rc = 256
    hp = 1
    nhp = DF_HEADS // hp

    def col(cg):
        return pl.BlockSpec((seq, hp * LANE), lambda b, h: (b, cg * nhp + h))

    return pl.pallas_call(
        functools.partial(_diffattn_kernel, n_q=seq // tq, tq=tq, rc=rc, hp=hp, lam_init=lam_init),
        grid=(batch, nhp),
        in_specs=[pl.BlockSpec((4, DF_HD), lambda b, h: (0, 0)),
                  col(CG_DQ), col(CG_DK), col(CG_DV), col(CG_ZB),
                  pl.BlockSpec((1, DF_VD), lambda b, h: (0, 0))],
        out_specs=pl.BlockSpec((seq, hp * LANE), lambda b, h: (b, h)),
        out_shape=jax.ShapeDtypeStruct((batch * seq, D_MODEL), BF16),
        scratch_shapes=[pltpu.VMEM((2 * hp, seq, 2 * tq), F32), pltpu.VMEM((2 * hp, seq, 2 * tq), BF16),
                        pltpu.VMEM((hp, DF_VD + ONES_ROWS, seq), BF16)],
        compiler_params=_params("arbitrary", "arbitrary"),
        name="diff_attn",
    )(lam_vecs, p, p, p, p, g_dsub.reshape(1, -1))


def _cached_diffattn_kernel(lam_ref, q_ref, kn_ref, vn_ref, z_ref, g_ref, kt_ref, vc_ref, y_ref,
                            *, tq, past, lam_init):
    lam = _lambda(lam_ref, lam_init)
    heads = [slice(h * LANE, (h + 1) * LANE) for h in range(DF_HEADS)]
    scores = []
    for sl in heads:
        qs = _split_maps(q_ref[:, sl])
        scores.append((_dot(qs, kt_ref[0, sl, :].astype(BF16)),
                       _dot_nt(qs, kn_ref[:, sl])))
    for h, sl in enumerate(heads):
        s_c, s_n = scores[h]
        m = jnp.maximum(jnp.max(s_c, axis=-1, keepdims=True), jnp.max(s_n, axis=-1, keepdims=True))
        p_c = jnp.exp2(s_c - m)
        p_n = jnp.exp2(s_n - m)
        rinv = 1.0 / (jnp.sum(p_c, axis=-1, keepdims=True) + jnp.sum(p_n, axis=-1, keepdims=True))
        r1 = rinv[:tq]
        r2 = rinv[tq:] * lam
        a_c = (p_c[:tq] * r1 - p_c[tq:] * r2).astype(BF16)
        a_n = (p_n[:tq] * r1 - p_n[tq:] * r2).astype(BF16)
        v_c = vc_ref[0, pl.ds(h, past, stride=DF_HEADS), :].astype(BF16)
        o = _dot(a_c, v_c) + _dot(a_n, vn_ref[:, sl])
        y = _rms(o, g_ref[...]) * (1.0 - lam_init) * z_ref[:, sl].astype(F32)
        y_ref[:, sl] = y.astype(BF16)


def _cached_diff_attn(lam_vecs, p, g_dsub, kt_cache, v_cache, *, batch, seq, lam_init):
    past = v_cache.shape[1]

    def col(cg):
        return pl.BlockSpec((seq, D_MODEL), lambda b: (b, cg))

    return pl.pallas_call(
        functools.partial(_cached_diffattn_kernel, tq=seq, past=past, lam_init=lam_init),
        grid=(batch,),
        in_specs=[pl.BlockSpec((4, DF_HD), lambda b: (0, 0)),
                  col(CG_DQ), col(CG_DK), col(CG_DV), col(CG_ZB),
                  pl.BlockSpec((1, DF_VD), lambda b: (0, 0)),
                  pl.BlockSpec((1, D_MODEL, past), lambda b: (b, 0, 0)),
                  pl.BlockSpec((1, past * DF_HEADS, DF_VD), lambda b: (b, 0, 0))],
        out_specs=pl.BlockSpec((seq, D_MODEL), lambda b: (b, 0)),
        out_shape=jax.ShapeDtypeStruct((batch * seq, D_MODEL), BF16),
        compiler_params=_params("arbitrary"),
        name="cached_diff_attn",
    )(lam_vecs, p, p, p, p, g_dsub.reshape(1, -1), kt_cache, v_cache.reshape(batch, past * DF_HEADS, DF_VD))


def _merge_kernel(x_ref, ya_ref, yb_ref, qm_ref, zm_ref, g0_ref, g1_ref, g2_ref, mk_ref, mv_ref,
                  wb_ref, wo_ref, y_ref, *, nb, n_mem):
    heads = [slice(h * MEM_HD, (h + 1) * MEM_HD) for h in range(MEM_HEADS)]
    sub = x_ref.shape[0] // nb
    parts = [(slice(j * sub, (j + 1) * sub), slice(j * n_mem, (j + 1) * n_mem)) for j in range(nb)]
    scores = [[_dot_nt(qm_ref[rows, sl], mk_ref[mem, sl]) for sl in heads]
              for rows, mem in parts]
    hsum = (g0_ref[...].astype(F32) * _dot(ya_ref[...], wb_ref[0])
            + g1_ref[...].astype(F32) * _dot(yb_ref[...], wb_ref[1]))
    ym = []
    for (rows, mem), part_scores in zip(parts, scores):
        ym_part = []
        for s, sl in zip(part_scores, heads):
            p = jnp.exp2(s - jnp.max(s, axis=-1, keepdims=True))
            rinv = 1.0 / jnp.sum(p, axis=-1, keepdims=True)
            o = _dot(p.astype(BF16), mv_ref[mem, sl]) * rinv
            ym_part.append((o * zm_ref[rows, sl].astype(F32)).astype(BF16))
        ym.append(jnp.concatenate(ym_part, axis=-1))
    ym = ym[0] if nb == 1 else jnp.concatenate(ym, axis=0)
    hsum = hsum + g2_ref[...].astype(F32) * _dot(ym, wb_ref[2])
    y_ref[...] = x_ref[...] + _dot(hsum.astype(BF16), wo_ref[...])


def _merge(x, ya, yb, p, mem_k, mem_v, w_branch, w_out, *, seq, n_mem):
    n = x.shape[0]
    tm = min(512, n)
    nb = max(1, tm // seq)
    per_b = max(1, seq // tm)
    row = lambda i: (i, 0)

    def col(cg):
        return pl.BlockSpec((tm, D_MODEL), lambda i: (i, cg))

    mem = pl.BlockSpec((nb * n_mem, D_MODEL), lambda i: (i // per_b, 0))
    return pl.pallas_call(
        functools.partial(_merge_kernel, nb=nb, n_mem=n_mem),
        grid=(n // tm,),
        in_specs=[pl.BlockSpec((tm, D_MODEL), row), pl.BlockSpec((tm, D_MODEL), row), pl.BlockSpec((tm, D_MODEL), row),
                  col(CG_MQ), col(CG_ZM), col(CG_G0), col(CG_G0 + 1), col(CG_G0 + 2), mem, mem,
                  pl.BlockSpec((N_BRANCH, D_MODEL, D_MODEL), lambda i: (0, 0, 0), pipeline_mode=pl.Buffered(1)),
                  pl.BlockSpec((D_MODEL, D_MODEL), lambda i: (0, 0), pipeline_mode=pl.Buffered(1))],
        out_specs=pl.BlockSpec((tm, D_MODEL), row),
        out_shape=jax.ShapeDtypeStruct((n, D_MODEL), F32),
        compiler_params=_params("arbitrary"),
        name="merge",
    )(x, ya, yb, p, p, p, p, p, mem_k, mem_v, w_branch, w_out)


def _layer(x, s0, past_k, past_v, mem_k, mem_v, lam_init, w, n_mem):
    batch, seq, _ = x.shape
    n = batch * seq
    xf = x.reshape(n, D_MODEL)
    prompt = past_k is None
    p, logf, k32, v32 = _in_proj(xf, w["g_norm"], w["w_in"], w["lb_logits"], w["g_dq"], w["g_dk"], w["g_mq"],
                                 seq=seq, k_transposed=prompt)
    ya, s_new = _hgrn(p, logf, s0, w["g_hg_out"], batch, seq)
    if prompt:
        yb = _diff_attn(w["lam_vecs"], p, w["g_dsub"], batch=batch, seq=seq, lam_init=lam_init)
    else:
        past = past_k.shape[1]
        kt_cache = jnp.transpose(past_k, (0, 2, 3, 4, 1)).reshape(batch, D_MODEL, past)
        yb = _cached_diff_attn(w["lam_vecs"], p, w["g_dsub"], kt_cache, past_v, batch=batch, seq=seq,
                               lam_init=lam_init)
    y = _merge(xf, ya, yb, p, mem_k, mem_v, w["w_branch"], w["w_out"], seq=seq, n_mem=n_mem)
    if prompt:
        k_out = jnp.transpose(k32.reshape(batch, DF_HEADS, 2, DF_HD, seq), (0, 4, 1, 2, 3))
    else:
        k_out = k32.reshape(batch, seq, DF_HEADS, 2, DF_HD)
    return y.reshape(batch, seq, D_MODEL), s_new, k_out, v32.reshape(batch, seq, DF_HEADS, DF_VD)


def kernel(x_prompt, x_sample, mem_prompt, cache_diff_k, cache_diff_v, cache_mem_k, cache_mem_v, state_hgrn,
           g_norm, w_in, hg_lb_logits, g_hg_out, g_dq, g_dk, lam_q1, lam_k1, lam_q2, lam_k2, g_dsub, g_mem,
           w_mkv, g_mq, g_mk, w_branch, w_out):
    depth = g_norm.shape[0]
    assert depth == 1 and hg_lb_logits.shape[0] == 2, "single-layer step only"
    batch, n_mem = mem_prompt.shape[:2]
    lam_init = 0.8 - 0.6
    w = dict(g_norm=g_norm[0], w_in=w_in[0].astype(BF16), lb_logits=hg_lb_logits, g_hg_out=g_hg_out[0],
             g_dq=g_dq[0], g_dk=g_dk[0], g_mq=g_mq[0], g_dsub=g_dsub[0],
             lam_vecs=jnp.stack([lam_q1[0], lam_k1[0], lam_q2[0], lam_k2[0]]),
             w_branch=w_branch[0].astype(BF16), w_out=w_out[0].astype(BF16))

    mem_k, mem_v, mem_kb, mem_vb = _memory_kv(mem_prompt.reshape(batch * n_mem, D_MODEL), g_mem[0], w_mkv[0], g_mk[0],
                                              batch=batch, n_mem=n_mem)
    s0 = jnp.zeros((batch, HG_HEADS, HG_D, HG_D), F32)
    y_p, s_p, k_p, v_p = _layer(x_prompt, s0, None, None, mem_kb, mem_vb, lam_init, w, n_mem)
    y_s, s_s, k_s, v_s = _layer(x_sample, state_hgrn[0], cache_diff_k[0], cache_diff_v[0],
                                cache_mem_k[0].reshape(-1, D_MODEL).astype(BF16),
                                cache_mem_v[0].reshape(-1, D_MODEL).astype(BF16), lam_init, w, n_mem)
    return (y_p, y_s, s_p[None], s_s[None], k_p[None], v_p[None], k_s[None], v_s[None], mem_k[None], mem_v[None])
```

```python
import functools

import jax
import jax.numpy as jnp
from jax import lax
from jax.experimental import pallas as pl
from jax.experimental.pallas import tpu as pltpu

F32 = jnp.float32
BF16 = jnp.bfloat16

EPS = 1e-6
LOG2E = 1.4426950408889634
D_MODEL = 1024
CHUNK = 64
GLA_BLOCK = 16
HG_HEADS, HG_D = 8, 128
DF_HEADS, DF_HD, DF_VD = 8, 64, 128
MEM_HEADS, MEM_HD = 4, 256
N_BRANCH = 3
N_COLGROUPS = 13
(CG_HQ, CG_HF, CG_HI, CG_ZA, CG_DQ, CG_DK, CG_DV, CG_ZB, CG_MQ, CG_ZM, CG_G0) = range(11)

LANE = 128
ONES_ROWS = 16
VMEM_LIMIT = 52 * 1024 * 1024
VMEM_LIMIT_INPROJ = 60 * 1024 * 1024

NT = (((1,), (1,)), ((), ()))
TN = (((0,), (0,)), ((), ()))


def _dot(a, b):
    return jnp.dot(a, b, preferred_element_type=F32)


def _dot_nt(a, b):
    return lax.dot_general(a, b, NT, preferred_element_type=F32)


def _dot_tn(a, b):
    return lax.dot_general(a, b, TN, preferred_element_type=F32)


def _sigmoid(x):
    return 0.5 * jnp.tanh(0.5 * x) + 0.5


def _rms(x, g):
    return x * lax.rsqrt(jnp.mean(x * x, axis=-1, keepdims=True) + EPS) * g


def _params(*sem, vmem=VMEM_LIMIT):
    return pltpu.CompilerParams(dimension_semantics=sem, vmem_limit_bytes=vmem)


def _memkv_kernel(mem_ref, g_ref, w_ref, gk_ref, k_ref, v_ref, kb_ref, vb_ref):
    xn = _rms(mem_ref[...], g_ref[...]).astype(BF16)
    kv = _dot(xn, w_ref[...])
    for h in range(MEM_HEADS):
        sl = slice(h * MEM_HD, (h + 1) * MEM_HD)
        k = _rms(kv[:, sl], gk_ref[...])
        k_ref[0, :, h, :] = k
        kb_ref[:, sl] = k.astype(BF16)
        v_ref[0, :, h, :] = kv[:, D_MODEL + h * MEM_HD:D_MODEL + (h + 1) * MEM_HD]
    vb_ref[...] = kv[:, D_MODEL:].astype(BF16)


def _memory_kv(mem, g_mem, w_mkv, g_mk, *, batch, n_mem):
    out = pl.BlockSpec((1, n_mem, MEM_HEADS, MEM_HD), lambda i: (i, 0, 0, 0))
    flat = pl.BlockSpec((n_mem, D_MODEL), lambda i: (i, 0))
    return pl.pallas_call(
        _memkv_kernel,
        grid=(batch,),
        in_specs=[
            pl.BlockSpec((n_mem, D_MODEL), lambda i: (i, 0)),
            pl.BlockSpec((1, D_MODEL), lambda i: (0, 0)),
            pl.BlockSpec((D_MODEL, 2 * D_MODEL), lambda i: (0, 0)),
            pl.BlockSpec((1, MEM_HD), lambda i: (0, 0)),
        ],
        out_specs=[out, out, flat, flat],
        out_shape=[jax.ShapeDtypeStruct((batch, n_mem, MEM_HEADS, MEM_HD), F32)] * 2
        + [jax.ShapeDtypeStruct((batch * n_mem, D_MODEL), BF16)] * 2,
        compiler_params=_params("arbitrary"),
        name="memory_kv",
    )(mem, g_mem.reshape(1, -1), w_mkv.astype(BF16), g_mk.reshape(1, -1))


def _half_head_rms(acc, g_ref, scale):
    lane = lax.broadcasted_iota(jnp.int32, (1, LANE), 1)
    lo = lane < DF_HD
    outs = []
    for h in range(DF_HEADS):
        sl = slice(h * LANE, (h + 1) * LANE)
        xh = acc[:, sl]
        sq = xh * xh
        s_lo = jnp.sum(jnp.where(lo, sq, 0.0), axis=-1, keepdims=True)
        s_hi = jnp.sum(jnp.where(lo, 0.0, sq), axis=-1, keepdims=True)
        r = jnp.where(lo, lax.rsqrt(s_lo * (1.0 / DF_HD) + EPS), lax.rsqrt(s_hi * (1.0 / DF_HD) + EPS))
        outs.append(xh * r * (g_ref[:, sl] * scale))
    return outs


def _inproj_kernel(x_ref, gn_ref, w_ref, lbl_ref, gq_ref, gk_ref, gmq_ref,
                   p_ref, logf_ref, k_ref, v_ref, xn_scr, *, k_transposed):
    xn_scr[...] = _rms(x_ref[...], gn_ref[...]).astype(BF16)

    def proj(j):
        return _dot(xn_scr[...], w_ref[:, j * D_MODEL:(j + 1) * D_MODEL])

    def put(j, val):
        p_ref[:, j * D_MODEL:(j + 1) * D_MODEL] = val.astype(BF16)

    def passthrough(j):
        put(j, proj(j))

    def forget():
        l = lbl_ref[...]
        e = jnp.exp(l - jnp.max(l, axis=0, keepdims=True))
        lb = e[0:1] / jnp.sum(e, axis=0, keepdims=True)
        sp = _sigmoid(proj(CG_HF))
        logf_ref[...] = jnp.log(lb + (1.0 - lb) * sp)
        put(CG_HF, (1.0 - lb) * (1.0 - sp))

    def silu(j):
        acc = proj(j)
        put(j, acc * _sigmoid(acc))

    def gate(j):
        put(j, _sigmoid(proj(j)))

    def diff_q():
        outs = _half_head_rms(proj(CG_DQ), gq_ref, DF_HD ** -0.5 * LOG2E)
        for h in range(DF_HEADS):
            p_ref[:, CG_DQ * D_MODEL + h * LANE:CG_DQ * D_MODEL + (h + 1) * LANE] = outs[h].astype(BF16)

    def diff_k():
        outs = _half_head_rms(proj(CG_DK), gk_ref, 1.0)
        for h in range(DF_HEADS):
            sl = slice(h * LANE, (h + 1) * LANE)
            p_ref[:, CG_DK * D_MODEL + h * LANE:CG_DK * D_MODEL + (h + 1) * LANE] = outs[h].astype(BF16)
            if k_transposed:
                k_ref[0, sl, :] = outs[h].T
            else:
                k_ref[:, sl] = outs[h]

    def diff_v():
        acc = proj(CG_DV)
        put(CG_DV, acc)
        v_ref[...] = acc

    def mem_q():
        acc = proj(CG_MQ)
        for h in range(MEM_HEADS):
            sl = slice(h * MEM_HD, (h + 1) * MEM_HD)
            p_ref[:, CG_MQ * D_MODEL + h * MEM_HD:CG_MQ * D_MODEL + (h + 1) * MEM_HD] = (
                _rms(acc[:, sl], gmq_ref[...]) * (MEM_HD ** -0.5 * LOG2E)).astype(BF16)

    diff_k()
    passthrough(CG_HI)
    diff_q()
    diff_v()
    forget()
    gate(CG_G0)
    mem_q()
    silu(CG_ZA)
    gate(CG_G0 + 1)
    silu(CG_ZB)
    gate(CG_G0 + 2)
    silu(CG_ZM)
    passthrough(CG_HQ)


def _in_proj(x, g_norm, w_in, lb_logits, g_dq, g_dk, g_mq, *, seq, k_transposed):
    n = x.shape[0]
    tm = min(256, seq) if k_transposed else min(256, n)
    row = lambda i: (i, 0)
    const = lambda i: (0, 0)
    if k_transposed:
        per_b = seq // tm
        k_spec = pl.BlockSpec((1, D_MODEL, tm), lambda i: (i // per_b, 0, i % per_b))
        k_shape = jax.ShapeDtypeStruct((n // seq, D_MODEL, seq), F32)
    else:
        k_spec = pl.BlockSpec((tm, D_MODEL), row)
        k_shape = jax.ShapeDtypeStruct((n, D_MODEL), F32)
    return pl.pallas_call(
        functools.partial(_inproj_kernel, k_transposed=k_transposed),
        grid=(n // tm,),
        in_specs=[
            pl.BlockSpec((tm, D_MODEL), row),
            pl.BlockSpec((1, D_MODEL), const),
            pl.BlockSpec((D_MODEL, N_COLGROUPS * D_MODEL), lambda i: (0, 0), pipeline_mode=pl.Buffered(1)),
            pl.BlockSpec((2, D_MODEL), const),
            pl.BlockSpec((1, D_MODEL), const),
            pl.BlockSpec((1, D_MODEL), const),
            pl.BlockSpec((1, MEM_HD), const),
        ],
        out_specs=[
            pl.BlockSpec((tm, N_COLGROUPS * D_MODEL), row),
            pl.BlockSpec((tm, D_MODEL), row),
            k_spec,
            pl.BlockSpec((tm, D_MODEL), row),
        ],
        out_shape=[
            jax.ShapeDtypeStruct((n, N_COLGROUPS * D_MODEL), BF16),
            jax.ShapeDtypeStruct((n, D_MODEL), F32),
            k_shape,
            jax.ShapeDtypeStruct((n, D_MODEL), F32),
        ],
        scratch_shapes=[pltpu.VMEM((tm, D_MODEL), BF16)],
        compiler_params=_params("arbitrary", vmem=VMEM_LIMIT_INPROJ),
        name="in_proj",
    )(x, g_norm.reshape(1, -1), w_in, lb_logits,
      jnp.tile(g_dq, 2 * DF_HEADS).reshape(1, -1), jnp.tile(g_dk, 2 * DF_HEADS).reshape(1, -1), g_mq.reshape(1, -1))


def _split_bf16(x):
    hi = x.astype(BF16)
    return hi, (x - hi.astype(F32)).astype(BF16)


def _hgrn_kernel(q_ref, k_ref, v_ref, z_ref, lf_ref, s0_ref, g_ref, y_ref, s_ref, st_scr, *, seq, tc, hp):
    nb = tc // GLA_BLOCK
    row = lax.broadcasted_iota(jnp.int32, (tc, tc), 0)
    col = lax.broadcasted_iota(jnp.int32, (tc, tc), 1)
    causal = col <= row
    tril = jnp.where(causal, 1.0, 0.0).astype(BF16)
    rowk = lax.broadcasted_iota(jnp.int32, (tc, LANE), 0)

    @pl.when(pl.program_id(1) == 0)
    def _():
        for h in range(hp):
            st_scr[h] = s0_ref[0, h].T

    n_chunks = seq // tc
    unroll = 4 if n_chunks % 4 == 0 else 1

    stacked_att = tc % LANE == 0
    tril2 = jnp.concatenate([tril, tril], axis=1)

    def chunks(c, carry):
        heads = [slice(h * LANE, (h + 1) * LANE) for h in range(hp)]
        for u in range(unroll):
            rows = pl.ds(pl.multiple_of((c * unroll + u) * tc, tc), tc)
            g_hi, g_lo = _split_bf16(lf_ref[rows, :])
            b_all = _dot(tril2, jnp.concatenate([g_hi, g_lo], axis=0))
            bs = [b_all[:, sl] for sl in heads]
            qs = [q_ref[rows, sl].astype(F32) for sl in heads]
            ks = [k_ref[rows, sl].astype(F32) for sl in heads]
            vs = [v_ref[rows, sl] for sl in heads]

            q_ins, sv_t = [], []
            for h in range(hp):
                b, st = bs[h], st_scr[h]
                b_end = b[tc - 1:tc, :]
                v_t = vs[h].astype(F32).T.astype(BF16)
                q_ins.append((qs[h] * jnp.exp(b)).astype(BF16))
                sv_t.append(jnp.concatenate([st.astype(BF16), v_t], axis=1))
                k_dec = (ks[h] * jnp.exp(b_end - b)).astype(BF16)
                st_scr[h] = jnp.exp(b_end) * st + _dot(v_t, k_dec)

            atts = []
            for h in range(hp):
                b, q, k = bs[h], qs[h], ks[h]
                starts = [b[n * GLA_BLOCK - 1:n * GLA_BLOCK, :] if n else jnp.zeros((1, LANE), F32) for n in range(nb)]
                k_inv = [(k * jnp.exp(jnp.where(rowk < (n + 1) * GLA_BLOCK, starts[n] - b, 0.0))).astype(BF16)
                         for n in range(nb)]
                if stacked_att:
                    g_rows = jnp.concatenate([jnp.broadcast_to(g, (GLA_BLOCK, LANE)) for g in starts], axis=0)
                    q_dec = (q * jnp.exp(b - g_rows)).astype(BF16)
                    wide = _dot_nt(q_dec, jnp.concatenate(k_inv, axis=0))
                    att = [wide[n * GLA_BLOCK:(n + 1) * GLA_BLOCK, n * tc:(n + 1) * tc] for n in range(nb)]
                else:
                    att = [_dot_nt((q[n * GLA_BLOCK:(n + 1) * GLA_BLOCK]
                                    * jnp.exp(b[n * GLA_BLOCK:(n + 1) * GLA_BLOCK] - starts[n])).astype(BF16), k_inv[n])
                           for n in range(nb)]
                atts.append(jnp.where(causal, jnp.concatenate(att, axis=0), 0.0).astype(BF16))

            for h in range(hp):
                o = _dot_nt(jnp.concatenate([q_ins[h], atts[h]], axis=1), sv_t[h])
                y = _rms(o, g_ref[...]) * z_ref[rows, heads[h]].astype(F32)
                y_ref[rows, heads[h]] = y.astype(BF16)
        return carry

    lax.fori_loop(0, n_chunks // unroll, chunks, 0)

    @pl.when(pl.program_id(1) == pl.num_programs(1) - 1)
    def _():
        for h in range(hp):
            s_ref[0, h] = st_scr[h].T


def _hgrn(p, logf, s0, g_hg_out, batch, seq):
    hp = HG_HEADS
    ts = min(1024, seq)
    tc = min(128, ts)
    per_b = seq // ts

    def col(cg):
        return pl.BlockSpec((ts, D_MODEL), lambda b, t: (b * per_b + t, cg))

    st = pl.BlockSpec((1, hp, HG_D, HG_D), lambda b, t: (b, 0, 0, 0))
    return pl.pallas_call(
        functools.partial(_hgrn_kernel, seq=ts, tc=tc, hp=hp),
        grid=(batch, per_b),
        in_specs=[col(CG_HQ), col(CG_HF), col(CG_HI), col(CG_ZA), col(0), st,
                  pl.BlockSpec((1, HG_D), lambda b, t: (0, 0))],
        out_specs=[col(0), st],
        out_shape=[jax.ShapeDtypeStruct((batch * seq, D_MODEL), BF16),
                   jax.ShapeDtypeStruct((batch, HG_HEADS, HG_D, HG_D), F32)],
        scratch_shapes=[pltpu.VMEM((hp, HG_D, HG_D), F32)],
        compiler_params=_params("arbitrary", "arbitrary"),
        name="hgrn",
    )(p, p, p, p, logf, s0, g_hg_out.reshape(1, -1))


def _lambda(lam_ref, lam_init):
    lv = lam_ref[...]
    return (jnp.exp(jnp.sum(lv[0:1] * lv[1:2], axis=-1, keepdims=True))
            - jnp.exp(jnp.sum(lv[2:3] * lv[3:4], axis=-1, keepdims=True)) + lam_init)


def _split_maps(q):
    first = lax.broadcasted_iota(jnp.int32, (1, LANE), 1) < DF_HD
    zero = jnp.zeros((), q.dtype)
    return jnp.concatenate([jnp.where(first, q, zero), jnp.where(first, zero, q)], axis=0)


def _diffattn_kernel(lam_ref, q_ref, k_ref, v_ref, z_ref, g_ref, y_ref, s_scr, p_scr, vt_scr,
                     *, n_q, tq, rc, hp, lam_init):
    w = 2 * tq
    lam = _lambda(lam_ref, lam_init)
    heads = [slice(h * LANE, (h + 1) * LANE) for h in range(hp)]
    for h, sl in enumerate(heads):
        vt_scr[h, 0:DF_VD, :] = v_ref[:, sl].astype(F32).T.astype(BF16)
        vt_scr[h, DF_VD:, :] = jnp.ones((ONES_ROWS, vt_scr.shape[2]), BF16)

    key = lax.broadcasted_iota(jnp.int32, (tq, w), 0)
    qry = lax.broadcasted_iota(jnp.int32, (tq, w), 1) & (tq - 1)
    bias = jnp.where((key // CHUNK) <= (qry // CHUNK), 0.0, -jnp.inf)

    def fold8(x, op):
        return op(x.reshape(rc // 8, 8, w), axis=0)

    def scores(h, i):
        n_vis = (i + 1) * tq
        s_buf = s_scr.at[2 * h + i % 2]
        s = _dot_nt(k_ref[0:n_vis, heads[h]], _split_maps(q_ref[i * tq:(i + 1) * tq, heads[h]]))
        m8 = None
        for c in range(n_vis // rc):
            blk = s[c * rc:(c + 1) * rc]
            if (c + 1) * rc > n_vis - tq:
                blk = blk + bias[c * rc - (n_vis - tq):(c + 1) * rc - (n_vis - tq)]
            s_buf[c * rc:(c + 1) * rc, :] = blk
            m8 = fold8(blk, jnp.max) if m8 is None else jnp.maximum(m8, fold8(blk, jnp.max))
        return jnp.max(m8, axis=0, keepdims=True)

    def probs(h, i, m):
        s_buf, p_buf = s_scr.at[2 * h + i % 2], p_scr.at[2 * h + i % 2]
        for c in range((i + 1) * tq // rc):
            rows = slice(c * rc, (c + 1) * rc)
            p_buf[rows, :] = jnp.exp2(s_buf[rows, :] - m).astype(BF16)

    maxes = {}
    for i in range(min(2, n_q)):
        for h in range(hp):
            maxes[h, i] = scores(h, i)
    for h in range(hp):
        probs(h, 0, maxes[h, 0])
    for i in range(n_q):
        n_vis = (i + 1) * tq
        rows_q = slice(i * tq, (i + 1) * tq)
        for h in range(hp):
            if i + 2 < n_q:
                maxes[h, i + 2] = scores(h, i + 2)
            pv = _dot(vt_scr[h, :, 0:n_vis], p_scr[2 * h + i % 2, 0:n_vis, :])
            if i + 1 < n_q:
                probs(h, i + 1, maxes[h, i + 1])
            rinv = 1.0 / pv[DF_VD:DF_VD + 1, :]
            o = (pv[:DF_VD, :tq] * rinv[:, :tq] - pv[:DF_VD, tq:] * (rinv[:, tq:] * lam)).T
            y = _rms(o, g_ref[...]) * (1.0 - lam_init) * z_ref[rows_q, heads[h]].astype(F32)
            y_ref[rows_q, heads[h]] = y.astype(BF16)


def _diff_attn(lam_vecs, p, g_dsub, *, batch, seq, lam_init):
    tq = rc = 256
    hp = 2
    nhp = DF_HEADS // hp

    def col(cg):
        return pl.BlockSpec((seq, hp * LANE), lambda b, h: (b, cg * nhp + h))

    return pl.pallas_call(
        functools.partial(_diffattn_kernel, n_q=seq // tq, tq=tq, rc=rc, hp=hp, lam_init=lam_init),
        grid=(batch, nhp),
        in_specs=[pl.BlockSpec((4, DF_HD), lambda b, h: (0, 0)),
                  col(CG_DQ), col(CG_DK), col(CG_DV), col(CG_ZB),
                  pl.BlockSpec((1, DF_VD), lambda b, h: (0, 0))],
        out_specs=pl.BlockSpec((seq, hp * LANE), lambda b, h: (b, h)),
        out_shape=jax.ShapeDtypeStruct((batch * seq, D_MODEL), BF16),
        scratch_shapes=[pltpu.VMEM((2 * hp, seq, 2 * tq), F32), pltpu.VMEM((2 * hp, seq, 2 * tq), BF16),
                        pltpu.VMEM((hp, DF_VD + ONES_ROWS, seq), BF16)],
        compiler_params=_params("arbitrary", "arbitrary"),
        name="diff_attn",
    )(lam_vecs, p, p, p, p, g_dsub.reshape(1, -1))


def _cached_diffattn_kernel(lam_ref, q_ref, kn_ref, vn_ref, z_ref, g_ref, kt_ref, vc_ref, y_ref,
                            *, tq, past, lam_init):
    lam = _lambda(lam_ref, lam_init)
    heads = [slice(h * LANE, (h + 1) * LANE) for h in range(DF_HEADS)]
    scores = []
    for sl in heads:
        qs = _split_maps(q_ref[:, sl])
        scores.append((_dot(qs, kt_ref[0, sl, :].astype(BF16)),
                       _dot_nt(qs, kn_ref[:, sl])))
    for h, sl in enumerate(heads):
        s_c, s_n = scores[h]
        m = jnp.maximum(jnp.max(s_c, axis=-1, keepdims=True), jnp.max(s_n, axis=-1, keepdims=True))
        p_c = jnp.exp2(s_c - m)
        p_n = jnp.exp2(s_n - m)
        rinv = 1.0 / (jnp.sum(p_c, axis=-1, keepdims=True) + jnp.sum(p_n, axis=-1, keepdims=True))
        r1 = rinv[:tq]
        r2 = rinv[tq:] * lam
        a_c = (p_c[:tq] * r1 - p_c[tq:] * r2).astype(BF16)
        a_n = (p_n[:tq] * r1 - p_n[tq:] * r2).astype(BF16)
        v_c = vc_ref[0, pl.ds(h, past, stride=DF_HEADS), :].astype(BF16)
        o = _dot(a_c, v_c) + _dot(a_n, vn_ref[:, sl])
        y = _rms(o, g_ref[...]) * (1.0 - lam_init) * z_ref[:, sl].astype(F32)
        y_ref[:, sl] = y.astype(BF16)


def _cached_diff_attn(lam_vecs, p, g_dsub, kt_cache, v_cache, *, batch, seq, lam_init):
    past = v_cache.shape[1]

    def col(cg):
        return pl.BlockSpec((seq, D_MODEL), lambda b: (b, cg))

    return pl.pallas_call(
        functools.partial(_cached_diffattn_kernel, tq=seq, past=past, lam_init=lam_init),
        grid=(batch,),
        in_specs=[pl.BlockSpec((4, DF_HD), lambda b: (0, 0)),
                  col(CG_DQ), col(CG_DK), col(CG_DV), col(CG_ZB),
                  pl.BlockSpec((1, DF_VD), lambda b: (0, 0)),
                  pl.BlockSpec((1, D_MODEL, past), lambda b: (b, 0, 0)),
                  pl.BlockSpec((1, past * DF_HEADS, DF_VD), lambda b: (b, 0, 0))],
        out_specs=pl.BlockSpec((seq, D_MODEL), lambda b: (b, 0)),
        out_shape=jax.ShapeDtypeStruct((batch * seq, D_MODEL), BF16),
        compiler_params=_params("arbitrary"),
        name="cached_diff_attn",
    )(lam_vecs, p, p, p, p, g_dsub.reshape(1, -1), kt_cache, v_cache.reshape(batch, past * DF_HEADS, DF_VD))


def _merge_kernel(x_ref, ya_ref, yb_ref, qm_ref, zm_ref, g0_ref, g1_ref, g2_ref, mk_ref, mv_ref,
                  wb_ref, wo_ref, y_ref, *, nb, n_mem):
    heads = [slice(h * MEM_HD, (h + 1) * MEM_HD) for h in range(MEM_HEADS)]
    sub = x_ref.shape[0] // nb
    parts = [(slice(j * sub, (j + 1) * sub), slice(j * n_mem, (j + 1) * n_mem)) for j in range(nb)]
    scores = [[_dot_nt(qm_ref[rows, sl], mk_ref[mem, sl]) for sl in heads]
              for rows, mem in parts]
    hsum = (g0_ref[...].astype(F32) * _dot(ya_ref[...], wb_ref[0])
            + g1_ref[...].astype(F32) * _dot(yb_ref[...], wb_ref[1]))
    ym = []
    for (rows, mem), part_scores in zip(parts, scores):
        ym_part = []
        for s, sl in zip(part_scores, heads):
            p = jnp.exp2(s - jnp.max(s, axis=-1, keepdims=True))
            rinv = 1.0 / jnp.sum(p, axis=-1, keepdims=True)
            o = _dot(p.astype(BF16), mv_ref[mem, sl]) * rinv
            ym_part.append((o * zm_ref[rows, sl].astype(F32)).astype(BF16))
        ym.append(jnp.concatenate(ym_part, axis=-1))
    ym = ym[0] if nb == 1 else jnp.concatenate(ym, axis=0)
    hsum = hsum + g2_ref[...].astype(F32) * _dot(ym, wb_ref[2])
    y_ref[...] = x_ref[...] + _dot(hsum.astype(BF16), wo_ref[...])


def _merge(x, ya, yb, p, mem_k, mem_v, w_branch, w_out, *, seq, n_mem):
    n = x.shape[0]
    tm = min(512, n)
    nb = max(1, tm // seq)
    per_b = max(1, seq // tm)
    row = lambda i: (i, 0)

    def col(cg):
        return pl.BlockSpec((tm, D_MODEL), lambda i: (i, cg))

    mem = pl.BlockSpec((nb * n_mem, D_MODEL), lambda i: (i // per_b, 0))
    return pl.pallas_call(
        functools.partial(_merge_kernel, nb=nb, n_mem=n_mem),
        grid=(n // tm,),
        in_specs=[pl.BlockSpec((tm, D_MODEL), row), pl.BlockSpec((tm, D_MODEL), row), pl.BlockSpec((tm, D_MODEL), row),
                  col(CG_MQ), col(CG_ZM), col(CG_G0), col(CG_G0 + 1), col(CG_G0 + 2), mem, mem,
                  pl.BlockSpec((N_BRANCH, D_MODEL, D_MODEL), lambda i: (0, 0, 0), pipeline_mode=pl.Buffered(1)),
                  pl.BlockSpec((D_MODEL, D_MODEL), lambda i: (0, 0), pipeline_mode=pl.Buffered(1))],
        out_specs=pl.BlockSpec((tm, D_MODEL), row),
        out_shape=jax.ShapeDtypeStruct((n, D_MODEL), F32),
        compiler_params=_params("arbitrary"),
        name="merge",
    )(x, ya, yb, p, p, p, p, p, mem_k, mem_v, w_branch, w_out)


def _layer(x, s0, past_k, past_v, mem_k, mem_v, lam_init, w, n_mem):
    batch, seq, _ = x.shape
    n = batch * seq
    xf = x.reshape(n, D_MODEL)
    prompt = past_k is None
    p, logf, k32, v32 = _in_proj(xf, w["g_norm"], w["w_in"], w["lb_logits"], w["g_dq"], w["g_dk"], w["g_mq"],
                                 seq=seq, k_transposed=prompt)
    ya, s_new = _hgrn(p, logf, s0, w["g_hg_out"], batch, seq)
    if prompt:
        yb = _diff_attn(w["lam_vecs"], p, w["g_dsub"], batch=batch, seq=seq, lam_init=lam_init)
    else:
        past = past_k.shape[1]
        kt_cache = jnp.transpose(past_k, (0, 2, 3, 4, 1)).reshape(batch, D_MODEL, past)
        yb = _cached_diff_attn(w["lam_vecs"], p, w["g_dsub"], kt_cache, past_v, batch=batch, seq=seq,
                               lam_init=lam_init)
    y = _merge(xf, ya, yb, p, mem_k, mem_v, w["w_branch"], w["w_out"], seq=seq, n_mem=n_mem)
    if prompt:
        k_out = jnp.transpose(k32.reshape(batch, DF_HEADS, 2, DF_HD, seq), (0, 4, 1, 2, 3))
    else:
        k_out = k32.reshape(batch, seq, DF_HEADS, 2, DF_HD)
    return y.reshape(batch, seq, D_MODEL), s_new, k_out, v32.reshape(batch, seq, DF_HEADS, DF_VD)


def kernel(x_prompt, x_sample, mem_prompt, cache_diff_k, cache_diff_v, cache_mem_k, cache_mem_v, state_hgrn,
           g_norm, w_in, hg_lb_logits, g_hg_out, g_dq, g_dk, lam_q1, lam_k1, lam_q2, lam_k2, g_dsub, g_mem,
           w_mkv, g_mq, g_mk, w_branch, w_out):
    depth = g_norm.shape[0]
    assert depth == 1 and hg_lb_logits.shape[0] == 2, "single-layer step only"
    batch, n_mem = mem_prompt.shape[:2]
    lam_init = 0.8 - 0.6
    w = dict(g_norm=g_norm[0], w_in=w_in[0].astype(BF16), lb_logits=hg_lb_logits, g_hg_out=g_hg_out[0],
             g_dq=g_dq[0], g_dk=g_dk[0], g_mq=g_mq[0], g_dsub=g_dsub[0],
             lam_vecs=jnp.stack([lam_q1[0], lam_k1[0], lam_q2[0], lam_k2[0]]),
             w_branch=w_branch[0].astype(BF16), w_out=w_out[0].astype(BF16))

    mem_k, mem_v, mem_kb, mem_vb = _memory_kv(mem_prompt.reshape(batch * n_mem, D_MODEL), g_mem[0], w_mkv[0], g_mk[0],
                                              batch=batch, n_mem=n_mem)
    s0 = jnp.zeros((batch, HG_HEADS, HG_D, HG_D), F32)
    y_p, s_p, k_p, v_p = _layer(x_prompt, s0, None, None, mem_kb, mem_vb, lam_init, w, n_mem)
    y_s, s_s, k_s, v_s = _layer(x_sample, state_hgrn[0], cache_diff_k[0], cache_diff_v[0],
                                cache_mem_k[0].reshape(-1, D_MODEL).astype(BF16),
                                cache_mem_v[0].reshape(-1, D_MODEL).astype(BF16), lam_init, w, n_mem)
    return (y_p, y_s, s_p[None], s_s[None], k_p[None], v_p[None], k_s[None], v_s[None], mem_k[None], mem_v[None])
```
